```python
import jax, jax.numpy as jnp
from jax import lax
import numpy as np

D_MODEL = 1024
BATCH = 8
SEQ = 2048
DEPTH = 1
DEC_BATCH = 128
DEC_SEQ = 8
PAST_LEN = 16384
PAGE_SIZE = 128

D_CONV_A = D_MODEL
D_CONV_B = D_MODEL
W_A = 3
W_B = 31
N_EXPERTS = 32
TOP_K = 4
D_FF = D_MODEL
SWIGLU_ALPHA = 1.702
SWIGLU_LIMIT = 7.0
MOE_BLOCK = 256
EPS = 1e-5
D_IN = 3 * D_CONV_A + 2 * D_CONV_B + 2 * D_MODEL
SPLITS = (D_CONV_A, 2 * D_CONV_A, 3 * D_CONV_A, 3 * D_CONV_A + D_CONV_B,
          3 * D_CONV_A + 2 * D_CONV_B, 3 * D_CONV_A + 2 * D_CONV_B + D_MODEL)

kernel_name = "gated_conv_conformer_moe_decode_step"


def rmsnorm(x, g):
    xf = x.astype(jnp.float32)
    y = xf * lax.rsqrt(jnp.mean(xf * xf, axis=-1, keepdims=True) + EPS)
    return (y * g.astype(jnp.float32)).astype(x.dtype)


def layernorm(x, g, b):
    xf = x.astype(jnp.float32)
    mu = jnp.mean(xf, axis=-1, keepdims=True)
    xc = xf - mu
    y = xc * lax.rsqrt(jnp.mean(xc * xc, axis=-1, keepdims=True) + EPS)
    return (y * g.astype(jnp.float32) + b.astype(jnp.float32)).astype(x.dtype)


def causal_dwconv(u, hist, w, b):
    width, ch = w.shape
    full = jnp.concatenate([hist.astype(u.dtype), u], axis=1)
    y = lax.conv_general_dilated(full, w[:, None, :].astype(u.dtype), window_strides=(1,),
                                 padding="VALID", dimension_numbers=("NWC", "WIO", "NWC"),
                                 feature_group_count=ch)
    return y + b.astype(u.dtype), full[:, full.shape[1] - (width - 1):, :]


def token_mixer(xn, hist_a, hist_b, w_in, b_gates, conv_a_w, conv_a_b, w_a_out,
                conv_b_w, conv_b_b, ln_b_g, ln_b_b, w_b_out, w_o):
    p = xn @ w_in
    bA, cA, hA, vB, gB, gate_a, gate_b = jnp.split(p, SPLITS, axis=-1)
    convA, new_a = causal_dwconv(cA * hA, hist_a, conv_a_w, conv_a_b)
    yA = (bA * convA) @ w_a_out
    convB, new_b = causal_dwconv(vB * jax.nn.sigmoid(gB), hist_b, conv_b_w, conv_b_b)
    yB = jax.nn.silu(layernorm(convB, ln_b_g, ln_b_b)) @ w_b_out
    merged = jax.nn.sigmoid(gate_a + b_gates[0]) * yA + jax.nn.sigmoid(gate_b + b_gates[1]) * yB
    return merged @ w_o, new_a, new_b


def moe(x, w_router, b_router, w_gu, b_gu, w_down, b_down):
    shp = x.shape
    xt = x.reshape(-1, D_MODEL)
    T = xt.shape[0]
    logits = (xt @ w_router + b_router).astype(jnp.float32)
    top_v, top_i = lax.top_k(logits, TOP_K)
    gate = jax.nn.softmax(top_v, axis=-1)
    n_slots = T * TOP_K
    flat_e = top_i.reshape(-1)
    order = jnp.argsort(flat_e)
    sorted_e = flat_e[order]
    counts = jnp.bincount(flat_e, length=N_EXPERTS)
    padded = (counts + MOE_BLOCK - 1) // MOE_BLOCK * MOE_BLOCK
    pad_end = jnp.cumsum(padded)
    pad_start = pad_end - padded
    start = jnp.cumsum(counts) - counts
    dest = pad_start[sorted_e] + jnp.arange(n_slots) - start[sorted_e]
    n_blocks = -(-n_slots // MOE_BLOCK) + N_EXPERTS
    n_rows = n_blocks * MOE_BLOCK
    row_tok = jnp.zeros((n_rows,), jnp.int32).at[dest].set((order // TOP_K).astype(jnp.int32))
    block_e = jnp.minimum(jnp.searchsorted(pad_end, jnp.arange(n_blocks) * MOE_BLOCK, side="right"),
                          N_EXPERTS - 1)
    xb = xt[row_tok].reshape(n_blocks, MOE_BLOCK, D_MODEL)

    def expert_block(args):
        xblk, e = args
        h = xblk @ w_gu[e] + b_gu[e]
        g = jnp.minimum(h[:, :D_FF], SWIGLU_LIMIT)
        u = jnp.clip(h[:, D_FF:], -SWIGLU_LIMIT, SWIGLU_LIMIT)
        act = (u + 1.0) * (g * jax.nn.sigmoid(SWIGLU_ALPHA * g))
        return act @ w_down[e] + b_down[e]

    yb = lax.map(expert_block, (xb, block_e)).reshape(n_rows, D_MODEL)
    y_slots = jnp.zeros((n_slots, D_MODEL), yb.dtype).at[order].set(yb[dest])
    y = jnp.einsum("tk,tkd->td", gate.astype(yb.dtype), y_slots.reshape(T, TOP_K, D_MODEL))
    return y.reshape(shp)


def setup_inputs(seed: int = 0) -> dict:
    key = jax.random.key(seed)
    ks = jax.random.split(key, 24)
    nrm = lambda k, s, sc: jax.random.normal(k, s, jnp.float32) * sc
    L = DEPTH
    return {
        "x_prompt": nrm(ks[0], (BATCH, SEQ, D_MODEL), 1.0),
        "x_sample": nrm(ks[1], (DEC_BATCH, DEC_SEQ, D_MODEL), 1.0),
        "state_conv_a": nrm(ks[2], (L, DEC_BATCH, W_A - 1, D_CONV_A), 1.0),
        "state_conv_b": nrm(ks[3], (L, DEC_BATCH, W_B - 1, D_CONV_B), 0.5),
        "norm_mix_g": 1.0 + nrm(ks[4], (L, D_MODEL), 0.02),
        "w_in": nrm(ks[5], (L, D_MODEL, D_IN), D_MODEL ** -0.5),
        "b_gates": nrm(ks[6], (L, 2, D_MODEL), 0.02),
        "conv_a_w": nrm(ks[7], (L, W_A, D_CONV_A), W_A ** -0.5),
        "conv_a_b": nrm(ks[8], (L, D_CONV_A), 0.02),
        "w_a_out": nrm(ks[9], (L, D_CONV_A, D_MODEL), D_CONV_A ** -0.5),
        "conv_b_w": nrm(ks[10], (L, W_B, D_CONV_B), W_B ** -0.5),
        "conv_b_b": nrm(ks[11], (L, D_CONV_B), 0.02),
        "ln_b_g": 1.0 + nrm(ks[12], (L, D_CONV_B), 0.02),
        "ln_b_b": nrm(ks[13], (L, D_CONV_B), 0.02),
        "w_b_out": nrm(ks[14], (L, D_CONV_B, D_MODEL), D_CONV_B ** -0.5),
        "w_o": nrm(ks[15], (L, D_MODEL, D_MODEL), D_MODEL ** -0.5),
        "norm_ffn_g": 1.0 + nrm(ks[16], (L, D_MODEL), 0.02),
        "w_router": nrm(ks[17], (L, D_MODEL, N_EXPERTS), D_MODEL ** -0.5),
        "b_router": nrm(ks[18], (L, N_EXPERTS), 0.01),
        "w_gu": nrm(ks[19], (L, N_EXPERTS, D_MODEL, 2 * D_FF), D_MODEL ** -0.5),
        "b_gu": nrm(ks[20], (L, N_EXPERTS, 2 * D_FF), 0.02),
        "w_down": nrm(ks[21], (L, N_EXPERTS, D_FF, D_MODEL), D_FF ** -0.5),
        "b_down": nrm(ks[22], (L, N_EXPERTS, D_MODEL), 0.02),
        "norm_final_g": 1.0 + nrm(ks[23], (D_MODEL,), 0.02),
    }


def reference(x_prompt, x_sample, state_conv_a, state_conv_b, norm_mix_g, w_in, b_gates,
              conv_a_w, conv_a_b, w_a_out, conv_b_w, conv_b_b, ln_b_g, ln_b_b, w_b_out, w_o,
              norm_ffn_g, w_router, b_router, w_gu, b_gu, w_down, b_down, norm_final_g):
    xp, xs = x_prompt, x_sample
    na_p, nb_p, na_s, nb_s = [], [], [], []
    for l in range(DEPTH):
        mix_params = (w_in[l], b_gates[l], conv_a_w[l], conv_a_b[l], w_a_out[l], conv_b_w[l],
                      conv_b_b[l], ln_b_g[l], ln_b_b[l], w_b_out[l], w_o[l])
        moe_params = (w_router[l], b_router[l], w_gu[l], b_gu[l], w_down[l], b_down[l])
        hist_a0 = jnp.zeros((xp.shape[0], W_A - 1, D_CONV_A), xp.dtype)
        hist_b0 = jnp.zeros((xp.shape[0], W_B - 1, D_CONV_B), xp.dtype)
        m, a_p, b_p = token_mixer(rmsnorm(xp, norm_mix_g[l]), hist_a0, hist_b0, *mix_params)
        xp = xp + m
        xp = xp + moe(rmsnorm(xp, norm_ffn_g[l]), *moe_params)
        m, a_s, b_s = token_mixer(rmsnorm(xs, norm_mix_g[l]), state_conv_a[l], state_conv_b[l],
                                  *mix_params)
        xs = xs + m
        xs = xs + moe(rmsnorm(xs, norm_ffn_g[l]), *moe_params)
        na_p.append(a_p); nb_p.append(b_p); na_s.append(a_s); nb_s.append(b_s)
    y_prompt = rmsnorm(xp, norm_final_g)
    y_sample = rmsnorm(xs, norm_final_g)
    new_conv_a_prompt = jnp.stack(na_p, axis=0)
    new_conv_b_prompt = jnp.stack(nb_p, axis=0)
    new_conv_a_sample = jnp.stack(na_s, axis=0)
    new_conv_b_sample = jnp.stack(nb_s, axis=0)
    return (y_prompt, y_sample, new_conv_a_prompt, new_conv_b_prompt, new_conv_a_sample, new_conv_b_sample)
```

```python
import functools

import jax
import jax.numpy as jnp
from jax import lax
from jax.experimental import pallas as pl
from jax.experimental.pallas import tpu as pltpu

EPS = 1e-5
SWIGLU_ALPHA = 1.702
SWIGLU_LIMIT = 7.0
TOP_K = 4
MOE_BLOCK = 256
LANES = 128
SUBLANES = 8
NEG_BIG = -1e30
VMEM_LIMIT = 60 * 1024 * 1024
CONV_OUT_BLOCK = 4

F32 = jnp.float32
BF16 = jnp.bfloat16


def _sigmoid(v):
    return 1.0 / (1.0 + jnp.exp(-v))


def _store_row_tiles(ref, value):
    n, d = value.shape
    pitch = d // LANES
    for c in range(pitch):
        ref[pl.ds(c, n, stride=pitch), :] = value[:, c * LANES:(c + 1) * LANES]


def _load_row_tiles(ref, n, d):
    pitch = d // LANES
    return jnp.concatenate([ref[pl.ds(c, n, stride=pitch), :] for c in range(pitch)], axis=1)


def _rms(v, g):
    return v * lax.rsqrt(jnp.mean(v * v, axis=-1, keepdims=True) + EPS) * g


def _causal_conv(ext_ref, w_ref, bias, n_out, width, n_seq):
    d = ext_ref.shape[-1]
    outs = []
    for t0 in range(0, n_out, CONV_OUT_BLOCK):
        nb = min(CONV_OUT_BLOCK, n_out - t0)
        for sg in range(n_seq // SUBLANES):
            rows = pl.ds(sg * SUBLANES, SUBLANES)
            loaded = {}
            acc = [None] * nb
            for k in range(width):
                wk = w_ref[k]
                for j in range(nb):
                    src = t0 + j + k
                    if src not in loaded:
                        loaded[src] = ext_ref[src, rows, :]
                    term = wk * loaded[src]
                    acc[j] = term if acc[j] is None else acc[j] + term
            outs.append((t0, sg, acc))
    slabs = {}
    for t0, sg, acc in outs:
        for j, a in enumerate(acc):
            slabs[(t0 + j, sg)] = a
    ordered = [slabs[(t, sg)] for t in range(n_out) for sg in range(n_seq // SUBLANES)]
    return jnp.concatenate(ordered, axis=0).reshape(n_out * n_seq, d) + bias


def _mixer_kernel(x_ref, ha_ref, hb_ref, cnt0_ref, gmix_ref, win_ref, bg_ref, caw_ref, cab_ref,
                  waout_ref, cbw_ref, cbb_ref, lng_ref, lnb_ref, wbout_ref, wo_ref, gffn_ref,
                  wrh_ref, wrl_ref, br_ref,
                  xmid_ref, xn2_ref, route_ref, gate_ref, newa_ref, newb_ref, cnt_ref,
                  exta, extb, cnt_acc, *, n_tt, w_a, w_b):
    j = pl.program_id(0)
    i = pl.program_id(1)
    tt, n_seq, d = x_ref.shape
    rows = tt * n_seq

    @pl.when(i == 0)
    def _():
        exta[0:w_a - 1] = ha_ref[...]
        extb[0:w_b - 1] = hb_ref[...]

    @pl.when((i == 0) & (j == 0))
    def _():
        cnt_acc[...] = cnt0_ref[...]

    x = x_ref[...].reshape(rows, d)
    xn = _rms(x, gmix_ref[...]).astype(BF16)

    def proj(g):
        return jnp.dot(xn, win_ref[:, g * d:(g + 1) * d], preferred_element_type=F32)

    exta[w_a - 1:w_a - 1 + tt] = (proj(1) * proj(2)).reshape(tt, n_seq, d)
    conv_a = _causal_conv(exta, caw_ref, cab_ref[...], tt, w_a, n_seq)
    y_a = jnp.dot((proj(0) * conv_a).astype(BF16), waout_ref[...], preferred_element_type=F32)

    extb[w_b - 1:w_b - 1 + tt] = (proj(3) * _sigmoid(proj(4))).reshape(tt, n_seq, d)
    conv_b = _causal_conv(extb, cbw_ref, cbb_ref[...], tt, w_b, n_seq)
    mu = jnp.mean(conv_b, axis=-1, keepdims=True)
    cen = conv_b - mu
    ln = cen * lax.rsqrt(jnp.mean(cen * cen, axis=-1, keepdims=True) + EPS) * lng_ref[...] + lnb_ref[...]
    y_b = jnp.dot((ln * _sigmoid(ln)).astype(BF16), wbout_ref[...], preferred_element_type=F32)

    merged = (_sigmoid(proj(5) + bg_ref[0:1, :]) * y_a + _sigmoid(proj(6) + bg_ref[1:2, :]) * y_b)
    x_mid = x + jnp.dot(merged.astype(BF16), wo_ref[...], preferred_element_type=F32)
    xmid_ref[...] = x_mid

    @pl.when(i == n_tt - 1)
    def _():
        newa_ref[...] = exta[tt:tt + w_a - 1]
        newb_ref[...] = extb[tt:tt + w_b - 1]

    if n_tt > 1:
        exta[0:w_a - 1] = exta[tt:tt + w_a - 1]
        extb[0:w_b - 1] = extb[tt:tt + w_b - 1]

    xn2 = _rms(x_mid, gffn_ref[...])
    _store_row_tiles(xn2_ref, xn2)
    x_hi = xn2.astype(BF16)
    x_lo = (xn2 - x_hi.astype(F32)).astype(BF16)
    logits = (jnp.dot(x_hi, wrh_ref[...], preferred_element_type=F32)
              + jnp.dot(x_lo, wrh_ref[...], preferred_element_type=F32)
              + jnp.dot(x_hi, wrl_ref[...], preferred_element_type=F32)) + br_ref[...]
    lane = lax.broadcasted_iota(jnp.int32, (rows, LANES), 1)
    work = logits
    top_v, top_i = [], []
    for _ in range(TOP_K):
        m = jnp.max(work, axis=-1, keepdims=True)
        idx = jnp.min(jnp.where(work == m, lane, LANES), axis=-1, keepdims=True)
        top_v.append(m)
        top_i.append(idx)
        work = jnp.where(lane == idx, -jnp.inf, work)
    ex = [jnp.exp(v - top_v[0]) for v in top_v]
    den = ex[0] + ex[1] + ex[2] + ex[3]

    onehot = jnp.zeros((rows, LANES), F32)
    for idx in top_i:
        onehot = onehot + (lane == idx).astype(F32)
    r_io = lax.broadcasted_iota(jnp.int32, (rows, rows), 0)
    c_io = lax.broadcasted_iota(jnp.int32, (rows, rows), 1)
    tri = (c_io < r_io).astype(BF16)
    prefix = jnp.dot(tri, onehot.astype(BF16), preferred_element_type=F32) + cnt_acc[...]
    route = jnp.zeros((rows, LANES), jnp.int32)
    gate = jnp.zeros((rows, LANES), F32)
    for k in range(TOP_K):
        pos = jnp.sum(jnp.where(lane == top_i[k], prefix, 0.0), axis=-1, keepdims=True)
        route = jnp.where(lane == k, top_i[k], route)
        route = jnp.where(lane == TOP_K + k, pos.astype(jnp.int32), route)
        gate = jnp.where(lane == k, ex[k] / den, gate)
    route_ref[...] = route
    gate_ref[...] = gate
    cnt_acc[...] = cnt_acc[...] + jnp.sum(onehot, axis=0, keepdims=True)
    cnt_ref[...] = cnt_acc[...]


def _const_spec(shape):
    nd = len(shape)
    return pl.BlockSpec(shape, lambda j, i, _nd=nd: (0,) * _nd, pipeline_mode=pl.Buffered(1))


def _mixer_call(x_tm, hist_a, hist_b, cnt0, params, *, n_sb, tt):
    total_t, n_seq, d = x_tm.shape
    nt = total_t // n_sb
    n_tt = nt // tt
    rows = tt * n_seq
    n_tok = total_t * n_seq
    w_a = params["caw"].shape[0]
    w_b = params["cbw"].shape[0]
    names = ["gmix", "win", "bg", "caw", "cab", "waout", "cbw", "cbb", "lng", "lnb", "wbout", "wo",
             "gffn", "wrh", "wrl", "br"]
    consts = [params[n] for n in names]
    tile = lambda j, i: (j * n_tt + i, 0)
    in_specs = [
        pl.BlockSpec((tt, n_seq, d), lambda j, i: (j * n_tt + i, 0, 0)),
        pl.BlockSpec((w_a - 1, n_seq, d), lambda j, i: (j, 0, 0)),
        pl.BlockSpec((w_b - 1, n_seq, d), lambda j, i: (j, 0, 0)),
        _const_spec(cnt0.shape),
    ] + [_const_spec(c.shape) for c in consts]
    out_shape = (
        jax.ShapeDtypeStruct((n_tok, d), F32),
        jax.ShapeDtypeStruct((n_tok * d // LANES, LANES), F32),
        jax.ShapeDtypeStruct((n_tok, LANES), jnp.int32),
        jax.ShapeDtypeStruct((n_tok, LANES), F32),
        jax.ShapeDtypeStruct((n_sb * (w_a - 1), n_seq, d), F32),
        jax.ShapeDtypeStruct((n_sb * (w_b - 1), n_seq, d), F32),
        jax.ShapeDtypeStruct((1, LANES), F32),
    )
    out_specs = (
        pl.BlockSpec((rows, d), tile),
        pl.BlockSpec((rows * d // LANES, LANES), tile),
        pl.BlockSpec((rows, LANES), tile),
        pl.BlockSpec((rows, LANES), tile),
        pl.BlockSpec((w_a - 1, n_seq, d), lambda j, i: (j, 0, 0)),
        pl.BlockSpec((w_b - 1, n_seq, d), lambda j, i: (j, 0, 0)),
        pl.BlockSpec((1, LANES), lambda j, i: (0, 0)),
    )
    return pl.pallas_call(
        functools.partial(_mixer_kernel, n_tt=n_tt, w_a=w_a, w_b=w_b),
        grid=(n_sb, n_tt),
        in_specs=in_specs,
        out_specs=out_specs,
        out_shape=out_shape,
        scratch_shapes=[pltpu.VMEM((tt + w_a - 1, n_seq, d), F32),
                        pltpu.VMEM((tt + w_b - 1, n_seq, d), F32),
                        pltpu.VMEM((1, LANES), F32)],
        compiler_params=pltpu.CompilerParams(dimension_semantics=("arbitrary", "arbitrary"),
                                             vmem_limit_bytes=VMEM_LIMIT),
        name="mixer_router",
    )(x_tm, hist_a, hist_b, cnt0, *consts)


def _dispatch_kernel(zero_ref, x_ref, dest_ref, xb_ref, zero_buf, sem, zsem, *, pitch):
    tm = x_ref.shape[0] // pitch
    n_zero = zero_ref.shape[0]
    blk_rows = zero_buf.shape[0]

    @pl.when(pl.program_id(0) == 0)
    def _():
        zero_buf[...] = jnp.zeros_like(zero_buf)

        def zcopy(e):
            start = pl.multiple_of(jnp.maximum(zero_ref[e], 0) * pitch, blk_rows)
            return pltpu.make_async_copy(zero_buf, xb_ref.at[pl.ds(start, blk_rows)], zsem)

        def start(e, c):
            @pl.when(zero_ref[e] >= 0)
            def _():
                zcopy(e).start()
            return c

        def wait(e, c):
            @pl.when(zero_ref[e] >= 0)
            def _():
                zcopy(e).wait()
            return c

        lax.fori_loop(0, n_zero, start, 0)
        lax.fori_loop(0, n_zero, wait, 0)

    def row_copy(r, dst):
        src = x_ref.at[pl.ds(pl.multiple_of(r * pitch, pitch), pitch)]
        return pltpu.make_async_copy(src, xb_ref.at[pl.ds(pl.multiple_of(dst * pitch, pitch), pitch)], sem)

    def start_rows(r, c):
        for k in range(TOP_K):
            row_copy(r, dest_ref[0, r * TOP_K + k]).start()
        return c

    def wait_rows(r, c):
        for k in range(TOP_K):
            row_copy(r, 0).wait()
        return c

    lax.fori_loop(0, tm, start_rows, 0)
    lax.fori_loop(0, tm, wait_rows, 0)


def _dispatch_call(xn2_tiles, dest, tail_start, n_rows, *, tm, d):
    pitch = d // LANES
    n_tiles = xn2_tiles.shape[0] // (tm * pitch)
    dest3 = dest.reshape(n_tiles, 1, tm * TOP_K)
    grid_spec = pltpu.PrefetchScalarGridSpec(
        num_scalar_prefetch=1,
        grid=(n_tiles,),
        in_specs=[pl.BlockSpec((tm * pitch, LANES), lambda i, tail: (i, 0)),
                  pl.BlockSpec((None, 1, tm * TOP_K), lambda i, tail: (i, 0, 0), memory_space=pltpu.SMEM)],
        out_specs=pl.BlockSpec(memory_space=pl.ANY),
        scratch_shapes=[pltpu.VMEM((MOE_BLOCK * pitch, LANES), F32), pltpu.SemaphoreType.DMA,
                        pltpu.SemaphoreType.DMA],
    )
    return pl.pallas_call(
        functools.partial(_dispatch_kernel, pitch=pitch),
        grid_spec=grid_spec,
        out_shape=jax.ShapeDtypeStruct((n_rows * pitch, LANES), F32),
        compiler_params=pltpu.CompilerParams(dimension_semantics=("arbitrary",)),
        name="moe_dispatch",
    )(tail_start, xn2_tiles, dest3)


def _expert_kernel(be_ref, nu_ref, x_ref, wgu_ref, bgu_ref, wd_ref, bd_ref, y_ref):
    @pl.when(pl.program_id(0) < nu_ref[0])
    def _():
        d_ff, d = wd_ref.shape
        x = _load_row_tiles(x_ref, MOE_BLOCK, d)
        h = jnp.dot(x.astype(BF16), wgu_ref[...], preferred_element_type=F32) + bgu_ref[...]
        g = jnp.minimum(h[:, :d_ff], SWIGLU_LIMIT)
        u = jnp.clip(h[:, d_ff:], -SWIGLU_LIMIT, SWIGLU_LIMIT)
        act = (u + 1.0) * (g * _sigmoid(SWIGLU_ALPHA * g))
        y = jnp.dot(act.astype(BF16), wd_ref[...], preferred_element_type=F32) + bd_ref[...]
        _store_row_tiles(y_ref, y)

    @pl.when(pl.program_id(0) >= nu_ref[0])
    def _():
        y_ref[...] = jnp.zeros_like(y_ref)


def _expert_call(xb_tiles, block_e, n_used, wgu, bgu, wd, bd):
    n_exp, d, two_ff = wgu.shape
    d_ff = wd.shape[1]
    blk_rows = MOE_BLOCK * d // LANES
    n_blocks = xb_tiles.shape[0] // blk_rows
    blk = lambda b, be, nu: (jnp.minimum(b, nu[0] - 1), 0)
    per_e = lambda b, be, nu: (be[b], 0, 0)
    grid_spec = pltpu.PrefetchScalarGridSpec(
        num_scalar_prefetch=2,
        grid=(n_blocks,),
        in_specs=[pl.BlockSpec((blk_rows, LANES), blk),
                  pl.BlockSpec((None, d, two_ff), per_e),
                  pl.BlockSpec((None, 1, two_ff), per_e),
                  pl.BlockSpec((None, d_ff, d), per_e),
                  pl.BlockSpec((None, 1, d), per_e)],
        out_specs=pl.BlockSpec((blk_rows, LANES), lambda b, be, nu: (b, 0)),
    )
    return pl.pallas_call(
        _expert_kernel,
        grid_spec=grid_spec,
        out_shape=jax.ShapeDtypeStruct(xb_tiles.shape, F32),
        compiler_params=pltpu.CompilerParams(dimension_semantics=("arbitrary",), vmem_limit_bytes=VMEM_LIMIT),
        name="moe_experts",
    )(block_e, n_used, xb_tiles, wgu, bgu.reshape(n_exp, 1, two_ff), wd, bd.reshape(n_exp, 1, d))


def _combine_kernel(xmid_ref, gate_ref, dest_ref, gfin_ref, yb_ref, out_ref, ybuf, sem):
    tm, d = xmid_ref.shape
    pitch = d // LANES

    def row_copy(r, k, src):
        return pltpu.make_async_copy(yb_ref.at[pl.ds(pl.multiple_of(src * pitch, pitch), pitch)],
                                     ybuf.at[k, pl.ds(pl.multiple_of(r * pitch, pitch), pitch)], sem)

    def start_rows(r, c):
        for k in range(TOP_K):
            row_copy(r, k, dest_ref[0, r * TOP_K + k]).start()
        return c

    def wait_rows(r, c):
        for k in range(TOP_K):
            row_copy(r, k, 0).wait()
        return c

    lax.fori_loop(0, tm, start_rows, 0)
    lax.fori_loop(0, tm, wait_rows, 0)
    gate = gate_ref[...]
    y = xmid_ref[...]
    for k in range(TOP_K):
        y = y + gate[:, k:k + 1] * _load_row_tiles(ybuf.at[k], tm, d)
    out_ref[...] = _rms(y, gfin_ref[...])


def _combine_call(x_mid, gate, dest, g_final, yb, *, tm):
    n_tok, d = x_mid.shape
    n_tiles = n_tok // tm
    dest3 = dest.reshape(n_tiles, 1, tm * TOP_K)
    return pl.pallas_call(
        _combine_kernel,
        grid=(n_tiles,),
        in_specs=[pl.BlockSpec((tm, d), lambda i: (i, 0)),
                  pl.BlockSpec((tm, LANES), lambda i: (i, 0)),
                  pl.BlockSpec((None, 1, tm * TOP_K), lambda i: (i, 0, 0), memory_space=pltpu.SMEM),
                  pl.BlockSpec((1, d), lambda i: (0, 0)),
                  pl.BlockSpec(memory_space=pl.ANY)],
        out_specs=pl.BlockSpec((tm, d), lambda i: (i, 0)),
        out_shape=jax.ShapeDtypeStruct((n_tok, d), F32),
        scratch_shapes=[pltpu.VMEM((TOP_K, tm * d // LANES, LANES), F32), pltpu.SemaphoreType.DMA],
        compiler_params=pltpu.CompilerParams(dimension_semantics=("arbitrary",)),
        name="moe_combine",
    )(x_mid, gate, dest3, g_final, yb)


def _to_time_major(x, seq_block):
    n_seqs, t, d = x.shape
    n_sb = n_seqs // seq_block
    return x.reshape(n_sb, seq_block, t, d).transpose(0, 2, 1, 3).reshape(n_sb * t, seq_block, d)


def _from_time_major(x, n_seqs, seq_block):
    d = x.shape[-1]
    n_sb = n_seqs // seq_block
    t = x.size // (n_seqs * d)
    return x.reshape(n_sb, t, seq_block, d).transpose(0, 2, 1, 3).reshape(n_seqs, t, d)


def _layer(xp, xs, state_a, state_b, p, norm_final_g, *, tt_prompt, seq_block_sample, tm_rows):
    n_p, t_p, d = xp.shape
    n_s, t_s, _ = xs.shape
    w_a = p["caw"].shape[0]
    w_b = p["cbw"].shape[0]
    n_exp = p["wgu"].shape[0]

    xp_tm = _to_time_major(xp, n_p)
    cnt0 = jnp.zeros((1, LANES), F32)
    (xmid_p, xn2_p, route_p, gate_p, newa_p, newb_p, cnt_p) = _mixer_call(
        xp_tm, jnp.zeros((w_a - 1, n_p, d), F32), jnp.zeros((w_b - 1, n_p, d), F32), cnt0, p,
        n_sb=1, tt=tt_prompt)
    xs_tm = _to_time_major(xs, seq_block_sample)
    (xmid_s, xn2_s, route_s, gate_s, newa_s, newb_s, cnt_all) = _mixer_call(
        xs_tm, _to_time_major(state_a, seq_block_sample), _to_time_major(state_b, seq_block_sample), cnt_p, p,
        n_sb=n_s // seq_block_sample, tt=t_s)

    x_mid = jnp.concatenate([xmid_p, xmid_s], axis=0)
    xn2 = jnp.concatenate([xn2_p, xn2_s], axis=0)
    route = jnp.concatenate([route_p, route_s], axis=0)
    gate = jnp.concatenate([gate_p, gate_s], axis=0)
    n_tok = x_mid.shape[0]

    counts = cnt_all[0, :n_exp].astype(jnp.int32)
    padded = (counts + MOE_BLOCK - 1) // MOE_BLOCK * MOE_BLOCK
    pad_end = jnp.cumsum(padded)
    pad_start = pad_end - padded
    dest = (pad_start[route[:, :TOP_K]] + route[:, TOP_K:2 * TOP_K]).reshape(-1)
    n_blocks = -(-(n_tok * TOP_K) // MOE_BLOCK) + n_exp
    n_used = (pad_end[-1] // MOE_BLOCK).astype(jnp.int32)
    blk_start = jnp.arange(n_blocks, dtype=jnp.int32) * MOE_BLOCK
    block_e = jnp.minimum(jnp.searchsorted(pad_end, jnp.minimum(blk_start, pad_end[-1] - 1), side="right"),
                          n_exp - 1).astype(jnp.int32)
    last_blocks = jnp.arange(n_blocks - n_exp, n_blocks, dtype=jnp.int32)
    zero_start = jnp.concatenate([jnp.where(padded > 0, pad_end - MOE_BLOCK, -1),
                                  jnp.where(last_blocks >= n_used, last_blocks * MOE_BLOCK, -1)]).astype(jnp.int32)

    xb = _dispatch_call(xn2, dest, zero_start, n_blocks * MOE_BLOCK, tm=tm_rows, d=d)
    yb = _expert_call(xb, block_e, n_used.reshape(1), p["wgu"], p["bgu"], p["wd"], p["bd"])
    y = _combine_call(x_mid, gate, dest, norm_final_g.reshape(1, d), yb, tm=tm_rows)

    n_tok_p = n_p * t_p
    y_prompt = _from_time_major(y[:n_tok_p], n_p, n_p)
    y_sample = _from_time_major(y[n_tok_p:], n_s, seq_block_sample)
    new_a_p = _from_time_major(newa_p, n_p, n_p)
    new_b_p = _from_time_major(newb_p, n_p, n_p)
    new_a_s = _from_time_major(newa_s, n_s, seq_block_sample)
    new_b_s = _from_time_major(newb_s, n_s, seq_block_sample)
    return y_prompt, y_sample, new_a_p, new_b_p, new_a_s, new_b_s


def _prep_params(l, norm_mix_g, w_in, b_gates, conv_a_w, conv_a_b, w_a_out, conv_b_w, conv_b_b, ln_b_g,
                 ln_b_b, w_b_out, w_o, norm_ffn_g, w_router, b_router, w_gu, b_gu, w_down, b_down):
    d = w_in.shape[1]
    n_exp = w_router.shape[-1]
    row = lambda v: v.reshape(1, -1)
    taps = lambda w: jnp.broadcast_to(w[:, None, :], (w.shape[0], SUBLANES, w.shape[1]))
    wr = jnp.zeros((d, LANES), F32).at[:, :n_exp].set(w_router[l])
    wr_hi = wr.astype(BF16)
    br = jnp.full((1, LANES), NEG_BIG, F32).at[0, :n_exp].set(b_router[l])
    return dict(
        gmix=row(norm_mix_g[l]), win=w_in[l].astype(BF16), bg=b_gates[l],
        caw=taps(conv_a_w[l]), cab=row(conv_a_b[l]), waout=w_a_out[l].astype(BF16),
        cbw=taps(conv_b_w[l]), cbb=row(conv_b_b[l]), lng=row(ln_b_g[l]), lnb=row(ln_b_b[l]),
        wbout=w_b_out[l].astype(BF16), wo=w_o[l].astype(BF16), gffn=row(norm_ffn_g[l]),
        wrh=wr_hi, wrl=(wr - wr_hi.astype(F32)).astype(BF16), br=br,
        wgu=w_gu[l].astype(BF16), bgu=b_gu[l], wd=w_down[l].astype(BF16), bd=b_down[l])


def kernel(x_prompt, x_sample, state_conv_a, state_conv_b, norm_mix_g, w_in, b_gates, conv_a_w, conv_a_b, w_a_out, conv_b_w, conv_b_b, ln_b_g, ln_b_b, w_b_out, w_o, norm_ffn_g, w_router, b_router, w_gu, b_gu, w_down, b_down, norm_final_g):
    depth = w_in.shape[0]
    assert depth == 1, "the final norm is fused into the last layer's combine call"
    p = _prep_params(0, norm_mix_g, w_in, b_gates, conv_a_w, conv_a_b, w_a_out, conv_b_w, conv_b_b, ln_b_g,
                     ln_b_b, w_b_out, w_o, norm_ffn_g, w_router, b_router, w_gu, b_gu, w_down, b_down)
    n_p = x_prompt.shape[0]
    y_p, y_s, na_p, nb_p, na_s, nb_s = _layer(
        x_prompt, x_sample, state_conv_a[0], state_conv_b[0], p, norm_final_g,
        tt_prompt=MOE_BLOCK // n_p, seq_block_sample=MOE_BLOCK // x_sample.shape[1], tm_rows=MOE_BLOCK)
    return (y_p, y_s, na_p[None], nb_p[None], na_s[None], nb_s[None])
```

```python
import functools

import jax
import jax.numpy as jnp
from jax import lax
from jax.experimental import pallas as pl
from jax.experimental.pallas import tpu as pltpu

EPS = 1e-5
SWIGLU_ALPHA = 1.702
SWIGLU_LIMIT = 7.0
TOP_K = 4
MOE_BLOCK = 256
LANES = 128
SUBLANES = 8
NEG_BIG = -1e30
VMEM_LIMIT = 60 * 1024 * 1024
CONV_OUT_BLOCK = 4

F32 = jnp.float32
BF16 = jnp.bfloat16


def _sigmoid(v):
    return 1.0 / (1.0 + jnp.exp(-v))


def _store_row_tiles(ref, value):
    n, d = value.shape
    pitch = d // LANES
    for c in range(pitch):
        ref[pl.ds(c, n, stride=pitch), :] = value[:, c * LANES:(c + 1) * LANES]


def _load_row_tiles(ref, n, d):
    pitch = d // LANES
    return jnp.concatenate([ref[pl.ds(c, n, stride=pitch), :] for c in range(pitch)], axis=1)


def _rms(v, g):
    return v * lax.rsqrt(jnp.mean(v * v, axis=-1, keepdims=True) + EPS) * g


def _causal_conv(ext_ref, w_ref, bias, n_out, width, n_seq):
    d = ext_ref.shape[-1]
    slabs = {}
    for t0 in range(0, n_out, CONV_OUT_BLOCK):
        nb = min(CONV_OUT_BLOCK, n_out - t0)
        for sg in range(n_seq // SUBLANES):
            rows = pl.ds(sg * SUBLANES, SUBLANES)
            loaded = {}
            acc = [None] * nb
            for k in range(width):
                wk = w_ref[k]
                for j in range(nb):
                    src = t0 + j + k
                    if src not in loaded:
                        loaded[src] = ext_ref[src, rows, :]
                    term = wk * loaded[src]
                    acc[j] = term if acc[j] is None else acc[j] + term
            for j, a in enumerate(acc):
                slabs[(t0 + j, sg)] = a
    ordered = [slabs[(t, sg)] for t in range(n_out) for sg in range(n_seq // SUBLANES)]
    return jnp.concatenate(ordered, axis=0).reshape(n_out * n_seq, d) + bias


_MIXER_CONSTS = ("gmix", "win", "bg", "caw", "cab", "waout", "cbw", "cbb", "lng", "lnb", "wbout", "wo",
                 "gffn", "wrh", "wrl", "br")


def _mixer_tile(x_ref, exta, extb, c, xmid_ref, xn2_ref, route_ref, gate_ref, cnt_ref, cnt_acc, *, carry):
    tt, n_seq, d = x_ref.shape
    rows = tt * n_seq
    w_a = exta.shape[0] - tt + 1
    w_b = extb.shape[0] - tt + 1

    x = x_ref[...].reshape(rows, d)
    xn = _rms(x, c["gmix"][...]).astype(BF16)

    def proj(g):
        return jnp.dot(xn, c["win"][:, g * d:(g + 1) * d], preferred_element_type=F32)

    exta[w_a - 1:w_a - 1 + tt] = (proj(1) * proj(2)).reshape(tt, n_seq, d)
    conv_a = _causal_conv(exta, c["caw"], c["cab"][...], tt, w_a, n_seq)
    y_a = jnp.dot((proj(0) * conv_a).astype(BF16), c["waout"][...], preferred_element_type=F32)

    extb[w_b - 1:w_b - 1 + tt] = (proj(3) * _sigmoid(proj(4))).reshape(tt, n_seq, d)
    conv_b = _causal_conv(extb, c["cbw"], c["cbb"][...], tt, w_b, n_seq)
    mu = jnp.mean(conv_b, axis=-1, keepdims=True)
    cen = conv_b - mu
    ln = cen * lax.rsqrt(jnp.mean(cen * cen, axis=-1, keepdims=True) + EPS) * c["lng"][...] + c["lnb"][...]
    y_b = jnp.dot((ln * _sigmoid(ln)).astype(BF16), c["wbout"][...], preferred_element_type=F32)

    bg = c["bg"]
    merged = _sigmoid(proj(5) + bg[0:1, :]) * y_a + _sigmoid(proj(6) + bg[1:2, :]) * y_b
    x_mid = x + jnp.dot(merged.astype(BF16), c["wo"][...], preferred_element_type=F32)
    xmid_ref[...] = x_mid

    if carry:
        exta[0:w_a - 1] = exta[tt:tt + w_a - 1]
        extb[0:w_b - 1] = extb[tt:tt + w_b - 1]

    xn2 = _rms(x_mid, c["gffn"][...])
    _store_row_tiles(xn2_ref, xn2)
    x_hi = xn2.astype(BF16)
    x_lo = (xn2 - x_hi.astype(F32)).astype(BF16)
    logits = (jnp.dot(x_hi, c["wrh"][...], preferred_element_type=F32)
              + jnp.dot(x_lo, c["wrh"][...], preferred_element_type=F32)
              + jnp.dot(x_hi, c["wrl"][...], preferred_element_type=F32)) + c["br"][...]
    lane = lax.broadcasted_iota(jnp.int32, (rows, LANES), 1)
    work = logits
    top_v, top_i = [], []
    for _ in range(TOP_K):
        m = jnp.max(work, axis=-1, keepdims=True)
        idx = jnp.min(jnp.where(work == m, lane, LANES), axis=-1, keepdims=True)
        top_v.append(m)
        top_i.append(idx)
        work = jnp.where(lane == idx, -jnp.inf, work)
    ex = [jnp.exp(v - top_v[0]) for v in top_v]
    den = ex[0] + ex[1] + ex[2] + ex[3]

    onehot = jnp.zeros((rows, LANES), F32)
    for idx in top_i:
        onehot = onehot + (lane == idx).astype(F32)
    r_io = lax.broadcasted_iota(jnp.int32, (rows, rows), 0)
    c_io = lax.broadcasted_iota(jnp.int32, (rows, rows), 1)
    tri = (c_io < r_io).astype(BF16)
    prefix = jnp.dot(tri, onehot.astype(BF16), preferred_element_type=F32) + cnt_acc[...]
    route = jnp.zeros((rows, LANES), F32)
    gate = jnp.zeros((rows, LANES), F32)
    for k in range(TOP_K):
        pos = jnp.sum(jnp.where(lane == top_i[k], prefix, 0.0), axis=-1, keepdims=True)
        route = jnp.where(lane == k, top_i[k].astype(F32), route)
        route = jnp.where(lane == TOP_K + k, pos, route)
        gate = jnp.where(lane == k, ex[k] / den, gate)
    route_ref[...] = jnp.transpose(route)[:2 * TOP_K].astype(jnp.int32)
    gate_ref[...] = gate
    cnt_acc[...] = cnt_acc[...] + jnp.sum(onehot, axis=0, keepdims=True)
    cnt_ref[...] = cnt_acc[...]


def _mixer_kernel(*refs, n_tiles_p):
    n_c = len(_MIXER_CONSTS)
    xp_ref, xs_ref, hsa_ref, hsb_ref = refs[:4]
    c = dict(zip(_MIXER_CONSTS, refs[4:4 + n_c]))
    (xmid_ref, xn2_ref, route_ref, gate_ref, cnt_ref,
     newa_p_ref, newb_p_ref, newa_s_ref, newb_s_ref) = refs[4 + n_c:13 + n_c]
    exta_p, extb_p, exta_s, extb_s, cnt_acc, sem = refs[13 + n_c:]
    i = pl.program_id(0)
    tt_p = xp_ref.shape[0]
    tt_s = xs_ref.shape[0]
    w_a = exta_p.shape[0] - tt_p + 1
    w_b = extb_p.shape[0] - tt_p + 1
    outs = (xmid_ref, xn2_ref, route_ref, gate_ref, cnt_ref, cnt_acc)

    @pl.when(i == 0)
    def _():
        exta_p[0:w_a - 1] = jnp.zeros((w_a - 1,) + exta_p.shape[1:], F32)
        extb_p[0:w_b - 1] = jnp.zeros((w_b - 1,) + extb_p.shape[1:], F32)
        cnt_acc[...] = jnp.zeros_like(cnt_acc)

    @pl.when(i < n_tiles_p)
    def _():
        _mixer_tile(xp_ref, exta_p, extb_p, c, *outs, carry=True)

        @pl.when(i == n_tiles_p - 1)
        def _():
            cp_a = pltpu.make_async_copy(exta_p.at[pl.ds(0, w_a - 1)], newa_p_ref, sem.at[0])
            cp_b = pltpu.make_async_copy(extb_p.at[pl.ds(0, w_b - 1)], newb_p_ref, sem.at[1])
            cp_a.start()
            cp_b.start()
            cp_a.wait()
            cp_b.wait()

    @pl.when(i >= n_tiles_p)
    def _():
        q = i - n_tiles_p
        in_a = pltpu.make_async_copy(hsa_ref.at[pl.ds(q * (w_a - 1), w_a - 1)], exta_s.at[pl.ds(0, w_a - 1)],
                                     sem.at[0])
        in_b = pltpu.make_async_copy(hsb_ref.at[pl.ds(q * (w_b - 1), w_b - 1)], extb_s.at[pl.ds(0, w_b - 1)],
                                     sem.at[1])
        in_a.start()
        in_b.start()
        in_a.wait()
        in_b.wait()
        _mixer_tile(xs_ref, exta_s, extb_s, c, *outs, carry=False)
        out_a = pltpu.make_async_copy(exta_s.at[pl.ds(tt_s, w_a - 1)], newa_s_ref.at[pl.ds(q * (w_a - 1), w_a - 1)],
                                      sem.at[0])
        out_b = pltpu.make_async_copy(extb_s.at[pl.ds(tt_s, w_b - 1)], newb_s_ref.at[pl.ds(q * (w_b - 1), w_b - 1)],
                                      sem.at[1])
        out_a.start()
        out_b.start()
        out_a.wait()
        out_b.wait()


def _mixer_call(xp_tm, xs_tm, hs_a, hs_b, params, *, tt_p, tt_s, w_a, w_b):
    t_p, n_p, d = xp_tm.shape
    sb = xs_tm.shape[1]
    rows = tt_p * n_p
    assert rows == tt_s * sb == MOE_BLOCK
    n_tiles_p = t_p // tt_p
    n_tiles_s = xs_tm.shape[0] // tt_s
    n_tiles = n_tiles_p + n_tiles_s
    n_tok = n_tiles * rows
    pitch = d // LANES
    consts = [params[n] for n in _MIXER_CONSTS]
    const_spec = lambda a: pl.BlockSpec(a.shape, lambda i, _nd=a.ndim: (0,) * _nd, pipeline_mode=pl.Buffered(1))
    any_spec = pl.BlockSpec(memory_space=pl.ANY)
    in_specs = [
        pl.BlockSpec((tt_p, n_p, d), lambda i: (jnp.minimum(i, n_tiles_p - 1), 0, 0)),
        pl.BlockSpec((tt_s, sb, d), lambda i: (jnp.maximum(i - n_tiles_p, 0), 0, 0)),
        any_spec, any_spec,
    ] + [const_spec(a) for a in consts]
    out_shape = (
        jax.ShapeDtypeStruct((n_tok, d), F32),
        jax.ShapeDtypeStruct((n_tok * pitch, LANES), F32),
        jax.ShapeDtypeStruct((2 * TOP_K, n_tok), jnp.int32),
        jax.ShapeDtypeStruct((n_tok, LANES), F32),
        jax.ShapeDtypeStruct((1, LANES), F32),
        jax.ShapeDtypeStruct((w_a - 1, n_p, d), F32),
        jax.ShapeDtypeStruct((w_b - 1, n_p, d), F32),
        jax.ShapeDtypeStruct(hs_a.shape, F32),
        jax.ShapeDtypeStruct(hs_b.shape, F32),
    )
    out_specs = (
        pl.BlockSpec((rows, d), lambda i: (i, 0)),
        pl.BlockSpec((rows * pitch, LANES), lambda i: (i, 0)),
        pl.BlockSpec((2 * TOP_K, rows), lambda i: (0, i)),
        pl.BlockSpec((rows, LANES), lambda i: (i, 0)),
        pl.BlockSpec((1, LANES), lambda i: (0, 0)),
        any_spec, any_spec, any_spec, any_spec,
    )
    return pl.pallas_call(
        functools.partial(_mixer_kernel, n_tiles_p=n_tiles_p),
        grid=(n_tiles,),
        in_specs=in_specs,
        out_specs=out_specs,
        out_shape=out_shape,
        scratch_shapes=[pltpu.VMEM((tt_p + w_a - 1, n_p, d), F32),
                        pltpu.VMEM((tt_p + w_b - 1, n_p, d), F32),
                        pltpu.VMEM((tt_s + w_a - 1, sb, d), F32),
                        pltpu.VMEM((tt_s + w_b - 1, sb, d), F32),
                        pltpu.VMEM((1, LANES), F32),
                        pltpu.SemaphoreType.DMA((2,))],
        compiler_params=pltpu.CompilerParams(dimension_semantics=("arbitrary",), vmem_limit_bytes=VMEM_LIMIT),
        name="mixer_router",
    )(xp_tm, xs_tm, hs_a, hs_b, *consts)


def _dispatch_kernel(zero_ref, x_ref, dest_ref, xb_ref, zero_buf, sem, zsem, *, pitch):
    tm = x_ref.shape[0] // pitch
    n_zero = zero_ref.shape[0]
    blk_rows = zero_buf.shape[0]

    @pl.when(pl.program_id(0) == 0)
    def _():
        zero_buf[...] = jnp.zeros_like(zero_buf)

        def zcopy(e):
            start = pl.multiple_of(jnp.maximum(zero_ref[e], 0) * pitch, blk_rows)
            return pltpu.make_async_copy(zero_buf, xb_ref.at[pl.ds(start, blk_rows)], zsem)

        def start(e, c):
            @pl.when(zero_ref[e] >= 0)
            def _():
                zcopy(e).start()
            return c

        def wait(e, c):
            @pl.when(zero_ref[e] >= 0)
            def _():
                zcopy(e).wait()
            return c

        lax.fori_loop(0, n_zero, start, 0)
        lax.fori_loop(0, n_zero, wait, 0)

    def row_copy(r, dst):
        src = x_ref.at[pl.ds(pl.multiple_of(r * pitch, pitch), pitch)]
        return pltpu.make_async_copy(src, xb_ref.at[pl.ds(pl.multiple_of(dst * pitch, pitch), pitch)], sem)

    def start_rows(r, c):
        for k in range(TOP_K):
            row_copy(r, dest_ref[0, k * tm + r]).start(priority=k % 2)
        return c

    def wait_rows(r, c):
        for k in range(TOP_K):
            row_copy(r, 0).wait()
        return c

    lax.fori_loop(0, tm, start_rows, 0)
    lax.fori_loop(0, tm, wait_rows, 0)


def _dispatch_call(xn2_tiles, dest_tiles, zero_start, n_rows, *, tm, d):
    pitch = d // LANES
    n_tiles = xn2_tiles.shape[0] // (tm * pitch)
    grid_spec = pltpu.PrefetchScalarGridSpec(
        num_scalar_prefetch=1,
        grid=(n_tiles,),
        in_specs=[pl.BlockSpec((tm * pitch, LANES), lambda i, z: (i, 0)),
                  pl.BlockSpec((None, 1, tm * TOP_K), lambda i, z: (i, 0, 0), memory_space=pltpu.SMEM)],
        out_specs=pl.BlockSpec(memory_space=pl.ANY),
        scratch_shapes=[pltpu.VMEM((MOE_BLOCK * pitch, LANES), F32), pltpu.SemaphoreType.DMA,
                        pltpu.SemaphoreType.DMA],
    )
    return pl.pallas_call(
        functools.partial(_dispatch_kernel, pitch=pitch),
        grid_spec=grid_spec,
        out_shape=jax.ShapeDtypeStruct((n_rows * pitch, LANES), F32),
        compiler_params=pltpu.CompilerParams(dimension_semantics=("arbitrary",)),
        name="moe_dispatch",
    )(zero_start, xn2_tiles, dest_tiles)


def _expert_kernel(be_ref, nu_ref, x_ref, wgu_ref, bgu_ref, wd_ref, bd_ref, y_ref, wgu_bf, wd_bf):
    b = pl.program_id(0)

    @pl.when(b < nu_ref[0])
    def _():
        d_ff, d = wd_ref.shape

        @pl.when((b == 0) | (be_ref[b] != be_ref[jnp.maximum(b - 1, 0)]))
        def _():
            wgu_bf[...] = wgu_ref[...].astype(BF16)
            wd_bf[...] = wd_ref[...].astype(BF16)

        x = _load_row_tiles(x_ref, MOE_BLOCK, d)
        h = jnp.dot(x.astype(BF16), wgu_bf[...], preferred_element_type=F32) + bgu_ref[...]
        g = jnp.minimum(h[:, :d_ff], SWIGLU_LIMIT)
        u = jnp.clip(h[:, d_ff:], -SWIGLU_LIMIT, SWIGLU_LIMIT)
        act = (u + 1.0) * (g * _sigmoid(SWIGLU_ALPHA * g))
        y = jnp.dot(act.astype(BF16), wd_bf[...], preferred_element_type=F32) + bd_ref[...]
        _store_row_tiles(y_ref, y)

    @pl.when(b >= nu_ref[0])
    def _():
        y_ref[...] = jnp.zeros_like(y_ref)


def _expert_call(xb_tiles, block_e, n_used, wgu, bgu, wd, bd):
    n_exp, d, two_ff = wgu.shape
    d_ff = wd.shape[1]
    blk_rows = MOE_BLOCK * d // LANES
    n_blocks = xb_tiles.shape[0] // blk_rows
    per_e = lambda b, be, nu: (be[b], 0, 0)
    grid_spec = pltpu.PrefetchScalarGridSpec(
        num_scalar_prefetch=2,
        grid=(n_blocks,),
        in_specs=[pl.BlockSpec((blk_rows, LANES), lambda b, be, nu: (jnp.minimum(b, nu[0] - 1), 0)),
                  pl.BlockSpec((None, d, two_ff), per_e),
                  pl.BlockSpec((None, 1, two_ff), per_e),
                  pl.BlockSpec((None, d_ff, d), per_e),
                  pl.BlockSpec((None, 1, d), per_e)],
        out_specs=pl.BlockSpec((blk_rows, LANES), lambda b, be, nu: (b, 0)),
        scratch_shapes=[pltpu.VMEM((d, two_ff), BF16), pltpu.VMEM((d_ff, d), BF16)],
    )
    return pl.pallas_call(
        _expert_kernel,
        grid_spec=grid_spec,
        out_shape=jax.ShapeDtypeStruct(xb_tiles.shape, F32),
        compiler_params=pltpu.CompilerParams(dimension_semantics=("arbitrary",), vmem_limit_bytes=VMEM_LIMIT),
        name="moe_experts",
    )(block_e, n_used, xb_tiles, wgu, bgu.reshape(n_exp, 1, two_ff), wd, bd.reshape(n_exp, 1, d))


def _combine_kernel(xmid_ref, gate_ref, dest_ref, gfin_ref, yb_ref, outp_ref, outs_ref, ybuf, sem, *, n_tiles_p):
    tm, d = xmid_ref.shape
    pitch = d // LANES

    def row_copy(r, k, src):
        return pltpu.make_async_copy(yb_ref.at[pl.ds(pl.multiple_of(src * pitch, pitch), pitch)],
                                     ybuf.at[k, pl.ds(pl.multiple_of(r * pitch, pitch), pitch)], sem)

    def start_rows(r, c):
        for k in range(TOP_K):
            row_copy(r, k, dest_ref[0, k * tm + r]).start(priority=k % 2)
        return c

    def wait_rows(r, c):
        for k in range(TOP_K):
            row_copy(r, k, 0).wait()
        return c

    lax.fori_loop(0, tm, start_rows, 0)
    lax.fori_loop(0, tm, wait_rows, 0)
    gate = gate_ref[...]
    y = xmid_ref[...]
    for k in range(TOP_K):
        y = y + gate[:, k:k + 1] * _load_row_tiles(ybuf.at[k], tm, d)
    out = _rms(y, gfin_ref[...])

    @pl.when(pl.program_id(0) < n_tiles_p)
    def _():
        outp_ref[...] = out

    @pl.when(pl.program_id(0) >= n_tiles_p)
    def _():
        outs_ref[...] = out


def _combine_call(x_mid, gate, dest_tiles, g_final, yb, *, tm, n_tiles_p):
    n_tok, d = x_mid.shape
    n_tiles = n_tok // tm
    return pl.pallas_call(
        functools.partial(_combine_kernel, n_tiles_p=n_tiles_p),
        grid=(n_tiles,),
        in_specs=[pl.BlockSpec((tm, d), lambda i: (i, 0)),
                  pl.BlockSpec((tm, LANES), lambda i: (i, 0)),
                  pl.BlockSpec((None, 1, tm * TOP_K), lambda i: (i, 0, 0), memory_space=pltpu.SMEM),
                  pl.BlockSpec((1, d), lambda i: (0, 0)),
                  pl.BlockSpec(memory_space=pl.ANY)],
        out_specs=(pl.BlockSpec((tm, d), lambda i: (jnp.minimum(i, n_tiles_p - 1), 0)),
                   pl.BlockSpec((tm, d), lambda i: (jnp.maximum(i - n_tiles_p, 0), 0))),
        out_shape=(jax.ShapeDtypeStruct((n_tiles_p * tm, d), F32),
                   jax.ShapeDtypeStruct(((n_tiles - n_tiles_p) * tm, d), F32)),
        scratch_shapes=[pltpu.VMEM((TOP_K, tm * d // LANES, LANES), F32), pltpu.SemaphoreType.DMA],
        compiler_params=pltpu.CompilerParams(dimension_semantics=("arbitrary",)),
        name="moe_combine",
    )(x_mid, gate, dest_tiles, g_final, yb)


def _to_time_major(x, seq_block):
    n_seqs, t, d = x.shape
    n_sb = n_seqs // seq_block
    return x.reshape(n_sb, seq_block, t, d).transpose(0, 2, 1, 3).reshape(n_sb * t, seq_block, d)


def _from_time_major(x, n_seqs, seq_block):
    d = x.shape[-1]
    n_sb = n_seqs // seq_block
    t = x.size // (n_seqs * d)
    return x.reshape(n_sb, t, seq_block, d).transpose(0, 2, 1, 3).reshape(n_seqs, t, d)


def _layer(xp, xs, state_a, state_b, p, norm_final_g):
    n_p, t_p, d = xp.shape
    n_s, t_s, _ = xs.shape
    w_a = p["caw"].shape[0]
    w_b = p["cbw"].shape[0]
    n_exp = p["wgu"].shape[0]
    tm = MOE_BLOCK
    sb = tm // t_s
    n_tiles_p = n_p * t_p // tm

    (x_mid, xn2, route, gate, cnt, newa_p, newb_p, newa_s, newb_s) = _mixer_call(
        _to_time_major(xp, n_p), _to_time_major(xs, sb), _to_time_major(state_a, sb), _to_time_major(state_b, sb),
        p, tt_p=tm // n_p, tt_s=t_s, w_a=w_a, w_b=w_b)
    n_tok = x_mid.shape[0]
    n_tiles = n_tok // tm

    counts = cnt[0, :n_exp].astype(jnp.int32)
    padded = (counts + MOE_BLOCK - 1) // MOE_BLOCK * MOE_BLOCK
    pad_end = jnp.cumsum(padded)
    pad_start = pad_end - padded
    dest = pad_start[route[:TOP_K]] + route[TOP_K:]
    dest_tiles = dest.reshape(TOP_K, n_tiles, tm).transpose(1, 0, 2).reshape(n_tiles, 1, TOP_K * tm)
    n_blocks = -(-(n_tok * TOP_K) // MOE_BLOCK) + n_exp
    n_used = (pad_end[-1] // MOE_BLOCK).astype(jnp.int32)
    blk_start = jnp.minimum(jnp.arange(n_blocks, dtype=jnp.int32) * MOE_BLOCK, pad_end[-1] - 1)
    block_e = jnp.minimum(jnp.sum(blk_start[:, None] >= pad_end[None, :], axis=1), n_exp - 1).astype(jnp.int32)
    last_blocks = jnp.arange(n_blocks - n_exp, n_blocks, dtype=jnp.int32)
    zero_start = jnp.concatenate([jnp.where(padded > 0, pad_end - MOE_BLOCK, -1),
                                  jnp.where(last_blocks >= n_used, last_blocks * MOE_BLOCK, -1)]).astype(jnp.int32)

    xb = _dispatch_call(xn2, dest_tiles, zero_start, n_blocks * MOE_BLOCK, tm=tm, d=d)
    yb = _expert_call(xb, block_e, n_used.reshape(1), p["wgu"], p["bgu"], p["wd"], p["bd"])
    y_p, y_s = _combine_call(x_mid, gate, dest_tiles, norm_final_g.reshape(1, d), yb, tm=tm, n_tiles_p=n_tiles_p)

    return (_from_time_major(y_p, n_p, n_p), _from_time_major(y_s, n_s, sb),
            _from_time_major(newa_p, n_p, n_p), _from_time_major(newb_p, n_p, n_p),
            _from_time_major(newa_s, n_s, sb), _from_time_major(newb_s, n_s, sb))


def _prep_params(l, norm_mix_g, w_in, b_gates, conv_a_w, conv_a_b, w_a_out, conv_b_w, conv_b_b, ln_b_g,
                 ln_b_b, w_b_out, w_o, norm_ffn_g, w_router, b_router, w_gu, b_gu, w_down, b_down):
    d = w_in.shape[1]
    n_exp = w_router.shape[-1]
    row = lambda v: v.reshape(1, -1)
    taps = lambda w: jnp.broadcast_to(w[:, None, :], (w.shape[0], SUBLANES, w.shape[1]))
    wr = jnp.zeros((d, LANES), F32).at[:, :n_exp].set(w_router[l])
    wr_hi = wr.astype(BF16)
    br = jnp.full((1, LANES), NEG_BIG, F32).at[0, :n_exp].set(b_router[l])
    return dict(
        gmix=row(norm_mix_g[l]), win=w_in[l].astype(BF16), bg=b_gates[l],
        caw=taps(conv_a_w[l]), cab=row(conv_a_b[l]), waout=w_a_out[l].astype(BF16),
        cbw=taps(conv_b_w[l]), cbb=row(conv_b_b[l]), lng=row(ln_b_g[l]), lnb=row(ln_b_b[l]),
        wbout=w_b_out[l].astype(BF16), wo=w_o[l].astype(BF16), gffn=row(norm_ffn_g[l]),
        wrh=wr_hi, wrl=(wr - wr_hi.astype(F32)).astype(BF16), br=br,
        wgu=w_gu[l], bgu=b_gu[l], wd=w_down[l], bd=b_down[l])


def kernel(x_prompt, x_sample, state_conv_a, state_conv_b, norm_mix_g, w_in, b_gates, conv_a_w, conv_a_b, w_a_out, conv_b_w, conv_b_b, ln_b_g, ln_b_b, w_b_out, w_o, norm_ffn_g, w_router, b_router, w_gu, b_gu, w_down, b_down, norm_final_g):
    depth = w_in.shape[0]
    assert depth == 1, "the final norm is fused into the last layer's combine call"
    p = _prep_params(0, norm_mix_g, w_in, b_gates, conv_a_w, conv_a_b, w_a_out, conv_b_w, conv_b_b, ln_b_g,
                     ln_b_b, w_b_out, w_o, norm_ffn_g, w_router, b_router, w_gu, b_gu, w_down, b_down)
    y_p, y_s, na_p, nb_p, na_s, nb_s = _layer(x_prompt, x_sample, state_conv_a[0], state_conv_b[0], p,
                                               norm_final_g)
    return (y_p, y_s, na_p[None], nb_p[None], na_s[None], nb_s[None])
```

```python
import functools

import jax
import jax.numpy as jnp
from jax import lax
from jax.experimental import pallas as pl
from jax.experimental.pallas import tpu as pltpu

EPS = 1e-5
SWIGLU_ALPHA = 1.702
SWIGLU_LIMIT = 7.0
TOP_K = 4
MOE_BLOCK = 256
LANES = 128
SUBLANES = 8
NEG_BIG = -1e30
VMEM_LIMIT = 60 * 1024 * 1024
CONV_OUT_BLOCK = 4
ROW_DMA_UNROLL = 4

F32 = jnp.float32
BF16 = jnp.bfloat16


def _sigmoid(v):
    return 1.0 / (1.0 + jnp.exp(-v))


def _store_row_tiles(ref, value):
    n, d = value.shape
    pitch = d // LANES
    for c in range(pitch):
        ref[pl.ds(c, n, stride=pitch), :] = value[:, c * LANES:(c + 1) * LANES]


def _load_row_tiles(ref, n, d):
    pitch = d // LANES
    return jnp.concatenate([ref[pl.ds(c, n, stride=pitch), :] for c in range(pitch)], axis=1)


def _rms(v, g):
    return v * lax.rsqrt(jnp.mean(v * v, axis=-1, keepdims=True) + EPS) * g


def _causal_conv(ext_ref, w_ref, bias, n_out, width, n_seq):
    d = ext_ref.shape[-1]
    slabs = {}
    for t0 in range(0, n_out, CONV_OUT_BLOCK):
        nb = min(CONV_OUT_BLOCK, n_out - t0)
        for sg in range(n_seq // SUBLANES):
            rows = pl.ds(sg * SUBLANES, SUBLANES)
            loaded = {}
            acc = [None] * nb
            for k in range(width):
                wk = w_ref[k]
                for j in range(nb):
                    src = t0 + j + k
                    if src not in loaded:
                        loaded[src] = ext_ref[src, rows, :]
                    term = wk * loaded[src]
                    acc[j] = term if acc[j] is None else acc[j] + term
            for j, a in enumerate(acc):
                slabs[(t0 + j, sg)] = a
    ordered = [slabs[(t, sg)] for t in range(n_out) for sg in range(n_seq // SUBLANES)]
    return jnp.concatenate(ordered, axis=0).reshape(n_out * n_seq, d) + bias


_MIXER_CONSTS = ("gmix", "win", "bg", "caw", "cab", "waout", "cbw", "cbb", "lng", "lnb", "wbout", "wo",
                 "gffn", "wrh", "wrl", "br")


def _mixer_tile(x_ref, exta, extb, c, xmid_ref, xn2_ref, route_ref, gate_ref, cnt_ref, cnt_acc, *, carry):
    tt, n_seq, d = x_ref.shape
    rows = tt * n_seq
    w_a = exta.shape[0] - tt + 1
    w_b = extb.shape[0] - tt + 1

    x = x_ref[...].reshape(rows, d)
    xn = _rms(x, c["gmix"][...]).astype(BF16)

    def proj(g):
        return jnp.dot(xn, c["win"][:, g * d:(g + 1) * d], preferred_element_type=F32)

    exta[w_a - 1:w_a - 1 + tt] = (proj(1) * proj(2)).reshape(tt, n_seq, d)
    conv_a = _causal_conv(exta, c["caw"], c["cab"][...], tt, w_a, n_seq)
    y_a = jnp.dot((proj(0) * conv_a).astype(BF16), c["waout"][...], preferred_element_type=F32)

    extb[w_b - 1:w_b - 1 + tt] = (proj(3) * _sigmoid(proj(4))).reshape(tt, n_seq, d)
    conv_b = _causal_conv(extb, c["cbw"], c["cbb"][...], tt, w_b, n_seq)
    mu = jnp.mean(conv_b, axis=-1, keepdims=True)
    cen = conv_b - mu
    ln = cen * lax.rsqrt(jnp.mean(cen * cen, axis=-1, keepdims=True) + EPS) * c["lng"][...] + c["lnb"][...]
    y_b = jnp.dot((ln * _sigmoid(ln)).astype(BF16), c["wbout"][...], preferred_element_type=F32)

    bg = c["bg"]
    merged = _sigmoid(proj(5) + bg[0:1, :]) * y_a + _sigmoid(proj(6) + bg[1:2, :]) * y_b
    x_mid = x + jnp.dot(merged.astype(BF16), c["wo"][...], preferred_element_type=F32)
    xmid_ref[...] = x_mid

    if carry:
        exta[0:w_a - 1] = exta[tt:tt + w_a - 1]
        extb[0:w_b - 1] = extb[tt:tt + w_b - 1]

    xn2 = _rms(x_mid, c["gffn"][...])
    _store_row_tiles(xn2_ref, xn2)
    x_hi = xn2.astype(BF16)
    x_lo = (xn2 - x_hi.astype(F32)).astype(BF16)
    logits = (jnp.dot(x_hi, c["wrh"][...], preferred_element_type=F32)
              + jnp.dot(x_lo, c["wrh"][...], preferred_element_type=F32)
              + jnp.dot(x_hi, c["wrl"][...], preferred_element_type=F32)) + c["br"][...]
    lane = lax.broadcasted_iota(jnp.int32, (rows, LANES), 1)
    work = logits
    top_v, top_i = [], []
    for _ in range(TOP_K):
        m = jnp.max(work, axis=-1, keepdims=True)
        idx = jnp.min(jnp.where(work == m, lane, LANES), axis=-1, keepdims=True)
        top_v.append(m)
        top_i.append(idx)
        work = jnp.where(lane == idx, -jnp.inf, work)
    ex = [jnp.exp(v - top_v[0]) for v in top_v]
    den = ex[0] + ex[1] + ex[2] + ex[3]

    onehot = jnp.zeros((rows, LANES), F32)
    for idx in top_i:
        onehot = onehot + (lane == idx).astype(F32)
    r_io = lax.broadcasted_iota(jnp.int32, (rows, rows), 0)
    c_io = lax.broadcasted_iota(jnp.int32, (rows, rows), 1)
    tri = (c_io < r_io).astype(BF16)
    prefix = jnp.dot(tri, onehot.astype(BF16), preferred_element_type=F32) + cnt_acc[...]
    route = jnp.zeros((rows, LANES), F32)
    gate = jnp.zeros((rows, LANES), F32)
    for k in range(TOP_K):
        pos = jnp.sum(jnp.where(lane == top_i[k], prefix, 0.0), axis=-1, keepdims=True)
        route = jnp.where(lane == k, top_i[k].astype(F32), route)
        route = jnp.where(lane == TOP_K + k, pos, route)
        gate = jnp.where(lane == k, ex[k] / den, gate)
    route_ref[...] = jnp.transpose(route)[:2 * TOP_K].astype(jnp.int32)
    gate_ref[...] = gate
    cnt_acc[...] = cnt_acc[...] + jnp.sum(onehot, axis=0, keepdims=True)
    cnt_ref[...] = cnt_acc[...]


def _mixer_kernel(*refs, n_tiles_p):
    n_c = len(_MIXER_CONSTS)
    xp_ref, xs_ref, hsa_ref, hsb_ref = refs[:4]
    c = dict(zip(_MIXER_CONSTS, refs[4:4 + n_c]))
    (xmid_ref, xn2_ref, route_ref, gate_ref, cnt_ref,
     newa_p_ref, newb_p_ref, newa_s_ref, newb_s_ref) = refs[4 + n_c:13 + n_c]
    exta_p, extb_p, exta_s, extb_s, cnt_acc, sem = refs[13 + n_c:]
    i = pl.program_id(0)
    tt_p = xp_ref.shape[0]
    tt_s = xs_ref.shape[0]
    w_a = exta_p.shape[0] - tt_p + 1
    w_b = extb_p.shape[0] - tt_p + 1
    outs = (xmid_ref, xn2_ref, route_ref, gate_ref, cnt_ref, cnt_acc)

    @pl.when(i == 0)
    def _():
        exta_p[0:w_a - 1] = jnp.zeros((w_a - 1,) + exta_p.shape[1:], F32)
        extb_p[0:w_b - 1] = jnp.zeros((w_b - 1,) + extb_p.shape[1:], F32)
        cnt_acc[...] = jnp.zeros_like(cnt_acc)

    @pl.when(i < n_tiles_p)
    def _():
        _mixer_tile(xp_ref, exta_p, extb_p, c, *outs, carry=True)

        @pl.when(i == n_tiles_p - 1)
        def _():
            cp_a = pltpu.make_async_copy(exta_p.at[pl.ds(0, w_a - 1)], newa_p_ref, sem.at[0])
            cp_b = pltpu.make_async_copy(extb_p.at[pl.ds(0, w_b - 1)], newb_p_ref, sem.at[1])
            cp_a.start()
            cp_b.start()
            cp_a.wait()
            cp_b.wait()

    @pl.when(i >= n_tiles_p)
    def _():
        q = i - n_tiles_p
        in_a = pltpu.make_async_copy(hsa_ref.at[pl.ds(q * (w_a - 1), w_a - 1)], exta_s.at[pl.ds(0, w_a - 1)],
                                     sem.at[0])
        in_b = pltpu.make_async_copy(hsb_ref.at[pl.ds(q * (w_b - 1), w_b - 1)], extb_s.at[pl.ds(0, w_b - 1)],
                                     sem.at[1])
        in_a.start()
        in_b.start()
        in_a.wait()
        in_b.wait()
        _mixer_tile(xs_ref, exta_s, extb_s, c, *outs, carry=False)
        out_a = pltpu.make_async_copy(exta_s.at[pl.ds(tt_s, w_a - 1)], newa_s_ref.at[pl.ds(q * (w_a - 1), w_a - 1)],
                                      sem.at[0])
        out_b = pltpu.make_async_copy(extb_s.at[pl.ds(tt_s, w_b - 1)], newb_s_ref.at[pl.ds(q * (w_b - 1), w_b - 1)],
                                      sem.at[1])
        out_a.start()
        out_b.start()
        out_a.wait()
        out_b.wait()


def _mixer_call(xp_tm, xs_tm, hs_a, hs_b, params, *, tt_p, tt_s, w_a, w_b):
    t_p, n_p, d = xp_tm.shape
    sb = xs_tm.shape[1]
    rows = tt_p * n_p
    assert rows == tt_s * sb == MOE_BLOCK
    n_tiles_p = t_p // tt_p
    n_tiles_s = xs_tm.shape[0] // tt_s
    n_tiles = n_tiles_p + n_tiles_s
    n_tok = n_tiles * rows
    pitch = d // LANES
    consts = [params[n] for n in _MIXER_CONSTS]
    const_spec = lambda a: pl.BlockSpec(a.shape, lambda i, _nd=a.ndim: (0,) * _nd, pipeline_mode=pl.Buffered(1))
    any_spec = pl.BlockSpec(memory_space=pl.ANY)
    in_specs = [
        pl.BlockSpec((tt_p, n_p, d), lambda i: (jnp.minimum(i, n_tiles_p - 1), 0, 0)),
        pl.BlockSpec((tt_s, sb, d), lambda i: (jnp.maximum(i - n_tiles_p, 0), 0, 0)),
        any_spec, any_spec,
    ] + [const_spec(a) for a in consts]
    out_shape = (
        jax.ShapeDtypeStruct((n_tok, d), F32),
        jax.ShapeDtypeStruct((n_tok * pitch, LANES), F32),
        jax.ShapeDtypeStruct((2 * TOP_K, n_tok), jnp.int32),
        jax.ShapeDtypeStruct((n_tok, LANES), F32),
        jax.ShapeDtypeStruct((1, LANES), F32),
        jax.ShapeDtypeStruct((w_a - 1, n_p, d), F32),
        jax.ShapeDtypeStruct((w_b - 1, n_p, d), F32),
        jax.ShapeDtypeStruct(hs_a.shape, F32),
        jax.ShapeDtypeStruct(hs_b.shape, F32),
    )
    out_specs = (
        pl.BlockSpec((rows, d), lambda i: (i, 0)),
        pl.BlockSpec((rows * pitch, LANES), lambda i: (i, 0)),
        pl.BlockSpec((2 * TOP_K, rows), lambda i: (0, i)),
        pl.BlockSpec((rows, LANES), lambda i: (i, 0)),
        pl.BlockSpec((1, LANES), lambda i: (0, 0)),
        any_spec, any_spec, any_spec, any_spec,
    )
    return pl.pallas_call(
        functools.partial(_mixer_kernel, n_tiles_p=n_tiles_p),
        grid=(n_tiles,),
        in_specs=in_specs,
        out_specs=out_specs,
        out_shape=out_shape,
        scratch_shapes=[pltpu.VMEM((tt_p + w_a - 1, n_p, d), F32),
                        pltpu.VMEM((tt_p + w_b - 1, n_p, d), F32),
                        pltpu.VMEM((tt_s + w_a - 1, sb, d), F32),
                        pltpu.VMEM((tt_s + w_b - 1, sb, d), F32),
                        pltpu.VMEM((1, LANES), F32),
                        pltpu.SemaphoreType.DMA((2,))],
        compiler_params=pltpu.CompilerParams(dimension_semantics=("arbitrary",), vmem_limit_bytes=VMEM_LIMIT),
        name="mixer_router",
    )(xp_tm, xs_tm, hs_a, hs_b, *consts)


def _dispatch_kernel(zero_ref, x_ref, dest_ref, xb_ref, zero_buf, sem, zsem, *, pitch):
    tm = x_ref.shape[0] // pitch
    n_zero = zero_ref.shape[0]
    blk_rows = zero_buf.shape[0]

    @pl.when(pl.program_id(0) == 0)
    def _():
        zero_buf[...] = jnp.zeros_like(zero_buf)

        def zcopy(e):
            start = pl.multiple_of(jnp.maximum(zero_ref[e], 0) * pitch, blk_rows)
            return pltpu.make_async_copy(zero_buf, xb_ref.at[pl.ds(start, blk_rows)], zsem)

        def start(e, c):
            @pl.when(zero_ref[e] >= 0)
            def _():
                zcopy(e).start()
            return c

        def wait(e, c):
            @pl.when(zero_ref[e] >= 0)
            def _():
                zcopy(e).wait()
            return c

        lax.fori_loop(0, n_zero, start, 0)
        lax.fori_loop(0, n_zero, wait, 0)

    def row_copy(r, dst):
        src = x_ref.at[pl.ds(pl.multiple_of(r * pitch, pitch), pitch)]
        return pltpu.make_async_copy(src, xb_ref.at[pl.ds(pl.multiple_of(dst * pitch, pitch), pitch)], sem)

    def start_rows(r, c):
        for k in range(TOP_K):
            row_copy(r, dest_ref[0, k * tm + r]).start(priority=k % 2)
        return c

    lax.fori_loop(0, tm, start_rows, 0, unroll=ROW_DMA_UNROLL)
    n_all = tm * TOP_K * pitch
    pltpu.make_async_copy(xb_ref.at[pl.ds(0, n_all)], xb_ref.at[pl.ds(0, n_all)], sem).wait()


def _dispatch_call(xn2_tiles, dest_tiles, zero_start, n_rows, *, tm, d):
    pitch = d // LANES
    n_tiles = xn2_tiles.shape[0] // (tm * pitch)
    grid_spec = pltpu.PrefetchScalarGridSpec(
        num_scalar_prefetch=1,
        grid=(n_tiles,),
        in_specs=[pl.BlockSpec((tm * pitch, LANES), lambda i, z: (i, 0)),
                  pl.BlockSpec((None, 1, tm * TOP_K), lambda i, z: (i, 0, 0), memory_space=pltpu.SMEM)],
        out_specs=pl.BlockSpec(memory_space=pl.ANY),
        scratch_shapes=[pltpu.VMEM((MOE_BLOCK * pitch, LANES), F32), pltpu.SemaphoreType.DMA,
                        pltpu.SemaphoreType.DMA],
    )
    return pl.pallas_call(
        functools.partial(_dispatch_kernel, pitch=pitch),
        grid_spec=grid_spec,
        out_shape=jax.ShapeDtypeStruct((n_rows * pitch, LANES), F32),
        compiler_params=pltpu.CompilerParams(dimension_semantics=("arbitrary",)),
        name="moe_dispatch",
    )(zero_start, xn2_tiles, dest_tiles)


def _expert_kernel(be_ref, nu_ref, x_ref, wgu_ref, bgu_ref, wd_ref, bd_ref, y_ref, wgu_bf, wd_bf):
    b = pl.program_id(0)

    @pl.when(b < nu_ref[0])
    def _():
        d_ff, d = wd_ref.shape

        @pl.when((b == 0) | (be_ref[b] != be_ref[jnp.maximum(b - 1, 0)]))
        def _():
            wgu_bf[...] = wgu_ref[...].astype(BF16)
            wd_bf[...] = wd_ref[...].astype(BF16)

        x = _load_row_tiles(x_ref, MOE_BLOCK, d)
        h = jnp.dot(x.astype(BF16), wgu_bf[...], preferred_element_type=F32) + bgu_ref[...]
        g = jnp.minimum(h[:, :d_ff], SWIGLU_LIMIT)
        u = jnp.clip(h[:, d_ff:], -SWIGLU_LIMIT, SWIGLU_LIMIT)
        act = (u + 1.0) * (g * _sigmoid(SWIGLU_ALPHA * g))
        y = jnp.dot(act.astype(BF16), wd_bf[...], preferred_element_type=F32) + bd_ref[...]
        _store_row_tiles(y_ref, y)

    @pl.when(b >= nu_ref[0])
    def _():
        y_ref[...] = jnp.zeros_like(y_ref)


def _expert_call(xb_tiles, block_e, n_used, wgu, bgu, wd, bd):
    n_exp, d, two_ff = wgu.shape
    d_ff = wd.shape[1]
    blk_rows = MOE_BLOCK * d // LANES
    n_blocks = xb_tiles.shape[0] // blk_rows
    per_e = lambda b, be, nu: (be[b], 0, 0)
    grid_spec = pltpu.PrefetchScalarGridSpec(
        num_scalar_prefetch=2,
        grid=(n_blocks,),
        in_specs=[pl.BlockSpec((blk_rows, LANES), lambda b, be, nu: (jnp.minimum(b, nu[0] - 1), 0)),
                  pl.BlockSpec((None, d, two_ff), per_e),
                  pl.BlockSpec((None, 1, two_ff), per_e),
                  pl.BlockSpec((None, d_ff, d), per_e),
                  pl.BlockSpec((None, 1, d), per_e)],
        out_specs=pl.BlockSpec((blk_rows, LANES), lambda b, be, nu: (b, 0)),
        scratch_shapes=[pltpu.VMEM((d, two_ff), BF16), pltpu.VMEM((d_ff, d), BF16)],
    )
    return pl.pallas_call(
        _expert_kernel,
        grid_spec=grid_spec,
        out_shape=jax.ShapeDtypeStruct(xb_tiles.shape, F32),
        compiler_params=pltpu.CompilerParams(dimension_semantics=("arbitrary",), vmem_limit_bytes=VMEM_LIMIT),
        name="moe_experts",
    )(block_e, n_used, xb_tiles, wgu, bgu.reshape(n_exp, 1, two_ff), wd, bd.reshape(n_exp, 1, d))


def _combine_kernel(xmid_ref, gate_ref, dest_ref, gfin_ref, yb_ref, outp_ref, outs_ref, ybuf, sem, *, n_tiles_p):
    tm, d = xmid_ref.shape
    pitch = d // LANES

    def row_copy(r, k, src):
        return pltpu.make_async_copy(yb_ref.at[pl.ds(pl.multiple_of(src * pitch, pitch), pitch)],
                                     ybuf.at[k, pl.ds(pl.multiple_of(r * pitch, pitch), pitch)], sem)

    def start_rows(r, c):
        for k in range(TOP_K):
            row_copy(r, k, dest_ref[0, k * tm + r]).start(priority=k % 2)
        return c

    lax.fori_loop(0, tm, start_rows, 0, unroll=ROW_DMA_UNROLL)
    pltpu.make_async_copy(ybuf, ybuf, sem).wait()
    gate = gate_ref[...]
    y = xmid_ref[...]
    for k in range(TOP_K):
        y = y + gate[:, k:k + 1] * _load_row_tiles(ybuf.at[k], tm, d)
    out = _rms(y, gfin_ref[...])

    @pl.when(pl.program_id(0) < n_tiles_p)
    def _():
        outp_ref[...] = out

    @pl.when(pl.program_id(0) >= n_tiles_p)
    def _():
        outs_ref[...] = out


def _combine_call(x_mid, gate, dest_tiles, g_final, yb, *, tm, n_tiles_p):
    n_tok, d = x_mid.shape
    n_tiles = n_tok // tm
    return pl.pallas_call(
        functools.partial(_combine_kernel, n_tiles_p=n_tiles_p),
        grid=(n_tiles,),
        in_specs=[pl.BlockSpec((tm, d), lambda i: (i, 0)),
                  pl.BlockSpec((tm, LANES), lambda i: (i, 0)),
                  pl.BlockSpec((None, 1, tm * TOP_K), lambda i: (i, 0, 0), memory_space=pltpu.SMEM),
                  pl.BlockSpec((1, d), lambda i: (0, 0)),
                  pl.BlockSpec(memory_space=pl.ANY)],
        out_specs=(pl.BlockSpec((tm, d), lambda i: (jnp.minimum(i, n_tiles_p - 1), 0)),
                   pl.BlockSpec((tm, d), lambda i: (jnp.maximum(i - n_tiles_p, 0), 0))),
        out_shape=(jax.ShapeDtypeStruct((n_tiles_p * tm, d), F32),
                   jax.ShapeDtypeStruct(((n_tiles - n_tiles_p) * tm, d), F32)),
        scratch_shapes=[pltpu.VMEM((TOP_K, tm * d // LANES, LANES), F32), pltpu.SemaphoreType.DMA],
        compiler_params=pltpu.CompilerParams(dimension_semantics=("arbitrary",)),
        name="moe_combine",
    )(x_mid, gate, dest_tiles, g_final, yb)


def _to_time_major(x, seq_block):
    n_seqs, t, d = x.shape
    n_sb = n_seqs // seq_block
    return x.reshape(n_sb, seq_block, t, d).transpose(0, 2, 1, 3).reshape(n_sb * t, seq_block, d)


def _from_time_major(x, n_seqs, seq_block):
    d = x.shape[-1]
    n_sb = n_seqs // seq_block
    t = x.size // (n_seqs * d)
    return x.reshape(n_sb, t, seq_block, d).transpose(0, 2, 1, 3).reshape(n_seqs, t, d)


def _layer(xp, xs, state_a, state_b, p, norm_final_g):
    n_p, t_p, d = xp.shape
    n_s, t_s, _ = xs.shape
    w_a = p["caw"].shape[0]
    w_b = p["cbw"].shape[0]
    n_exp = p["wgu"].shape[0]
    tm = MOE_BLOCK
    sb = tm // t_s
    n_tiles_p = n_p * t_p // tm

    (x_mid, xn2, route, gate, cnt, newa_p, newb_p, newa_s, newb_s) = _mixer_call(
        _to_time_major(xp, n_p), _to_time_major(xs, sb), _to_time_major(state_a, sb), _to_time_major(state_b, sb),
        p, tt_p=tm // n_p, tt_s=t_s, w_a=w_a, w_b=w_b)
    n_tok = x_mid.shape[0]
    n_tiles = n_tok // tm

    counts = cnt[0, :n_exp].astype(jnp.int32)
    padded = (counts + MOE_BLOCK - 1) // MOE_BLOCK * MOE_BLOCK
    pad_end = jnp.cumsum(padded)
    pad_start = pad_end - padded
    is_e = route[:TOP_K, :, None] == jnp.arange(n_exp, dtype=jnp.int32)
    dest = jnp.sum(jnp.where(is_e, pad_start, 0), axis=-1) + route[TOP_K:]
    dest_tiles = dest.reshape(TOP_K, n_tiles, tm).transpose(1, 0, 2).reshape(n_tiles, 1, TOP_K * tm)
    n_blocks = -(-(n_tok * TOP_K) // MOE_BLOCK) + n_exp
    n_used = (pad_end[-1] // MOE_BLOCK).astype(jnp.int32)
    blk_start = jnp.minimum(jnp.arange(n_blocks, dtype=jnp.int32) * MOE_BLOCK, pad_end[-1] - 1)
    block_e = jnp.minimum(jnp.sum(blk_start[:, None] >= pad_end[None, :], axis=1), n_exp - 1).astype(jnp.int32)
    last_blocks = jnp.arange(n_blocks - n_exp, n_blocks, dtype=jnp.int32)
    zero_start = jnp.concatenate([jnp.where(padded > 0, pad_end - MOE_BLOCK, -1),
                                  jnp.where(last_blocks >= n_used, last_blocks * MOE_BLOCK, -1)]).astype(jnp.int32)

    xb = _dispatch_call(xn2, dest_tiles, zero_start, n_blocks * MOE_BLOCK, tm=tm, d=d)
    yb = _expert_call(xb, block_e, n_used.reshape(1), p["wgu"], p["bgu"], p["wd"], p["bd"])
    y_p, y_s = _combine_call(x_mid, gate, dest_tiles, norm_final_g.reshape(1, d), yb, tm=tm, n_tiles_p=n_tiles_p)

    return (_from_time_major(y_p, n_p, n_p), _from_time_major(y_s, n_s, sb),
            _from_time_major(newa_p, n_p, n_p), _from_time_major(newb_p, n_p, n_p),
            _from_time_major(newa_s, n_s, sb), _from_time_major(newb_s, n_s, sb))


def _prep_params(l, norm_mix_g, w_in, b_gates, conv_a_w, conv_a_b, w_a_out, conv_b_w, conv_b_b, ln_b_g,
                 ln_b_b, w_b_out, w_o, norm_ffn_g, w_router, b_router, w_gu, b_gu, w_down, b_down):
    d = w_in.shape[1]
    n_exp = w_router.shape[-1]
    row = lambda v: v.reshape(1, -1)
    taps = lambda w: jnp.broadcast_to(w[:, None, :], (w.shape[0], SUBLANES, w.shape[1]))
    wr = jnp.zeros((d, LANES), F32).at[:, :n_exp].set(w_router[l])
    wr_hi = wr.astype(BF16)
    br = jnp.full((1, LANES), NEG_BIG, F32).at[0, :n_exp].set(b_router[l])
    return dict(
        gmix=row(norm_mix_g[l]), win=w_in[l].astype(BF16), bg=b_gates[l],
        caw=taps(conv_a_w[l]), cab=row(conv_a_b[l]), waout=w_a_out[l].astype(BF16),
        cbw=taps(conv_b_w[l]), cbb=row(conv_b_b[l]), lng=row(ln_b_g[l]), lnb=row(ln_b_b[l]),
        wbout=w_b_out[l].astype(BF16), wo=w_o[l].astype(BF16), gffn=row(norm_ffn_g[l]),
        wrh=wr_hi, wrl=(wr - wr_hi.astype(F32)).astype(BF16), br=br,
        wgu=w_gu[l], bgu=b_gu[l], wd=w_down[l], bd=b_down[l])


def kernel(x_prompt, x_sample, state_conv_a, state_conv_b, norm_mix_g, w_in, b_gates, conv_a_w, conv_a_b, w_a_out, conv_b_w, conv_b_b, ln_b_g, ln_b_b, w_b_out, w_o, norm_ffn_g, w_router, b_router, w_gu, b_gu, w_down, b_down, norm_final_g):
    depth = w_in.shape[0]
    assert depth == 1, "the final norm is fused into the last layer's combine call"
    p = _prep_params(0, norm_mix_g, w_in, b_gates, conv_a_w, conv_a_b, w_a_out, conv_b_w, conv_b_b, ln_b_g,
                     ln_b_b, w_b_out, w_o, norm_ffn_g, w_router, b_router, w_gu, b_gu, w_down, b_down)
    y_p, y_s, na_p, nb_p, na_s, nb_s = _layer(x_prompt, x_sample, state_conv_a[0], state_conv_b[0], p,
                                               norm_final_g)
    return (y_p, y_s, na_p[None], nb_p[None], na_s[None], nb_s[None])
```

```python
import functools

import jax
import jax.numpy as jnp
from jax import lax
from jax.experimental import pallas as pl
from jax.experimental.pallas import tpu as pltpu

EPS = 1e-5
SWIGLU_ALPHA = 1.702
SWIGLU_LIMIT = 7.0
TOP_K = 4
MOE_BLOCK = 256
LANES = 128
SUBLANES = 8
NEG_BIG = -1e30
VMEM_LIMIT = 60 * 1024 * 1024
CONV_OUT_BLOCK = 4
ROW_DMA_UNROLL = 4
MIXER_SUBTILES = 2

F32 = jnp.float32
BF16 = jnp.bfloat16


def _sigmoid(v):
    return 1.0 / (1.0 + jnp.exp(-v))


def _store_row_tiles(ref, value):
    n, d = value.shape
    pitch = d // LANES
    for c in range(pitch):
        ref[pl.ds(c, n, stride=pitch), :] = value[:, c * LANES:(c + 1) * LANES]


def _load_row_tiles(ref, n, d):
    pitch = d // LANES
    return jnp.concatenate([ref[pl.ds(c, n, stride=pitch), :] for c in range(pitch)], axis=1)


def _rms(v, g):
    return v * lax.rsqrt(jnp.mean(v * v, axis=-1, keepdims=True) + EPS) * g


def _causal_conv(ext_ref, w_ref, bias, t_off, n_out, width, n_seq):
    d = ext_ref.shape[-1]
    slabs = {}
    for t0 in range(0, n_out, CONV_OUT_BLOCK):
        nb = min(CONV_OUT_BLOCK, n_out - t0)
        for sg in range(n_seq // SUBLANES):
            rows = pl.ds(sg * SUBLANES, SUBLANES)
            loaded = {}
            acc = [None] * nb
            for k in range(width):
                wk = w_ref[k]
                for j in range(nb):
                    src = t_off + t0 + j + k
                    if src not in loaded:
                        loaded[src] = ext_ref[src, rows, :]
                    term = wk * loaded[src]
                    acc[j] = term if acc[j] is None else acc[j] + term
            for j, a in enumerate(acc):
                slabs[(t0 + j, sg)] = a
    ordered = [slabs[(t, sg)] for t in range(n_out) for sg in range(n_seq // SUBLANES)]
    return jnp.concatenate(ordered, axis=0).reshape(n_out * n_seq, d) + bias


_MIXER_CONSTS = ("gmix", "win", "bg", "caw", "cab", "waout", "cbw", "cbb", "lng", "lnb", "wbout", "wo",
                 "gffn", "wrh", "wrl", "br")


def _mixer_subtiles(x_ref, exta, extb, c, xmid_ref, xn2_ref, route_ref, gate_ref, cnt_ref, cnt_acc, subs):
    n_seq, d = x_ref.shape[1:]
    rows = MOE_BLOCK
    tt = rows // n_seq
    w_a = c["caw"].shape[0]
    w_b = c["cbw"].shape[0]
    pitch = d // LANES
    st = [dict(x_t0=a, e_t0=b, r0=r) for a, b, r in subs]

    for s in st:
        s["x"] = x_ref[s["x_t0"]:s["x_t0"] + tt].reshape(rows, d)
        s["xn"] = _rms(s["x"], c["gmix"][...]).astype(BF16)

    def proj(s, g):
        return jnp.dot(s["xn"], c["win"][:, g * d:(g + 1) * d], preferred_element_type=F32)

    for s in st:
        lo = w_a - 1 + s["e_t0"]
        exta[lo:lo + tt] = (proj(s, 1) * proj(s, 2)).reshape(tt, n_seq, d)
    for s in st:
        lo = w_b - 1 + s["e_t0"]
        extb[lo:lo + tt] = (proj(s, 3) * _sigmoid(proj(s, 4))).reshape(tt, n_seq, d)
    for s in st:
        conv_a = _causal_conv(exta, c["caw"], c["cab"][...], s["e_t0"], tt, w_a, n_seq)
        s["y_a"] = jnp.dot((proj(s, 0) * conv_a).astype(BF16), c["waout"][...], preferred_element_type=F32)
    for s in st:
        conv_b = _causal_conv(extb, c["cbw"], c["cbb"][...], s["e_t0"], tt, w_b, n_seq)
        mu = jnp.mean(conv_b, axis=-1, keepdims=True)
        cen = conv_b - mu
        ln = cen * lax.rsqrt(jnp.mean(cen * cen, axis=-1, keepdims=True) + EPS) * c["lng"][...] + c["lnb"][...]
        s["y_b"] = jnp.dot((ln * _sigmoid(ln)).astype(BF16), c["wbout"][...], preferred_element_type=F32)
    bg = c["bg"]
    for s in st:
        merged = _sigmoid(proj(s, 5) + bg[0:1, :]) * s["y_a"] + _sigmoid(proj(s, 6) + bg[1:2, :]) * s["y_b"]
        x_mid = s["x"] + jnp.dot(merged.astype(BF16), c["wo"][...], preferred_element_type=F32)
        xmid_ref[s["r0"]:s["r0"] + rows, :] = x_mid
        s["xn2"] = _rms(x_mid, c["gffn"][...])
        _store_row_tiles(xn2_ref.at[pl.ds(s["r0"] * pitch, rows * pitch)], s["xn2"])

    lane = lax.broadcasted_iota(jnp.int32, (rows, LANES), 1)
    r_io = lax.broadcasted_iota(jnp.int32, (rows, rows), 0)
    c_io = lax.broadcasted_iota(jnp.int32, (rows, rows), 1)
    tri = (c_io < r_io).astype(BF16)
    cnt = cnt_acc[...]
    for s in st:
        x_hi = s["xn2"].astype(BF16)
        x_lo = (s["xn2"] - x_hi.astype(F32)).astype(BF16)
        logits = (jnp.dot(x_hi, c["wrh"][...], preferred_element_type=F32)
                  + jnp.dot(x_lo, c["wrh"][...], preferred_element_type=F32)
                  + jnp.dot(x_hi, c["wrl"][...], preferred_element_type=F32)) + c["br"][...]
        work = logits
        top_v, top_i = [], []
        for _ in range(TOP_K):
            m = jnp.max(work, axis=-1, keepdims=True)
            idx = jnp.min(jnp.where(work == m, lane, LANES), axis=-1, keepdims=True)
            top_v.append(m)
            top_i.append(idx)
            work = jnp.where(lane == idx, -jnp.inf, work)
        ex = [jnp.exp(v - top_v[0]) for v in top_v]
        den = ex[0] + ex[1] + ex[2] + ex[3]
        onehot = jnp.zeros((rows, LANES), F32)
        for idx in top_i:
            onehot = onehot + (lane == idx).astype(F32)
        prefix = jnp.dot(tri, onehot.astype(BF16), preferred_element_type=F32) + cnt
        route = jnp.zeros((rows, LANES), F32)
        gate = jnp.zeros((rows, LANES), F32)
        for k in range(TOP_K):
            pos = jnp.sum(jnp.where(lane == top_i[k], prefix, 0.0), axis=-1, keepdims=True)
            route = jnp.where(lane == k, top_i[k].astype(F32), route)
            route = jnp.where(lane == TOP_K + k, pos, route)
            gate = jnp.where(lane == k, ex[k] / den, gate)
        route_ref[:, s["r0"]:s["r0"] + rows] = jnp.transpose(route)[:2 * TOP_K].astype(jnp.int32)
        gate_ref[s["r0"]:s["r0"] + rows, :] = gate
        cnt = cnt + jnp.sum(onehot, axis=0, keepdims=True)
    cnt_acc[...] = cnt
    cnt_ref[...] = cnt


def _mixer_kernel(*refs, n_steps_p, n_sub):
    n_c = len(_MIXER_CONSTS)
    xp_ref, xs_ref, hsa_ref, hsb_ref = refs[:4]
    c = dict(zip(_MIXER_CONSTS, refs[4:4 + n_c]))
    (xmid_ref, xn2_ref, route_ref, gate_ref, cnt_ref,
     newa_p_ref, newb_p_ref, newa_s_ref, newb_s_ref) = refs[4 + n_c:13 + n_c]
    exta_p, extb_p, exta_s, extb_s, cnt_acc, sem = refs[13 + n_c:]
    i = pl.program_id(0)
    tt_p = xp_ref.shape[0] // n_sub
    tt_s = xs_ref.shape[0] // n_sub
    w_a = c["caw"].shape[0]
    w_b = c["cbw"].shape[0]
    outs = (xmid_ref, xn2_ref, route_ref, gate_ref, cnt_ref, cnt_acc)

    @pl.when(i == 0)
    def _():
        exta_p[0:w_a - 1] = jnp.zeros((w_a - 1,) + exta_p.shape[1:], F32)
        extb_p[0:w_b - 1] = jnp.zeros((w_b - 1,) + extb_p.shape[1:], F32)
        cnt_acc[...] = jnp.zeros_like(cnt_acc)

    @pl.when(i < n_steps_p)
    def _():
        _mixer_subtiles(xp_ref, exta_p, extb_p, c, *outs,
                        subs=[(h * tt_p, h * tt_p, h * MOE_BLOCK) for h in range(n_sub)])
        exta_p[0:w_a - 1] = exta_p[n_sub * tt_p:n_sub * tt_p + w_a - 1]
        extb_p[0:w_b - 1] = extb_p[n_sub * tt_p:n_sub * tt_p + w_b - 1]

        @pl.when(i == n_steps_p - 1)
        def _():
            cp_a = pltpu.make_async_copy(exta_p.at[pl.ds(0, w_a - 1)], newa_p_ref, sem.at[0])
            cp_b = pltpu.make_async_copy(extb_p.at[pl.ds(0, w_b - 1)], newb_p_ref, sem.at[1])
            cp_a.start()
            cp_b.start()
            cp_a.wait()
            cp_b.wait()

    @pl.when(i >= n_steps_p)
    def _():
        for h in range(n_sub):
            q = (i - n_steps_p) * n_sub + h
            in_a = pltpu.make_async_copy(hsa_ref.at[pl.ds(q * (w_a - 1), w_a - 1)],
                                         exta_s.at[pl.ds(0, w_a - 1)], sem.at[0])
            in_b = pltpu.make_async_copy(hsb_ref.at[pl.ds(q * (w_b - 1), w_b - 1)],
                                         extb_s.at[pl.ds(0, w_b - 1)], sem.at[1])
            in_a.start()
            in_b.start()
            in_a.wait()
            in_b.wait()
            _mixer_subtiles(xs_ref, exta_s, extb_s, c, *outs, subs=[(h * tt_s, 0, h * MOE_BLOCK)])
            out_a = pltpu.make_async_copy(exta_s.at[pl.ds(tt_s, w_a - 1)],
                                          newa_s_ref.at[pl.ds(q * (w_a - 1), w_a - 1)], sem.at[0])
            out_b = pltpu.make_async_copy(extb_s.at[pl.ds(tt_s, w_b - 1)],
                                          newb_s_ref.at[pl.ds(q * (w_b - 1), w_b - 1)], sem.at[1])
            out_a.start()
            out_b.start()
            out_a.wait()
            out_b.wait()


def _mixer_call(xp_tm, xs_tm, hs_a, hs_b, params, *, tt_p, tt_s, w_a, w_b):
    t_p, n_p, d = xp_tm.shape
    sb = xs_tm.shape[1]
    n_sub = MIXER_SUBTILES
    assert tt_p * n_p == tt_s * sb == MOE_BLOCK
    rows = n_sub * MOE_BLOCK
    n_steps_p = t_p // (tt_p * n_sub)
    n_steps_s = xs_tm.shape[0] // (tt_s * n_sub)
    assert n_steps_p * tt_p * n_sub == t_p and n_steps_s * tt_s * n_sub == xs_tm.shape[0]
    n_steps = n_steps_p + n_steps_s
    n_tok = n_steps * rows
    pitch = d // LANES
    consts = [params[n] for n in _MIXER_CONSTS]
    const_spec = lambda a: pl.BlockSpec(a.shape, lambda i, _nd=a.ndim: (0,) * _nd, pipeline_mode=pl.Buffered(1))
    any_spec = pl.BlockSpec(memory_space=pl.ANY)
    in_specs = [
        pl.BlockSpec((n_sub * tt_p, n_p, d), lambda i: (jnp.minimum(i, n_steps_p - 1), 0, 0)),
        pl.BlockSpec((n_sub * tt_s, sb, d), lambda i: (jnp.maximum(i - n_steps_p, 0), 0, 0)),
        any_spec, any_spec,
    ] + [const_spec(a) for a in consts]
    out_shape = (
        jax.ShapeDtypeStruct((n_tok, d), F32),
        jax.ShapeDtypeStruct((n_tok * pitch, LANES), F32),
        jax.ShapeDtypeStruct((2 * TOP_K, n_tok), jnp.int32),
        jax.ShapeDtypeStruct((n_tok, LANES), F32),
        jax.ShapeDtypeStruct((1, LANES), F32),
        jax.ShapeDtypeStruct((w_a - 1, n_p, d), F32),
        jax.ShapeDtypeStruct((w_b - 1, n_p, d), F32),
        jax.ShapeDtypeStruct(hs_a.shape, F32),
        jax.ShapeDtypeStruct(hs_b.shape, F32),
    )
    out_specs = (
        pl.BlockSpec((rows, d), lambda i: (i, 0)),
        pl.BlockSpec((rows * pitch, LANES), lambda i: (i, 0)),
        pl.BlockSpec((2 * TOP_K, rows), lambda i: (0, i)),
        pl.BlockSpec((rows, LANES), lambda i: (i, 0)),
        pl.BlockSpec((1, LANES), lambda i: (0, 0)),
        any_spec, any_spec, any_spec, any_spec,
    )
    return pl.pallas_call(
        functools.partial(_mixer_kernel, n_steps_p=n_steps_p, n_sub=n_sub),
        grid=(n_steps,),
        in_specs=in_specs,
        out_specs=out_specs,
        out_shape=out_shape,
        scratch_shapes=[pltpu.VMEM((n_sub * tt_p + w_a - 1, n_p, d), F32),
                        pltpu.VMEM((n_sub * tt_p + w_b - 1, n_p, d), F32),
                        pltpu.VMEM((tt_s + w_a - 1, sb, d), F32),
                        pltpu.VMEM((tt_s + w_b - 1, sb, d), F32),
                        pltpu.VMEM((1, LANES), F32),
                        pltpu.SemaphoreType.DMA((2,))],
        compiler_params=pltpu.CompilerParams(dimension_semantics=("arbitrary",), vmem_limit_bytes=VMEM_LIMIT),
        name="mixer_router",
    )(xp_tm, xs_tm, hs_a, hs_b, *consts)


def _dispatch_kernel(zero_ref, x_ref, dest_ref, xb_ref, zero_buf, sem, zsem, *, pitch):
    tm = x_ref.shape[0] // pitch
    n_zero = zero_ref.shape[0]
    blk_rows = zero_buf.shape[0]

    @pl.when(pl.program_id(0) == 0)
    def _():
        zero_buf[...] = jnp.zeros_like(zero_buf)

        def zcopy(e):
            start = pl.multiple_of(jnp.maximum(zero_ref[e], 0) * pitch, blk_rows)
            return pltpu.make_async_copy(zero_buf, xb_ref.at[pl.ds(start, blk_rows)], zsem)

        def start(e, c):
            @pl.when(zero_ref[e] >= 0)
            def _():
                zcopy(e).start()
            return c

        def wait(e, c):
            @pl.when(zero_ref[e] >= 0)
            def _():
                zcopy(e).wait()
            return c

        lax.fori_loop(0, n_zero, start, 0)
        lax.fori_loop(0, n_zero, wait, 0)

    def row_copy(r, dst):
        src = x_ref.at[pl.ds(pl.multiple_of(r * pitch, pitch), pitch)]
        return pltpu.make_async_copy(src, xb_ref.at[pl.ds(pl.multiple_of(dst * pitch, pitch), pitch)], sem)

    def start_rows(r, c):
        for k in range(TOP_K):
            row_copy(r, dest_ref[0, k * tm + r]).start(priority=k % 2)
        return c

    lax.fori_loop(0, tm, start_rows, 0, unroll=ROW_DMA_UNROLL)
    n_all = tm * TOP_K * pitch
    pltpu.make_async_copy(xb_ref.at[pl.ds(0, n_all)], xb_ref.at[pl.ds(0, n_all)], sem).wait()


def _dispatch_call(xn2_tiles, dest_tiles, zero_start, n_rows, *, tm, d):
    pitch = d // LANES
    n_tiles = xn2_tiles.shape[0] // (tm * pitch)
    grid_spec = pltpu.PrefetchScalarGridSpec(
        num_scalar_prefetch=1,
        grid=(n_tiles,),
        in_specs=[pl.BlockSpec((tm * pitch, LANES), lambda i, z: (i, 0)),
                  pl.BlockSpec((None, 1, tm * TOP_K), lambda i, z: (i, 0, 0), memory_space=pltpu.SMEM)],
        out_specs=pl.BlockSpec(memory_space=pl.ANY),
        scratch_shapes=[pltpu.VMEM((MOE_BLOCK * pitch, LANES), F32), pltpu.SemaphoreType.DMA,
                        pltpu.SemaphoreType.DMA],
    )
    return pl.pallas_call(
        functools.partial(_dispatch_kernel, pitch=pitch),
        grid_spec=grid_spec,
        out_shape=jax.ShapeDtypeStruct((n_rows * pitch, LANES), F32),
        compiler_params=pltpu.CompilerParams(dimension_semantics=("arbitrary",)),
        name="moe_dispatch",
    )(zero_start, xn2_tiles, dest_tiles)


def _expert_kernel(be_ref, nu_ref, x_ref, wgu_ref, bgu_ref, wd_ref, bd_ref, y_ref, wgu_bf, wd_bf):
    b = pl.program_id(0)

    @pl.when(b < nu_ref[0])
    def _():
        d_ff, d = wd_ref.shape

        @pl.when((b == 0) | (be_ref[b] != be_ref[jnp.maximum(b - 1, 0)]))
        def _():
            wgu_bf[...] = wgu_ref[...].astype(BF16)
            wd_bf[...] = wd_ref[...].astype(BF16)

        x = _load_row_tiles(x_ref, MOE_BLOCK, d)
        h = jnp.dot(x.astype(BF16), wgu_bf[...], preferred_element_type=F32) + bgu_ref[...]
        g = jnp.minimum(h[:, :d_ff], SWIGLU_LIMIT)
        u = jnp.clip(h[:, d_ff:], -SWIGLU_LIMIT, SWIGLU_LIMIT)
        act = (u + 1.0) * (g * _sigmoid(SWIGLU_ALPHA * g))
        y = jnp.dot(act.astype(BF16), wd_bf[...], preferred_element_type=F32) + bd_ref[...]
        _store_row_tiles(y_ref, y)

    @pl.when(b >= nu_ref[0])
    def _():
        y_ref[...] = jnp.zeros_like(y_ref)


def _expert_call(xb_tiles, block_e, n_used, wgu, bgu, wd, bd):
    n_exp, d, two_ff = wgu.shape
    d_ff = wd.shape[1]
    blk_rows = MOE_BLOCK * d // LANES
    n_blocks = xb_tiles.shape[0] // blk_rows
    per_e = lambda b, be, nu: (be[b], 0, 0)
    grid_spec = pltpu.PrefetchScalarGridSpec(
        num_scalar_prefetch=2,
        grid=(n_blocks,),
        in_specs=[pl.BlockSpec((blk_rows, LANES), lambda b, be, nu: (jnp.minimum(b, nu[0] - 1), 0)),
                  pl.BlockSpec((None, d, two_ff), per_e),
                  pl.BlockSpec((None, 1, two_ff), per_e),
                  pl.BlockSpec((None, d_ff, d), per_e),
                  pl.BlockSpec((None, 1, d), per_e)],
        out_specs=pl.BlockSpec((blk_rows, LANES), lambda b, be, nu: (b, 0)),
        scratch_shapes=[pltpu.VMEM((d, two_ff), BF16), pltpu.VMEM((d_ff, d), BF16)],
    )
    return pl.pallas_call(
        _expert_kernel,
        grid_spec=grid_spec,
        out_shape=jax.ShapeDtypeStruct(xb_tiles.shape, F32),
        compiler_params=pltpu.CompilerParams(dimension_semantics=("arbitrary",), vmem_limit_bytes=VMEM_LIMIT),
        name="moe_experts",
    )(block_e, n_used, xb_tiles, wgu, bgu.reshape(n_exp, 1, two_ff), wd, bd.reshape(n_exp, 1, d))


def _combine_kernel(xmid_ref, gate_ref, dest_ref, gfin_ref, yb_ref, outp_ref, outs_ref, ybuf, sem, *, n_tiles_p):
    tm, d = xmid_ref.shape
    pitch = d // LANES

    def row_copy(r, k, src):
        return pltpu.make_async_copy(yb_ref.at[pl.ds(pl.multiple_of(src * pitch, pitch), pitch)],
                                     ybuf.at[k, pl.ds(pl.multiple_of(r * pitch, pitch), pitch)], sem)

    def start_rows(r, c):
        for k in range(TOP_K):
            row_copy(r, k, dest_ref[0, k * tm + r]).start(priority=k % 2)
        return c

    lax.fori_loop(0, tm, start_rows, 0, unroll=ROW_DMA_UNROLL)
    pltpu.make_async_copy(ybuf, ybuf, sem).wait()
    gate = gate_ref[...]
    y = xmid_ref[...]
    for k in range(TOP_K):
        y = y + gate[:, k:k + 1] * _load_row_tiles(ybuf.at[k], tm, d)
    out = _rms(y, gfin_ref[...])

    @pl.when(pl.program_id(0) < n_tiles_p)
    def _():
        outp_ref[...] = out

    @pl.when(pl.program_id(0) >= n_tiles_p)
    def _():
        outs_ref[...] = out


def _combine_call(x_mid, gate, dest_tiles, g_final, yb, *, tm, n_tiles_p):
    n_tok, d = x_mid.shape
    n_tiles = n_tok // tm
    return pl.pallas_call(
        functools.partial(_combine_kernel, n_tiles_p=n_tiles_p),
        grid=(n_tiles,),
        in_specs=[pl.BlockSpec((tm, d), lambda i: (i, 0)),
                  pl.BlockSpec((tm, LANES), lambda i: (i, 0)),
                  pl.BlockSpec((None, 1, tm * TOP_K), lambda i: (i, 0, 0), memory_space=pltpu.SMEM),
                  pl.BlockSpec((1, d), lambda i: (0, 0)),
                  pl.BlockSpec(memory_space=pl.ANY)],
        out_specs=(pl.BlockSpec((tm, d), lambda i: (jnp.minimum(i, n_tiles_p - 1), 0)),
                   pl.BlockSpec((tm, d), lambda i: (jnp.maximum(i - n_tiles_p, 0), 0))),
        out_shape=(jax.ShapeDtypeStruct((n_tiles_p * tm, d), F32),
                   jax.ShapeDtypeStruct(((n_tiles - n_tiles_p) * tm, d), F32)),
        scratch_shapes=[pltpu.VMEM((TOP_K, tm * d // LANES, LANES), F32), pltpu.SemaphoreType.DMA],
        compiler_params=pltpu.CompilerParams(dimension_semantics=("arbitrary",)),
        name="moe_combine",
    )(x_mid, gate, dest_tiles, g_final, yb)


def _to_time_major(x, seq_block):
    n_seqs, t, d = x.shape
    n_sb = n_seqs // seq_block
    return x.reshape(n_sb, seq_block, t, d).transpose(0, 2, 1, 3).reshape(n_sb * t, seq_block, d)


def _from_time_major(x, n_seqs, seq_block):
    d = x.shape[-1]
    n_sb = n_seqs // seq_block
    t = x.size // (n_seqs * d)
    return x.reshape(n_sb, t, seq_block, d).transpose(0, 2, 1, 3).reshape(n_seqs, t, d)


def _layer(xp, xs, state_a, state_b, p, norm_final_g):
    n_p, t_p, d = xp.shape
    n_s, t_s, _ = xs.shape
    w_a = p["caw"].shape[0]
    w_b = p["cbw"].shape[0]
    n_exp = p["wgu"].shape[0]
    tm = MOE_BLOCK
    sb = tm // t_s
    n_tiles_p = n_p * t_p // tm

    (x_mid, xn2, route, gate, cnt, newa_p, newb_p, newa_s, newb_s) = _mixer_call(
        _to_time_major(xp, n_p), _to_time_major(xs, sb), _to_time_major(state_a, sb), _to_time_major(state_b, sb),
        p, tt_p=tm // n_p, tt_s=t_s, w_a=w_a, w_b=w_b)
    n_tok = x_mid.shape[0]
    n_tiles = n_tok // tm

    counts = cnt[0, :n_exp].astype(jnp.int32)
    padded = (counts + MOE_BLOCK - 1) // MOE_BLOCK * MOE_BLOCK
    pad_end = jnp.cumsum(padded)
    pad_start = pad_end - padded
    is_e = route[:TOP_K, :, None] == jnp.arange(n_exp, dtype=jnp.int32)
    dest = jnp.sum(jnp.where(is_e, pad_start, 0), axis=-1) + route[TOP_K:]
    dest_tiles = dest.reshape(TOP_K, n_tiles, tm).transpose(1, 0, 2).reshape(n_tiles, 1, TOP_K * tm)
    n_blocks = -(-(n_tok * TOP_K) // MOE_BLOCK) + n_exp
    n_used = (pad_end[-1] // MOE_BLOCK).astype(jnp.int32)
    blk_start = jnp.minimum(jnp.arange(n_blocks, dtype=jnp.int32) * MOE_BLOCK, pad_end[-1] - 1)
    block_e = jnp.minimum(jnp.sum(blk_start[:, None] >= pad_end[None, :], axis=1), n_exp - 1).astype(jnp.int32)
    last_blocks = jnp.arange(n_blocks - n_exp, n_blocks, dtype=jnp.int32)
    zero_start = jnp.concatenate([jnp.where(padded > 0, pad_end - MOE_BLOCK, -1),
                                  jnp.where(last_blocks >= n_used, last_blocks * MOE_BLOCK, -1)]).astype(jnp.int32)

    xb = _dispatch_call(xn2, dest_tiles, zero_start, n_blocks * MOE_BLOCK, tm=tm, d=d)
    yb = _expert_call(xb, block_e, n_used.reshape(1), p["wgu"], p["bgu"], p["wd"], p["bd"])
    y_p, y_s = _combine_call(x_mid, gate, dest_tiles, norm_final_g.reshape(1, d), yb, tm=tm, n_tiles_p=n_tiles_p)

    return (_from_time_major(y_p, n_p, n_p), _from_time_major(y_s, n_s, sb),
            _from_time_major(newa_p, n_p, n_p), _from_time_major(newb_p, n_p, n_p),
            _from_time_major(newa_s, n_s, sb), _from_time_major(newb_s, n_s, sb))


def _prep_params(l, norm_mix_g, w_in, b_gates, conv_a_w, conv_a_b, w_a_out, conv_b_w, conv_b_b, ln_b_g,
                 ln_b_b, w_b_out, w_o, norm_ffn_g, w_router, b_router, w_gu, b_gu, w_down, b_down):
    d = w_in.shape[1]
    n_exp = w_router.shape[-1]
    row = lambda v: v.reshape(1, -1)
    taps = lambda w: jnp.broadcast_to(w[:, None, :], (w.shape[0], SUBLANES, w.shape[1]))
    wr = jnp.zeros((d, LANES), F32).at[:, :n_exp].set(w_router[l])
    wr_hi = wr.astype(BF16)
    br = jnp.full((1, LANES), NEG_BIG, F32).at[0, :n_exp].set(b_router[l])
    return dict(
        gmix=row(norm_mix_g[l]), win=w_in[l].astype(BF16), bg=b_gates[l],
        caw=taps(conv_a_w[l]), cab=row(conv_a_b[l]), waout=w_a_out[l].astype(BF16),
        cbw=taps(conv_b_w[l]), cbb=row(conv_b_b[l]), lng=row(ln_b_g[l]), lnb=row(ln_b_b[l]),
        wbout=w_b_out[l].astype(BF16), wo=w_o[l].astype(BF16), gffn=row(norm_ffn_g[l]),
        wrh=wr_hi, wrl=(wr - wr_hi.astype(F32)).astype(BF16), br=br,
        wgu=w_gu[l], bgu=b_gu[l], wd=w_down[l], bd=b_down[l])


def kernel(x_prompt, x_sample, state_conv_a, state_conv_b, norm_mix_g, w_in, b_gates, conv_a_w, conv_a_b, w_a_out, conv_b_w, conv_b_b, ln_b_g, ln_b_b, w_b_out, w_o, norm_ffn_g, w_router, b_router, w_gu, b_gu, w_down, b_down, norm_final_g):
    depth = w_in.shape[0]
    assert depth == 1, "the final norm is fused into the last layer's combine call"
    p = _prep_params(0, norm_mix_g, w_in, b_gates, conv_a_w, conv_a_b, w_a_out, conv_b_w, conv_b_b, ln_b_g,
                     ln_b_b, w_b_out, w_o, norm_ffn_g, w_router, b_router, w_gu, b_gu, w_down, b_down)
    y_p, y_s, na_p, nb_p, na_s, nb_s = _layer(x_prompt, x_sample, state_conv_a[0], state_conv_b[0], p,
                                               norm_final_g)
    return (y_p, y_s, na_p[None], nb_p[None], na_s[None], nb_s[None])
```

```python
import functools

import jax
import jax.numpy as jnp
from jax import lax
from jax.experimental import pallas as pl
from jax.experimental.pallas import tpu as pltpu

EPS = 1e-5
SWIGLU_ALPHA = 1.702
SWIGLU_LIMIT = 7.0
TOP_K = 4
MOE_BLOCK = 256
TOKEN_TILE = 256
LANES = 128
SUBLANES = 8
NEG_BIG = -1e30
VMEM_LIMIT = 60 * 1024 * 1024
CONV_OUT_BLOCK = 8
ROW_DMA_UNROLL = 4
MIXER_SUBTILES = 1
CONV_LANES = 256

F32 = jnp.float32
BF16 = jnp.bfloat16


def _sigmoid(v):
    return 1.0 / (1.0 + jnp.exp(-v))


def _store_row_tiles(ref, value):
    n, d = value.shape
    pitch = d // LANES
    for c in range(pitch):
        ref[pl.ds(c, n, stride=pitch), :] = value[:, c * LANES:(c + 1) * LANES]


def _load_row_tiles(ref, n, d):
    pitch = d // LANES
    return jnp.concatenate([ref[pl.ds(c, n, stride=pitch), :] for c in range(pitch)], axis=1)


def _rms(v, g):
    return v * lax.rsqrt(jnp.mean(v * v, axis=-1, keepdims=True) + EPS) * g


def _causal_conv(ext_ref, w_ref, bias, out_ref, t_off, n_out, width, n_seq):
    assert n_out % CONV_OUT_BLOCK == 0

    def block(tb, carry):
        for sg in range(n_seq // SUBLANES):
            for lc in range(ext_ref.shape[-1] // CONV_LANES):
                _conv_pass(ext_ref, w_ref, bias, out_ref, t_off, tb * CONV_OUT_BLOCK, sg, lc, width, n_seq)
        return carry

    lax.fori_loop(0, n_out // CONV_OUT_BLOCK, block, 0)
    return out_ref[...]


def _conv_pass(ext_ref, w_ref, bias, out_ref, t_off, t0, sg, lc, width, n_seq):
    nb = CONV_OUT_BLOCK
    rows = pl.ds(sg * SUBLANES, SUBLANES)
    lanes = pl.ds(lc * CONV_LANES, CONV_LANES)
    loaded = {}
    acc = [None] * nb
    for k in range(width):
        wk = w_ref[k, :, lanes]
        for j in range(nb):
            if j + k not in loaded:
                loaded[j + k] = ext_ref[t_off + t0 + j + k, rows, lanes]
            term = wk * loaded[j + k]
            acc[j] = term if acc[j] is None else acc[j] + term
    for j, a in enumerate(acc):
        row0 = (t0 + j) * n_seq + sg * SUBLANES
        if not isinstance(row0, int):
            row0 = pl.multiple_of(row0, SUBLANES)
        out_ref[pl.ds(row0, SUBLANES), lanes] = a + bias[:, lc * CONV_LANES:(lc + 1) * CONV_LANES]


_MIXER_CONSTS = ("gmix", "win", "bg", "caw", "cab", "waout", "cbw", "cbb", "lng", "lnb", "wbout", "wo",
                 "gffn", "wrh", "wrl", "br")


def _time_major_rows(src_ref, buf, t0, tt):
    n_seq, _, d = src_ref.shape
    for s in range(n_seq):
        for c in range(d // LANES):
            buf[c, pl.ds(s, tt, stride=n_seq), :] = src_ref[s, t0:t0 + tt, c * LANES:(c + 1) * LANES]
    return jnp.concatenate([buf[c] for c in range(d // LANES)], axis=1)


def _store_seq_major(dst_ref, buf, value):
    n_seq, tt, d = dst_ref.shape
    for c in range(d // LANES):
        buf[c] = value[:, c * LANES:(c + 1) * LANES]
    for s in range(n_seq):
        for c in range(d // LANES):
            dst_ref[s, :, c * LANES:(c + 1) * LANES] = buf[c, pl.ds(s, tt, stride=n_seq), :]


def _mixer_subtiles(n_seq, exta, extb, c, xmid_ref, xn2_ref, route_ref, gate_ref, cnt_ref, cnt_acc, conv_buf,
                    subs):
    rows = TOKEN_TILE
    d = exta.shape[-1]
    tt = rows // n_seq
    w_a = c["caw"].shape[0]
    w_b = c["cbw"].shape[0]
    pitch = d // LANES
    st = [dict(x=a, e_t0=b, r0=r) for a, b, r in subs]

    for s in st:
        s["xn"] = _rms(s["x"], c["gmix"][...]).astype(BF16)

    def proj(s, g):
        return jnp.dot(s["xn"], c["win"][:, g * d:(g + 1) * d], preferred_element_type=F32)

    for s in st:
        lo = w_a - 1 + s["e_t0"]
        exta[lo:lo + tt] = (proj(s, 1) * proj(s, 2)).reshape(tt, n_seq, d)
    for s in st:
        lo = w_b - 1 + s["e_t0"]
        extb[lo:lo + tt] = (proj(s, 3) * _sigmoid(proj(s, 4))).reshape(tt, n_seq, d)
    for s in st:
        conv_a = _causal_conv(exta, c["caw"], c["cab"][...], conv_buf.at[0], s["e_t0"], tt, w_a, n_seq)
        s["y_a"] = jnp.dot((proj(s, 0) * conv_a).astype(BF16), c["waout"][...], preferred_element_type=F32)
    for s in st:
        conv_b = _causal_conv(extb, c["cbw"], c["cbb"][...], conv_buf.at[1], s["e_t0"], tt, w_b, n_seq)
        mu = jnp.mean(conv_b, axis=-1, keepdims=True)
        cen = conv_b - mu
        ln = cen * lax.rsqrt(jnp.mean(cen * cen, axis=-1, keepdims=True) + EPS) * c["lng"][...] + c["lnb"][...]
        s["y_b"] = jnp.dot((ln * _sigmoid(ln)).astype(BF16), c["wbout"][...], preferred_element_type=F32)
    bg = c["bg"]
    for s in st:
        merged = _sigmoid(proj(s, 5) + bg[0:1, :]) * s["y_a"] + _sigmoid(proj(s, 6) + bg[1:2, :]) * s["y_b"]
        x_mid = s["x"] + jnp.dot(merged.astype(BF16), c["wo"][...], preferred_element_type=F32)
        xmid_ref[s["r0"]:s["r0"] + rows, :] = x_mid
        s["xn2"] = _rms(x_mid, c["gffn"][...])
        _store_row_tiles(xn2_ref.at[pl.ds(s["r0"] * pitch, rows * pitch)], s["xn2"])

    lane = lax.broadcasted_iota(jnp.int32, (rows, LANES), 1)
    r_io = lax.broadcasted_iota(jnp.int32, (rows, rows), 0)
    c_io = lax.broadcasted_iota(jnp.int32, (rows, rows), 1)
    tri = (c_io < r_io).astype(BF16)
    cnt = cnt_acc[...]
    for s in st:
        x_hi = s["xn2"].astype(BF16)
        x_lo = (s["xn2"] - x_hi.astype(F32)).astype(BF16)
        logits = (jnp.dot(x_hi, c["wrh"][...], preferred_element_type=F32)
                  + jnp.dot(x_lo, c["wrh"][...], preferred_element_type=F32)
                  + jnp.dot(x_hi, c["wrl"][...], preferred_element_type=F32)) + c["br"][...]
        work = logits
        top_v, top_i = [], []
        for _ in range(TOP_K):
            m = jnp.max(work, axis=-1, keepdims=True)
            idx = jnp.min(jnp.where(work == m, lane, LANES), axis=-1, keepdims=True)
            top_v.append(m)
            top_i.append(idx)
            work = jnp.where(lane == idx, -jnp.inf, work)
        ex = [jnp.exp(v - top_v[0]) for v in top_v]
        den = ex[0] + ex[1] + ex[2] + ex[3]
        onehot = jnp.zeros((rows, LANES), F32)
        for idx in top_i:
            onehot = onehot + (lane == idx).astype(F32)
        prefix = jnp.dot(tri, onehot.astype(BF16), preferred_element_type=F32) + cnt
        route = jnp.zeros((rows, LANES), F32)
        gate = jnp.zeros((rows, LANES), F32)
        for k in range(TOP_K):
            pos = jnp.sum(jnp.where(lane == top_i[k], prefix, 0.0), axis=-1, keepdims=True)
            route = jnp.where(lane == k, top_i[k].astype(F32), route)
            route = jnp.where(lane == TOP_K + k, pos, route)
            gate = jnp.where(lane == k, ex[k] / den, gate)
        route_ref[:, s["r0"]:s["r0"] + rows] = jnp.transpose(route)[:2 * TOP_K].astype(jnp.int32)
        gate_ref[s["r0"]:s["r0"] + rows, :] = gate
        cnt = cnt + jnp.sum(onehot, axis=0, keepdims=True)
    cnt_acc[...] = cnt
    cnt_ref[...] = cnt


def _mixer_kernel(*refs, n_steps_p, n_sub):
    n_c = len(_MIXER_CONSTS)
    xp_ref, xs_ref, hsa_ref, hsb_ref = refs[:4]
    c = dict(zip(_MIXER_CONSTS, refs[4:4 + n_c]))
    (xmid_ref, xn2_ref, route_ref, gate_ref, cnt_ref,
     newa_p_ref, newb_p_ref, newa_s_ref, newb_s_ref) = refs[4 + n_c:13 + n_c]
    exta_p, extb_p, exta_s, extb_s, cnt_acc, conv_buf, xt_buf, sem = refs[13 + n_c:]
    i = pl.program_id(0)
    n_p, sb = xp_ref.shape[0], xs_ref.shape[1]
    tt_p = xp_ref.shape[1] // n_sub
    tt_s = xs_ref.shape[0] // n_sub
    w_a = c["caw"].shape[0]
    w_b = c["cbw"].shape[0]
    outs = (xmid_ref, xn2_ref, route_ref, gate_ref, cnt_ref, cnt_acc, conv_buf)

    @pl.when(i == 0)
    def _():
        exta_p[0:w_a - 1] = jnp.zeros((w_a - 1,) + exta_p.shape[1:], F32)
        extb_p[0:w_b - 1] = jnp.zeros((w_b - 1,) + extb_p.shape[1:], F32)
        cnt_acc[...] = jnp.zeros_like(cnt_acc)

    @pl.when(i < n_steps_p)
    def _():
        _mixer_subtiles(n_p, exta_p, extb_p, c, *outs,
                        subs=[(_time_major_rows(xp_ref, xt_buf, h * tt_p, tt_p), h * tt_p, h * TOKEN_TILE)
                              for h in range(n_sub)])
        exta_p[0:w_a - 1] = exta_p[n_sub * tt_p:n_sub * tt_p + w_a - 1]
        extb_p[0:w_b - 1] = extb_p[n_sub * tt_p:n_sub * tt_p + w_b - 1]

        @pl.when(i == n_steps_p - 1)
        def _():
            cp_a = pltpu.make_async_copy(exta_p.at[pl.ds(0, w_a - 1)], newa_p_ref, sem.at[0])
            cp_b = pltpu.make_async_copy(extb_p.at[pl.ds(0, w_b - 1)], newb_p_ref, sem.at[1])
            cp_a.start()
            cp_b.start()
            cp_a.wait()
            cp_b.wait()

    @pl.when(i >= n_steps_p)
    def _():
        for h in range(n_sub):
            q = (i - n_steps_p) * n_sub + h
            in_a = pltpu.make_async_copy(hsa_ref.at[pl.ds(q * (w_a - 1), w_a - 1)],
                                         exta_s.at[pl.ds(0, w_a - 1)], sem.at[0])
            in_b = pltpu.make_async_copy(hsb_ref.at[pl.ds(q * (w_b - 1), w_b - 1)],
                                         extb_s.at[pl.ds(0, w_b - 1)], sem.at[1])
            in_a.start()
            in_b.start()
            in_a.wait()
            in_b.wait()
            x_s = xs_ref[h * tt_s:(h + 1) * tt_s].reshape(TOKEN_TILE, xs_ref.shape[-1])
            _mixer_subtiles(sb, exta_s, extb_s, c, *outs, subs=[(x_s, 0, h * TOKEN_TILE)])
            out_a = pltpu.make_async_copy(exta_s.at[pl.ds(tt_s, w_a - 1)],
                                          newa_s_ref.at[pl.ds(q * (w_a - 1), w_a - 1)], sem.at[0])
            out_b = pltpu.make_async_copy(extb_s.at[pl.ds(tt_s, w_b - 1)],
                                          newb_s_ref.at[pl.ds(q * (w_b - 1), w_b - 1)], sem.at[1])
            out_a.start()
            out_b.start()
            out_a.wait()
            out_b.wait()


def _mixer_call(xp, xs_tm, hs_a, hs_b, params, *, tt_p, tt_s, w_a, w_b):
    n_p, t_p, d = xp.shape
    sb = xs_tm.shape[1]
    n_sub = MIXER_SUBTILES
    assert tt_p * n_p == tt_s * sb == TOKEN_TILE
    rows = n_sub * TOKEN_TILE
    n_steps_p = t_p // (tt_p * n_sub)
    n_steps_s = xs_tm.shape[0] // (tt_s * n_sub)
    assert n_steps_p * tt_p * n_sub == t_p and n_steps_s * tt_s * n_sub == xs_tm.shape[0]
    n_steps = n_steps_p + n_steps_s
    n_tok = n_steps * rows
    pitch = d // LANES
    consts = [params[n] for n in _MIXER_CONSTS]
    const_spec = lambda a: pl.BlockSpec(a.shape, lambda i, _nd=a.ndim: (0,) * _nd, pipeline_mode=pl.Buffered(1))
    any_spec = pl.BlockSpec(memory_space=pl.ANY)
    in_specs = [
        pl.BlockSpec((n_p, n_sub * tt_p, d), lambda i: (0, jnp.minimum(i, n_steps_p - 1), 0)),
        pl.BlockSpec((n_sub * tt_s, sb, d), lambda i: (jnp.maximum(i - n_steps_p, 0), 0, 0)),
        any_spec, any_spec,
    ] + [const_spec(a) for a in consts]
    out_shape = (
        jax.ShapeDtypeStruct((n_tok, d), F32),
        jax.ShapeDtypeStruct((n_tok * pitch, LANES), F32),
        jax.ShapeDtypeStruct((2 * TOP_K, n_tok), jnp.int32),
        jax.ShapeDtypeStruct((n_tok, LANES), F32),
        jax.ShapeDtypeStruct((1, LANES), F32),
        jax.ShapeDtypeStruct((w_a - 1, n_p, d), F32),
        jax.ShapeDtypeStruct((w_b - 1, n_p, d), F32),
        jax.ShapeDtypeStruct(hs_a.shape, F32),
        jax.ShapeDtypeStruct(hs_b.shape, F32),
    )
    out_specs = (
        pl.BlockSpec((rows, d), lambda i: (i, 0)),
        pl.BlockSpec((rows * pitch, LANES), lambda i: (i, 0)),
        pl.BlockSpec((2 * TOP_K, rows), lambda i: (0, i)),
        pl.BlockSpec((rows, LANES), lambda i: (i, 0)),
        pl.BlockSpec((1, LANES), lambda i: (0, 0)),
        any_spec, any_spec, any_spec, any_spec,
    )
    return pl.pallas_call(
        functools.partial(_mixer_kernel, n_steps_p=n_steps_p, n_sub=n_sub),
        grid=(n_steps,),
        in_specs=in_specs,
        out_specs=out_specs,
        out_shape=out_shape,
        scratch_shapes=[pltpu.VMEM((n_sub * tt_p + w_a - 1, n_p, d), F32),
                        pltpu.VMEM((n_sub * tt_p + w_b - 1, n_p, d), F32),
                        pltpu.VMEM((tt_s + w_a - 1, sb, d), F32),
                        pltpu.VMEM((tt_s + w_b - 1, sb, d), F32),
                        pltpu.VMEM((1, LANES), F32),
                        pltpu.VMEM((2, TOKEN_TILE, d), F32),
                        pltpu.VMEM((pitch, TOKEN_TILE, LANES), F32),
                        pltpu.SemaphoreType.DMA((2,))],
        compiler_params=pltpu.CompilerParams(dimension_semantics=("arbitrary",), vmem_limit_bytes=VMEM_LIMIT),
        name="mixer_router",
    )(xp, xs_tm, hs_a, hs_b, *consts)


def _dispatch_kernel(zero_ref, x_ref, dest_ref, xb_ref, zero_buf, sem, zsem, *, pitch):
    tm = x_ref.shape[0] // pitch
    n_zero = zero_ref.shape[0]
    blk_rows = zero_buf.shape[0]

    @pl.when(pl.program_id(0) == 0)
    def _():
        zero_buf[...] = jnp.zeros_like(zero_buf)

        def zcopy(e):
            start = pl.multiple_of(jnp.maximum(zero_ref[e], 0) * pitch, blk_rows)
            return pltpu.make_async_copy(zero_buf, xb_ref.at[pl.ds(start, blk_rows)], zsem)

        def start(e, c):
            @pl.when(zero_ref[e] >= 0)
            def _():
                zcopy(e).start()
            return c

        def wait(e, c):
            @pl.when(zero_ref[e] >= 0)
            def _():
                zcopy(e).wait()
            return c

        lax.fori_loop(0, n_zero, start, 0)
        lax.fori_loop(0, n_zero, wait, 0)

    def row_copy(r, dst):
        src = x_ref.at[pl.ds(pl.multiple_of(r * pitch, pitch), pitch)]
        return pltpu.make_async_copy(src, xb_ref.at[pl.ds(pl.multiple_of(dst * pitch, pitch), pitch)], sem)

    def start_rows(r, c):
        for k in range(TOP_K):
            row_copy(r, dest_ref[0, k * tm + r]).start(priority=k % 2)
        return c

    lax.fori_loop(0, tm, start_rows, 0, unroll=ROW_DMA_UNROLL)
    n_all = tm * TOP_K * pitch
    pltpu.make_async_copy(xb_ref.at[pl.ds(0, n_all)], xb_ref.at[pl.ds(0, n_all)], sem).wait()


def _dispatch_call(xn2_tiles, dest_tiles, zero_start, n_rows, *, tm, d):
    pitch = d // LANES
    n_tiles = xn2_tiles.shape[0] // (tm * pitch)
    grid_spec = pltpu.PrefetchScalarGridSpec(
        num_scalar_prefetch=1,
        grid=(n_tiles,),
        in_specs=[pl.BlockSpec((tm * pitch, LANES), lambda i, z: (i, 0)),
                  pl.BlockSpec((None, 1, tm * TOP_K), lambda i, z: (i, 0, 0), memory_space=pltpu.SMEM)],
        out_specs=pl.BlockSpec(memory_space=pl.ANY),
        scratch_shapes=[pltpu.VMEM((MOE_BLOCK * pitch, LANES), F32), pltpu.SemaphoreType.DMA,
                        pltpu.SemaphoreType.DMA],
    )
    return pl.pallas_call(
        functools.partial(_dispatch_kernel, pitch=pitch),
        grid_spec=grid_spec,
        out_shape=jax.ShapeDtypeStruct((n_rows * pitch, LANES), F32),
        compiler_params=pltpu.CompilerParams(dimension_semantics=("arbitrary",)),
        name="moe_dispatch",
    )(zero_start, xn2_tiles, dest_tiles)


def _expert_kernel(be_ref, nu_ref, first_ref, slot_ref, next_ref, x_ref, wgu_hbm, bgu_ref, wd_hbm, bd_ref, y_ref,
                   wgu_st, wd_st, wgu_bf, wd_bf, sem):
    b = pl.program_id(0)

    def fetch(e, slot):
        return (pltpu.make_async_copy(wgu_hbm.at[e], wgu_st.at[slot], sem.at[slot, 0]),
                pltpu.make_async_copy(wd_hbm.at[e], wd_st.at[slot], sem.at[slot, 1]))

    @pl.when(b < nu_ref[0])
    def _():
        d_ff, d = wd_bf.shape

        @pl.when(first_ref[b] == 1)
        def _():
            slot = slot_ref[b]

            @pl.when(b == 0)
            def _():
                for cp in fetch(be_ref[b], slot):
                    cp.start()

            for cp in fetch(be_ref[b], slot):
                cp.wait()

            @pl.when(next_ref[b] >= 0)
            def _():
                for cp in fetch(next_ref[b], 1 - slot):
                    cp.start()

            wgu_bf[...] = wgu_st[slot].astype(BF16)
            wd_bf[...] = wd_st[slot].astype(BF16)

        x = _load_row_tiles(x_ref, MOE_BLOCK, d)
        h = jnp.dot(x.astype(BF16), wgu_bf[...], preferred_element_type=F32) + bgu_ref[...]
        g = jnp.minimum(h[:, :d_ff], SWIGLU_LIMIT)
        u = jnp.clip(h[:, d_ff:], -SWIGLU_LIMIT, SWIGLU_LIMIT)
        act = (u + 1.0) * (g * _sigmoid(SWIGLU_ALPHA * g))
        y = jnp.dot(act.astype(BF16), wd_bf[...], preferred_element_type=F32) + bd_ref[...]
        _store_row_tiles(y_ref, y)

    @pl.when(b >= nu_ref[0])
    def _():
        y_ref[...] = jnp.zeros_like(y_ref)


def _expert_call(xb_tiles, block_e, n_used, run_first, run_slot, run_next, wgu, bgu, wd, bd):
    n_exp, d, two_ff = wgu.shape
    d_ff = wd.shape[1]
    blk_rows = MOE_BLOCK * d // LANES
    n_blocks = xb_tiles.shape[0] // blk_rows
    per_e = lambda b, be, *_: (be[b], 0, 0)
    any_spec = pl.BlockSpec(memory_space=pl.ANY)
    grid_spec = pltpu.PrefetchScalarGridSpec(
        num_scalar_prefetch=5,
        grid=(n_blocks,),
        in_specs=[pl.BlockSpec((blk_rows, LANES), lambda b, be, nu, *_: (jnp.minimum(b, nu[0] - 1), 0)),
                  any_spec,
                  pl.BlockSpec((None, 1, two_ff), per_e),
                  any_spec,
                  pl.BlockSpec((None, 1, d), per_e)],
        out_specs=pl.BlockSpec((blk_rows, LANES), lambda b, *_: (b, 0)),
        scratch_shapes=[pltpu.VMEM((2, d, two_ff), F32), pltpu.VMEM((2, d_ff, d), F32),
                        pltpu.VMEM((d, two_ff), BF16), pltpu.VMEM((d_ff, d), BF16),
                        pltpu.SemaphoreType.DMA((2, 2))],
    )
    return pl.pallas_call(
        _expert_kernel,
        grid_spec=grid_spec,
        out_shape=jax.ShapeDtypeStruct(xb_tiles.shape, F32),
        compiler_params=pltpu.CompilerParams(dimension_semantics=("arbitrary",), vmem_limit_bytes=VMEM_LIMIT),
        name="moe_experts",
    )(block_e, n_used, run_first, run_slot, run_next, xb_tiles, wgu, bgu.reshape(n_exp, 1, two_ff), wd,
      bd.reshape(n_exp, 1, d))


def _combine_kernel(xmid_ref, gate_ref, dest_ref, dest_next_ref, gfin_ref, yb_ref, outp_ref, outs_ref,
                    ybuf, ot_buf, sem, *, n_tiles_p):
    i = pl.program_id(0)
    n_tiles = pl.num_programs(0)
    tm, d = xmid_ref.shape
    pitch = d // LANES

    def start_tile(dref, slot):
        def start_rows(r, c):
            for k in range(TOP_K):
                src = dref[0, k * tm + r]
                pltpu.make_async_copy(yb_ref.at[pl.ds(pl.multiple_of(src * pitch, pitch), pitch)],
                                      ybuf.at[slot, k, pl.ds(pl.multiple_of(r * pitch, pitch), pitch)],
                                      sem.at[slot]).start(priority=k % 2)
            return c

        lax.fori_loop(0, tm, start_rows, 0, unroll=ROW_DMA_UNROLL)

    @pl.when(i == 0)
    def _():
        start_tile(dest_ref, 0)

    slot = i % 2

    @pl.when(i + 1 < n_tiles)
    def _():
        start_tile(dest_next_ref, 1 - slot)

    pltpu.make_async_copy(ybuf.at[slot], ybuf.at[slot], sem.at[slot]).wait()
    gate = gate_ref[...]
    y = xmid_ref[...]
    for k in range(TOP_K):
        y = y + gate[:, k:k + 1] * _load_row_tiles(ybuf.at[slot, k], tm, d)
    out = _rms(y, gfin_ref[...])

    @pl.when(i < n_tiles_p)
    def _():
        _store_seq_major(outp_ref, ot_buf, out)

    @pl.when(i >= n_tiles_p)
    def _():
        outs_ref[...] = out


def _combine_call(x_mid, gate, dest_tiles, g_final, yb, *, tm, n_p, n_tiles_p):
    n_tok, d = x_mid.shape
    n_tiles = n_tok // tm
    tt = tm // n_p
    dest_spec = lambda off: pl.BlockSpec((None, 1, tm * TOP_K),
                                         lambda i: (jnp.minimum(i + off, n_tiles - 1), 0, 0),
                                         memory_space=pltpu.SMEM)
    return pl.pallas_call(
        functools.partial(_combine_kernel, n_tiles_p=n_tiles_p),
        grid=(n_tiles,),
        in_specs=[pl.BlockSpec((tm, d), lambda i: (i, 0)),
                  pl.BlockSpec((tm, LANES), lambda i: (i, 0)),
                  dest_spec(0), dest_spec(1),
                  pl.BlockSpec((1, d), lambda i: (0, 0)),
                  pl.BlockSpec(memory_space=pl.ANY)],
        out_specs=(pl.BlockSpec((n_p, tt, d), lambda i: (0, jnp.minimum(i, n_tiles_p - 1), 0)),
                   pl.BlockSpec((tm, d), lambda i: (jnp.maximum(i - n_tiles_p, 0), 0))),
        out_shape=(jax.ShapeDtypeStruct((n_p, n_tiles_p * tt, d), F32),
                   jax.ShapeDtypeStruct(((n_tiles - n_tiles_p) * tm, d), F32)),
        scratch_shapes=[pltpu.VMEM((2, TOP_K, tm * d // LANES, LANES), F32),
                        pltpu.VMEM((d // LANES, tm, LANES), F32),
                        pltpu.SemaphoreType.DMA((2,))],
        compiler_params=pltpu.CompilerParams(dimension_semantics=("arbitrary",), vmem_limit_bytes=VMEM_LIMIT),
        name="moe_combine",
    )(x_mid, gate, dest_tiles, dest_tiles, g_final, yb)


def _to_time_major(x, seq_block):
    n_seqs, t, d = x.shape
    n_sb = n_seqs // seq_block
    return x.reshape(n_sb, seq_block, t, d).transpose(0, 2, 1, 3).reshape(n_sb * t, seq_block, d)


def _from_time_major(x, n_seqs, seq_block):
    d = x.shape[-1]
    n_sb = n_seqs // seq_block
    t = x.size // (n_seqs * d)
    return x.reshape(n_sb, t, seq_block, d).transpose(0, 2, 1, 3).reshape(n_seqs, t, d)


def _layer(xp, xs, state_a, state_b, p, norm_final_g):
    n_p, t_p, d = xp.shape
    n_s, t_s, _ = xs.shape
    w_a = p["caw"].shape[0]
    w_b = p["cbw"].shape[0]
    n_exp = p["wgu"].shape[0]
    tm = TOKEN_TILE
    sb = tm // t_s
    n_tiles_p = n_p * t_p // tm

    (x_mid, xn2, route, gate, cnt, newa_p, newb_p, newa_s, newb_s) = _mixer_call(
        xp, _to_time_major(xs, sb), _to_time_major(state_a, sb), _to_time_major(state_b, sb),
        p, tt_p=tm // n_p, tt_s=t_s, w_a=w_a, w_b=w_b)
    n_tok = x_mid.shape[0]
    n_tiles = n_tok // tm

    counts = cnt[0, :n_exp].astype(jnp.int32)
    padded = (counts + MOE_BLOCK - 1) // MOE_BLOCK * MOE_BLOCK
    pad_end = jnp.cumsum(padded)
    pad_start = pad_end - padded
    is_e = route[:TOP_K, :, None] == jnp.arange(n_exp, dtype=jnp.int32)
    dest = jnp.sum(jnp.where(is_e, pad_start, 0), axis=-1) + route[TOP_K:]
    dest_tiles = dest.reshape(TOP_K, n_tiles, tm).transpose(1, 0, 2).reshape(n_tiles, 1, TOP_K * tm)
    n_blocks = -(-(n_tok * TOP_K) // MOE_BLOCK) + n_exp
    n_used = (pad_end[-1] // MOE_BLOCK).astype(jnp.int32)
    blk_start = jnp.minimum(jnp.arange(n_blocks, dtype=jnp.int32) * MOE_BLOCK, pad_end[-1] - 1)
    block_e = jnp.minimum(jnp.sum(blk_start[:, None] >= pad_end[None, :], axis=1), n_exp - 1).astype(jnp.int32)
    last_blocks = jnp.arange(n_blocks - n_exp, n_blocks, dtype=jnp.int32)
    zero_start = jnp.concatenate([jnp.where(padded > 0, pad_end - MOE_BLOCK, -1),
                                  jnp.where(last_blocks >= n_used, last_blocks * MOE_BLOCK, -1)]).astype(jnp.int32)

    xb = _dispatch_call(xn2, dest_tiles, zero_start, n_blocks * MOE_BLOCK, tm=tm, d=d)
    blk_ids = jnp.arange(n_blocks, dtype=jnp.int32)
    prev_e = jnp.concatenate([jnp.full((1,), -1, jnp.int32), block_e[:-1]])
    run_first = ((block_e != prev_e) & (blk_ids < n_used)).astype(jnp.int32)
    run_slot = ((jnp.cumsum(run_first) - 1) % 2).astype(jnp.int32)
    e_ids = jnp.arange(n_exp, dtype=jnp.int32)
    later = lax.cummin(jnp.where(padded > 0, e_ids, n_exp), axis=0, reverse=True)
    next_of = jnp.concatenate([later[1:], jnp.full((1,), n_exp, jnp.int32)])
    next_of = jnp.where(next_of >= n_exp, -1, next_of)
    run_next = jnp.sum(jnp.where(block_e[:, None] == e_ids[None, :], next_of[None, :], 0), axis=1).astype(jnp.int32)

    yb = _expert_call(xb, block_e, n_used.reshape(1), run_first, run_slot, run_next,
                      p["wgu"], p["bgu"], p["wd"], p["bd"])
    y_p, y_s = _combine_call(x_mid, gate, dest_tiles, norm_final_g.reshape(1, d), yb, tm=tm, n_p=n_p,
                             n_tiles_p=n_tiles_p)

    return (y_p, _from_time_major(y_s, n_s, sb),
            _from_time_major(newa_p, n_p, n_p), _from_time_major(newb_p, n_p, n_p),
            _from_time_major(newa_s, n_s, sb), _from_time_major(newb_s, n_s, sb))


def _prep_params(l, norm_mix_g, w_in, b_gates, conv_a_w, conv_a_b, w_a_out, conv_b_w, conv_b_b, ln_b_g,
                 ln_b_b, w_b_out, w_o, norm_ffn_g, w_router, b_router, w_gu, b_gu, w_down, b_down):
    d = w_in.shape[1]
    n_exp = w_router.shape[-1]
    row = lambda v: v.reshape(1, -1)
    taps = lambda w: jnp.broadcast_to(w[:, None, :], (w.shape[0], SUBLANES, w.shape[1]))
    wr = jnp.zeros((d, LANES), F32).at[:, :n_exp].set(w_router[l])
    wr_hi = wr.astype(BF16)
    br = jnp.full((1, LANES), NEG_BIG, F32).at[0, :n_exp].set(b_router[l])
    return dict(
        gmix=row(norm_mix_g[l]), win=w_in[l].astype(BF16), bg=b_gates[l],
        caw=taps(conv_a_w[l]), cab=row(conv_a_b[l]), waout=w_a_out[l].astype(BF16),
        cbw=taps(conv_b_w[l]), cbb=row(conv_b_b[l]), lng=row(ln_b_g[l]), lnb=row(ln_b_b[l]),
        wbout=w_b_out[l].astype(BF16), wo=w_o[l].astype(BF16), gffn=row(norm_ffn_g[l]),
        wrh=wr_hi, wrl=(wr - wr_hi.astype(F32)).astype(BF16), br=br,
        wgu=w_gu[l], bgu=b_gu[l], wd=w_down[l], bd=b_down[l])


def kernel(x_prompt, x_sample, state_conv_a, state_conv_b, norm_mix_g, w_in, b_gates, conv_a_w, conv_a_b, w_a_out, conv_b_w, conv_b_b, ln_b_g, ln_b_b, w_b_out, w_o, norm_ffn_g, w_router, b_router, w_gu, b_gu, w_down, b_down, norm_final_g):
    depth = w_in.shape[0]
    assert depth == 1, "the final norm is fused into the last layer's combine call"
    p = _prep_params(0, norm_mix_g, w_in, b_gates, conv_a_w, conv_a_b, w_a_out, conv_b_w, conv_b_b, ln_b_g,
                     ln_b_b, w_b_out, w_o, norm_ffn_g, w_router, b_router, w_gu, b_gu, w_down, b_down)
    y_p, y_s, na_p, nb_p, na_s, nb_s = _layer(x_prompt, x_sample, state_conv_a[0], state_conv_b[0], p,
                                               norm_final_g)
    return (y_p, y_s, na_p[None], nb_p[None], na_s[None], nb_s[None])
```

```python
import functools

import jax
import jax.numpy as jnp
from jax import lax
from jax.experimental import pallas as pl
from jax.experimental.pallas import tpu as pltpu

EPS = 1e-5
SWIGLU_ALPHA = 1.702
SWIGLU_LIMIT = 7.0
TOP_K = 4
MOE_BLOCK = 256
TOKEN_TILE = 256
LANES = 128
SUBLANES = 8
NEG_BIG = -1e30
VMEM_LIMIT = 60 * 1024 * 1024
CONV_OUT_BLOCK = 8
ROW_DMA_UNROLL = 4
CONV_LANES = 256
CONV_LOOP_UNROLL = 4

F32 = jnp.float32
BF16 = jnp.bfloat16


def _sigmoid(v):
    return 1.0 / (1.0 + jnp.exp(-v))


def _store_row_tiles(ref, value):
    n, d = value.shape
    pitch = d // LANES
    for c in range(pitch):
        ref[pl.ds(c, n, stride=pitch), :] = value[:, c * LANES:(c + 1) * LANES]


def _load_row_tiles(ref, n, d):
    pitch = d // LANES
    return jnp.concatenate([ref[pl.ds(c, n, stride=pitch), :] for c in range(pitch)], axis=1)


def _rms(v, g):
    return v * lax.rsqrt(jnp.mean(v * v, axis=-1, keepdims=True) + EPS) * g


def _conv_pass_count(n_out, n_seq, n_lc):
    assert n_out % CONV_OUT_BLOCK == 0 and n_seq % SUBLANES == 0
    return (n_out // CONV_OUT_BLOCK) * (n_seq // SUBLANES) * n_lc


def _conv_pass(ext_ref, w_ref, bias_ref, out_ref, p, n_seq):
    n_lc, width = w_ref.shape[:2]
    n_sg = n_seq // SUBLANES
    nb = CONV_OUT_BLOCK
    lc = lax.rem(p, n_lc)
    rest = lax.div(p, n_lc)
    sg = lax.rem(rest, n_sg)
    t0 = lax.div(rest, n_sg) * nb
    rows = pl.ds(pl.multiple_of(sg * SUBLANES, SUBLANES), SUBLANES)
    loaded = {}
    acc = [None] * nb
    for k in range(width):
        wk = w_ref[lc, k]
        for j in range(nb):
            if j + k not in loaded:
                loaded[j + k] = ext_ref[lc, t0 + j + k, rows, :]
            term = wk * loaded[j + k]
            acc[j] = term if acc[j] is None else acc[j] + term
    for j, a in enumerate(acc):
        row0 = pl.multiple_of((t0 + j) * n_seq + sg * SUBLANES, SUBLANES)
        out_ref[lc, pl.ds(row0, SUBLANES), :] = a + bias_ref[lc]


def _store_chunks(ext_ref, lo, value, n_seq):
    n_lc, _, _, cl = ext_ref.shape
    tt = value.shape[0] // n_seq
    for lc in range(n_lc):
        ext_ref[lc, lo:lo + tt] = value[:, lc * cl:(lc + 1) * cl].reshape(tt, n_seq, cl)


def _read_chunks(out_ref):
    return jnp.concatenate([out_ref[lc] for lc in range(out_ref.shape[0])], axis=1)


_MIXER_CONSTS = ("gmix", "win14", "wgate", "bg", "caw", "cab", "waout", "cbw", "cbb", "lng", "lnb", "wbout", "wo",
                 "gffn", "wrh", "wrl", "br")


def _time_major_rows(src_ref, buf, t0, tt):
    n_seq, _, d = src_ref.shape
    for s in range(n_seq):
        for c in range(d // LANES):
            buf[c, pl.ds(s, tt, stride=n_seq), :] = src_ref[s, t0:t0 + tt, c * LANES:(c + 1) * LANES]
    return jnp.concatenate([buf[c] for c in range(d // LANES)], axis=1)


def _store_seq_major(dst_ref, buf, value):
    n_seq, tt, d = dst_ref.shape
    for c in range(d // LANES):
        buf[c] = value[:, c * LANES:(c + 1) * LANES]
    for s in range(n_seq):
        for c in range(d // LANES):
            dst_ref[s, :, c * LANES:(c + 1) * LANES] = buf[c, pl.ds(s, tt, stride=n_seq), :]


def _mixer_tile(x, n_seq, exta, extb, c, xmid_ref, xn2_ref, route_ref, gate_ref, cnt_ref, cnt_acc,
                conv_a_buf, conv_b_buf, lhs_buf, res_buf):
    rows, d = x.shape
    n_lc, w_a = c["caw"].shape[:2]
    w_b = c["cbw"].shape[1]
    cl = d // n_lc
    tt = rows // n_seq
    n_pass = _conv_pass_count(tt, n_seq, n_lc)
    n_gate = c["wgate"].shape[0]
    n_ya = c["waout"].shape[0]
    assert n_pass == n_gate + n_ya and n_gate == 3 * n_lc

    xn = _rms(x, c["gmix"][...]).astype(BF16)
    lhs_buf[0] = xn

    def proj(g):
        return jnp.dot(xn, c["win14"][:, (g - 1) * d:g * d], preferred_element_type=F32)

    _store_chunks(exta, w_a - 1, proj(1) * proj(2), n_seq)
    _store_chunks(extb, w_b - 1, proj(3) * _sigmoid(proj(4)), n_seq)

    def pass_a(p, carry):
        _conv_pass(exta, c["caw"], c["cab"], conv_a_buf, p, n_seq)
        return carry

    lax.fori_loop(0, n_pass, pass_a, 0)
    conv_a = _read_chunks(conv_a_buf)

    def pass_b_gate(p, carry):
        _conv_pass(extb, c["cbw"], c["cbb"], conv_b_buf, p, n_seq)
        res_buf[p] = jnp.dot(lhs_buf[0], c["wgate"][p], preferred_element_type=F32)
        return carry

    lax.fori_loop(0, n_gate, pass_b_gate, 0, unroll=CONV_LOOP_UNROLL)
    p0 = jnp.concatenate([res_buf[n] for n in range(n_lc)], axis=1)
    lhs_buf[1] = (p0 * conv_a).astype(BF16)

    def pass_b_ya(p, carry):
        _conv_pass(extb, c["cbw"], c["cbb"], conv_b_buf, n_gate + p, n_seq)
        res_buf[n_gate + p] = jnp.dot(lhs_buf[1], c["waout"][p], preferred_element_type=F32)
        return carry

    lax.fori_loop(0, n_ya, pass_b_ya, 0, unroll=CONV_LOOP_UNROLL)
    gate_a = jnp.concatenate([res_buf[n_lc + n] for n in range(n_lc)], axis=1)
    gate_b = jnp.concatenate([res_buf[2 * n_lc + n] for n in range(n_lc)], axis=1)
    y_a = jnp.concatenate([res_buf[n_gate + n] for n in range(n_ya)], axis=1)

    conv_b = _read_chunks(conv_b_buf)
    mu = jnp.mean(conv_b, axis=-1, keepdims=True)
    cen = conv_b - mu
    ln = cen * lax.rsqrt(jnp.mean(cen * cen, axis=-1, keepdims=True) + EPS) * c["lng"][...] + c["lnb"][...]
    y_b = jnp.dot((ln * _sigmoid(ln)).astype(BF16), c["wbout"][...], preferred_element_type=F32)

    bg = c["bg"]
    merged = _sigmoid(gate_a + bg[0:1, :]) * y_a + _sigmoid(gate_b + bg[1:2, :]) * y_b
    x_mid = x + jnp.dot(merged.astype(BF16), c["wo"][...], preferred_element_type=F32)
    xmid_ref[...] = x_mid
    xn2 = _rms(x_mid, c["gffn"][...])
    _store_row_tiles(xn2_ref, xn2)

    lane = lax.broadcasted_iota(jnp.int32, (rows, LANES), 1)
    r_io = lax.broadcasted_iota(jnp.int32, (rows, rows), 0)
    c_io = lax.broadcasted_iota(jnp.int32, (rows, rows), 1)
    tri = (c_io < r_io).astype(BF16)
    x_hi = xn2.astype(BF16)
    x_lo = (xn2 - x_hi.astype(F32)).astype(BF16)
    logits = (jnp.dot(x_hi, c["wrh"][...], preferred_element_type=F32)
              + jnp.dot(x_lo, c["wrh"][...], preferred_element_type=F32)
              + jnp.dot(x_hi, c["wrl"][...], preferred_element_type=F32)) + c["br"][...]
    work = logits
    top_v, top_i = [], []
    for _ in range(TOP_K):
        m = jnp.max(work, axis=-1, keepdims=True)
        idx = jnp.min(jnp.where(work == m, lane, LANES), axis=-1, keepdims=True)
        top_v.append(m)
        top_i.append(idx)
        work = jnp.where(lane == idx, -jnp.inf, work)
    ex = [jnp.exp(v - top_v[0]) for v in top_v]
    den = ex[0] + ex[1] + ex[2] + ex[3]
    onehot = jnp.zeros((rows, LANES), F32)
    for idx in top_i:
        onehot = onehot + (lane == idx).astype(F32)
    prefix = jnp.dot(tri, onehot.astype(BF16), preferred_element_type=F32) + cnt_acc[...]
    route = jnp.zeros((rows, LANES), F32)
    gate = jnp.zeros((rows, LANES), F32)
    for k in range(TOP_K):
        pos = jnp.sum(jnp.where(lane == top_i[k], prefix, 0.0), axis=-1, keepdims=True)
        route = jnp.where(lane == k, top_i[k].astype(F32), route)
        route = jnp.where(lane == TOP_K + k, pos, route)
        gate = jnp.where(lane == k, ex[k] / den, gate)
    route_ref[...] = jnp.transpose(route)[:2 * TOP_K].astype(jnp.int32)
    gate_ref[...] = gate
    cnt = cnt_acc[...] + jnp.sum(onehot, axis=0, keepdims=True)
    cnt_acc[...] = cnt
    cnt_ref[...] = cnt


def _mixer_kernel(*refs, n_steps_p):
    n_c = len(_MIXER_CONSTS)
    xp_ref, xs_ref, hsa_ref, hsb_ref = refs[:4]
    c = dict(zip(_MIXER_CONSTS, refs[4:4 + n_c]))
    (xmid_ref, xn2_ref, route_ref, gate_ref, cnt_ref,
     newa_p_ref, newb_p_ref, newa_s_ref, newb_s_ref) = refs[4 + n_c:13 + n_c]
    (exta_p, extb_p, exta_s, extb_s, cnt_acc, conv_a_buf, conv_b_buf, lhs_buf, res_buf, xt_buf,
     sem) = refs[13 + n_c:]
    i = pl.program_id(0)
    n_p, sb = xp_ref.shape[0], xs_ref.shape[1]
    tt_p = xp_ref.shape[1]
    tt_s = xs_ref.shape[0]
    w_a = c["caw"].shape[1]
    w_b = c["cbw"].shape[1]
    tile_args = (c, xmid_ref, xn2_ref, route_ref, gate_ref, cnt_ref, cnt_acc,
                 conv_a_buf, conv_b_buf, lhs_buf, res_buf)
    hist = lambda ref, w: ref.at[:, pl.ds(0, w - 1)]
    tail = lambda ref, tt, w: ref.at[:, pl.ds(tt, w - 1)]

    @pl.when(i == 0)
    def _():
        hist(exta_p, w_a)[...] = jnp.zeros(hist(exta_p, w_a).shape, F32)
        hist(extb_p, w_b)[...] = jnp.zeros(hist(extb_p, w_b).shape, F32)
        cnt_acc[...] = jnp.zeros_like(cnt_acc)

    @pl.when(i < n_steps_p)
    def _():
        _mixer_tile(_time_major_rows(xp_ref, xt_buf, 0, tt_p), n_p, exta_p, extb_p, *tile_args)
        hist(exta_p, w_a)[...] = tail(exta_p, tt_p, w_a)[...]
        hist(extb_p, w_b)[...] = tail(extb_p, tt_p, w_b)[...]

        @pl.when(i == n_steps_p - 1)
        def _():
            cp_a = pltpu.make_async_copy(hist(exta_p, w_a), newa_p_ref, sem.at[0])
            cp_b = pltpu.make_async_copy(hist(extb_p, w_b), newb_p_ref, sem.at[1])
            cp_a.start()
            cp_b.start()
            cp_a.wait()
            cp_b.wait()

    @pl.when(i >= n_steps_p)
    def _():
        q = i - n_steps_p
        in_a = pltpu.make_async_copy(hsa_ref.at[q], hist(exta_s, w_a), sem.at[0])
        in_b = pltpu.make_async_copy(hsb_ref.at[q], hist(extb_s, w_b), sem.at[1])
        in_a.start()
        in_b.start()
        in_a.wait()
        in_b.wait()
        _mixer_tile(xs_ref[...].reshape(TOKEN_TILE, xs_ref.shape[-1]), sb, exta_s, extb_s, *tile_args)
        out_a = pltpu.make_async_copy(tail(exta_s, tt_s, w_a), newa_s_ref.at[q], sem.at[0])
        out_b = pltpu.make_async_copy(tail(extb_s, tt_s, w_b), newb_s_ref.at[q], sem.at[1])
        out_a.start()
        out_b.start()
        out_a.wait()
        out_b.wait()


def _mixer_call(xp, xs_tm, hs_a, hs_b, params, *, tt_p, tt_s):
    n_p, t_p, d = xp.shape
    sb = xs_tm.shape[1]
    n_lc, w_a = params["caw"].shape[:2]
    w_b = params["cbw"].shape[1]
    cl = d // n_lc
    assert tt_p * n_p == tt_s * sb == TOKEN_TILE
    rows = TOKEN_TILE
    n_steps_p = t_p // tt_p
    n_steps_s = xs_tm.shape[0] // tt_s
    assert n_steps_p * tt_p == t_p and n_steps_s * tt_s == xs_tm.shape[0]
    n_steps = n_steps_p + n_steps_s
    n_tok = n_steps * rows
    pitch = d // LANES
    n_chunks = params["wgate"].shape[0] + params["waout"].shape[0]
    consts = [params[n] for n in _MIXER_CONSTS]
    const_spec = lambda a: pl.BlockSpec(a.shape, lambda i, _nd=a.ndim: (0,) * _nd, pipeline_mode=pl.Buffered(1))
    any_spec = pl.BlockSpec(memory_space=pl.ANY)
    in_specs = [
        pl.BlockSpec((n_p, tt_p, d), lambda i: (0, jnp.minimum(i, n_steps_p - 1), 0)),
        pl.BlockSpec((tt_s, sb, d), lambda i: (jnp.maximum(i - n_steps_p, 0), 0, 0)),
        any_spec, any_spec,
    ] + [const_spec(a) for a in consts]
    out_shape = (
        jax.ShapeDtypeStruct((n_tok, d), F32),
        jax.ShapeDtypeStruct((n_tok * pitch, LANES), F32),
        jax.ShapeDtypeStruct((2 * TOP_K, n_tok), jnp.int32),
        jax.ShapeDtypeStruct((n_tok, LANES), F32),
        jax.ShapeDtypeStruct((1, LANES), F32),
        jax.ShapeDtypeStruct((n_lc, w_a - 1, n_p, cl), F32),
        jax.ShapeDtypeStruct((n_lc, w_b - 1, n_p, cl), F32),
        jax.ShapeDtypeStruct(hs_a.shape, F32),
        jax.ShapeDtypeStruct(hs_b.shape, F32),
    )
    out_specs = (
        pl.BlockSpec((rows, d), lambda i: (i, 0)),
        pl.BlockSpec((rows * pitch, LANES), lambda i: (i, 0)),
        pl.BlockSpec((2 * TOP_K, rows), lambda i: (0, i)),
        pl.BlockSpec((rows, LANES), lambda i: (i, 0)),
        pl.BlockSpec((1, LANES), lambda i: (0, 0)),
        any_spec, any_spec, any_spec, any_spec,
    )
    return pl.pallas_call(
        functools.partial(_mixer_kernel, n_steps_p=n_steps_p),
        grid=(n_steps,),
        in_specs=in_specs,
        out_specs=out_specs,
        out_shape=out_shape,
        scratch_shapes=[pltpu.VMEM((n_lc, tt_p + w_a - 1, n_p, cl), F32),
                        pltpu.VMEM((n_lc, tt_p + w_b - 1, n_p, cl), F32),
                        pltpu.VMEM((n_lc, tt_s + w_a - 1, sb, cl), F32),
                        pltpu.VMEM((n_lc, tt_s + w_b - 1, sb, cl), F32),
                        pltpu.VMEM((1, LANES), F32),
                        pltpu.VMEM((n_lc, rows, cl), F32),
                        pltpu.VMEM((n_lc, rows, cl), F32),
                        pltpu.VMEM((2, rows, d), BF16),
                        pltpu.VMEM((n_chunks, rows, cl), F32),
                        pltpu.VMEM((pitch, rows, LANES), F32),
                        pltpu.SemaphoreType.DMA((2,))],
        compiler_params=pltpu.CompilerParams(dimension_semantics=("arbitrary",), vmem_limit_bytes=VMEM_LIMIT),
        name="mixer_router",
    )(xp, xs_tm, hs_a, hs_b, *consts)


def _dispatch_kernel(zero_ref, x_ref, dest_ref, xb_ref, zero_buf, sem, zsem, *, pitch):
    tm = x_ref.shape[0] // pitch
    n_zero = zero_ref.shape[0]
    blk_rows = zero_buf.shape[0]

    @pl.when(pl.program_id(0) == 0)
    def _():
        zero_buf[...] = jnp.zeros_like(zero_buf)

        def zcopy(e):
            start = pl.multiple_of(jnp.maximum(zero_ref[e], 0) * pitch, blk_rows)
            return pltpu.make_async_copy(zero_buf, xb_ref.at[pl.ds(start, blk_rows)], zsem)

        def start(e, c):
            @pl.when(zero_ref[e] >= 0)
            def _():
                zcopy(e).start()
            return c

        def wait(e, c):
            @pl.when(zero_ref[e] >= 0)
            def _():
                zcopy(e).wait()
            return c

        lax.fori_loop(0, n_zero, start, 0)
        lax.fori_loop(0, n_zero, wait, 0)

    def row_copy(r, dst):
        src = x_ref.at[pl.ds(pl.multiple_of(r * pitch, pitch), pitch)]
        return pltpu.make_async_copy(src, xb_ref.at[pl.ds(pl.multiple_of(dst * pitch, pitch), pitch)], sem)

    def start_rows(r, c):
        for k in range(TOP_K):
            row_copy(r, dest_ref[0, k * tm + r]).start(priority=k % 2)
        return c

    lax.fori_loop(0, tm, start_rows, 0, unroll=ROW_DMA_UNROLL)
    n_all = tm * TOP_K * pitch
    pltpu.make_async_copy(xb_ref.at[pl.ds(0, n_all)], xb_ref.at[pl.ds(0, n_all)], sem).wait()


def _dispatch_call(xn2_tiles, dest_tiles, zero_start, n_rows, *, tm, d):
    pitch = d // LANES
    n_tiles = xn2_tiles.shape[0] // (tm * pitch)
    grid_spec = pltpu.PrefetchScalarGridSpec(
        num_scalar_prefetch=1,
        grid=(n_tiles,),
        in_specs=[pl.BlockSpec((tm * pitch, LANES), lambda i, z: (i, 0)),
                  pl.BlockSpec((None, 1, tm * TOP_K), lambda i, z: (i, 0, 0), memory_space=pltpu.SMEM)],
        out_specs=pl.BlockSpec(memory_space=pl.ANY),
        scratch_shapes=[pltpu.VMEM((MOE_BLOCK * pitch, LANES), F32), pltpu.SemaphoreType.DMA,
                        pltpu.SemaphoreType.DMA],
    )
    return pl.pallas_call(
        functools.partial(_dispatch_kernel, pitch=pitch),
        grid_spec=grid_spec,
        out_shape=jax.ShapeDtypeStruct((n_rows * pitch, LANES), F32),
        compiler_params=pltpu.CompilerParams(dimension_semantics=("arbitrary",)),
        name="moe_dispatch",
    )(zero_start, xn2_tiles, dest_tiles)


def _expert_kernel(be_ref, nu_ref, first_ref, slot_ref, next_ref, x_ref, wgu_hbm, bgu_ref, wd_hbm, bd_ref, y_ref,
                   wgu_st, wd_st, wgu_bf, wd_bf, sem):
    b = pl.program_id(0)

    def fetch(e, slot):
        return (pltpu.make_async_copy(wgu_hbm.at[e], wgu_st.at[slot], sem.at[slot, 0]),
                pltpu.make_async_copy(wd_hbm.at[e], wd_st.at[slot], sem.at[slot, 1]))

    @pl.when(b < nu_ref[0])
    def _():
        d_ff, d = wd_bf.shape

        @pl.when(first_ref[b] == 1)
        def _():
            slot = slot_ref[b]

            @pl.when(b == 0)
            def _():
                for cp in fetch(be_ref[b], slot):
                    cp.start()

            for cp in fetch(be_ref[b], slot):
                cp.wait()

            @pl.when(next_ref[b] >= 0)
            def _():
                for cp in fetch(next_ref[b], 1 - slot):
                    cp.start()

            wgu_bf[...] = wgu_st[slot].astype(BF16)
            wd_bf[...] = wd_st[slot].astype(BF16)

        x = _load_row_tiles(x_ref, MOE_BLOCK, d)
        h = jnp.dot(x.astype(BF16), wgu_bf[...], preferred_element_type=F32) + bgu_ref[...]
        g = jnp.minimum(h[:, :d_ff], SWIGLU_LIMIT)
        u = jnp.clip(h[:, d_ff:], -SWIGLU_LIMIT, SWIGLU_LIMIT)
        act = (u + 1.0) * (g * _sigmoid(SWIGLU_ALPHA * g))
        y = jnp.dot(act.astype(BF16), wd_bf[...], preferred_element_type=F32) + bd_ref[...]
        _store_row_tiles(y_ref, y)

    @pl.when(b >= nu_ref[0])
    def _():
        y_ref[...] = jnp.zeros_like(y_ref)


def _expert_call(xb_tiles, block_e, n_used, run_first, run_slot, run_next, wgu, bgu, wd, bd):
    n_exp, d, two_ff = wgu.shape
    d_ff = wd.shape[1]
    blk_rows = MOE_BLOCK * d // LANES
    n_blocks = xb_tiles.shape[0] // blk_rows
    per_e = lambda b, be, *_: (be[b], 0, 0)
    any_spec = pl.BlockSpec(memory_space=pl.ANY)
    grid_spec = pltpu.PrefetchScalarGridSpec(
        num_scalar_prefetch=5,
        grid=(n_blocks,),
        in_specs=[pl.BlockSpec((blk_rows, LANES), lambda b, be, nu, *_: (jnp.minimum(b, nu[0] - 1), 0)),
                  any_spec,
                  pl.BlockSpec((None, 1, two_ff), per_e),
                  any_spec,
                  pl.BlockSpec((None, 1, d), per_e)],
        out_specs=pl.BlockSpec((blk_rows, LANES), lambda b, *_: (b, 0)),
        scratch_shapes=[pltpu.VMEM((2, d, two_ff), F32), pltpu.VMEM((2, d_ff, d), F32),
                        pltpu.VMEM((d, two_ff), BF16), pltpu.VMEM((d_ff, d), BF16),
                        pltpu.SemaphoreType.DMA((2, 2))],
    )
    return pl.pallas_call(
        _expert_kernel,
        grid_spec=grid_spec,
        out_shape=jax.ShapeDtypeStruct(xb_tiles.shape, F32),
        compiler_params=pltpu.CompilerParams(dimension_semantics=("arbitrary",), vmem_limit_bytes=VMEM_LIMIT),
        name="moe_experts",
    )(block_e, n_used, run_first, run_slot, run_next, xb_tiles, wgu, bgu.reshape(n_exp, 1, two_ff), wd,
      bd.reshape(n_exp, 1, d))


def _combine_kernel(xmid_ref, gate_ref, dest_ref, dest_next_ref, gfin_ref, yb_ref, outp_ref, outs_ref,
                    ybuf, ot_buf, sem, *, n_tiles_p):
    i = pl.program_id(0)
    n_tiles = pl.num_programs(0)
    tm, d = xmid_ref.shape
    pitch = d // LANES

    def start_tile(dref, slot):
        def start_rows(r, c):
            for k in range(TOP_K):
                src = dref[0, k * tm + r]
                pltpu.make_async_copy(yb_ref.at[pl.ds(pl.multiple_of(src * pitch, pitch), pitch)],
                                      ybuf.at[slot, k, pl.ds(pl.multiple_of(r * pitch, pitch), pitch)],
                                      sem.at[slot]).start(priority=k % 2)
            return c

        lax.fori_loop(0, tm, start_rows, 0, unroll=ROW_DMA_UNROLL)

    @pl.when(i == 0)
    def _():
        start_tile(dest_ref, 0)

    slot = i % 2

    @pl.when(i + 1 < n_tiles)
    def _():
        start_tile(dest_next_ref, 1 - slot)

    pltpu.make_async_copy(ybuf.at[slot], ybuf.at[slot], sem.at[slot]).wait()
    gate = gate_ref[...]
    y = xmid_ref[...]
    for k in range(TOP_K):
        y = y + gate[:, k:k + 1] * _load_row_tiles(ybuf.at[slot, k], tm, d)
    out = _rms(y, gfin_ref[...])

    @pl.when(i < n_tiles_p)
    def _():
        _store_seq_major(outp_ref, ot_buf, out)

    @pl.when(i >= n_tiles_p)
    def _():
        outs_ref[...] = out


def _combine_call(x_mid, gate, dest_tiles, g_final, yb, *, tm, n_p, n_tiles_p):
    n_tok, d = x_mid.shape
    n_tiles = n_tok // tm
    tt = tm // n_p
    dest_spec = lambda off: pl.BlockSpec((None, 1, tm * TOP_K),
                                         lambda i: (jnp.minimum(i + off, n_tiles - 1), 0, 0),
                                         memory_space=pltpu.SMEM)
    return pl.pallas_call(
        functools.partial(_combine_kernel, n_tiles_p=n_tiles_p),
        grid=(n_tiles,),
        in_specs=[pl.BlockSpec((tm, d), lambda i: (i, 0)),
                  pl.BlockSpec((tm, LANES), lambda i: (i, 0)),
                  dest_spec(0), dest_spec(1),
                  pl.BlockSpec((1, d), lambda i: (0, 0)),
                  pl.BlockSpec(memory_space=pl.ANY)],
        out_specs=(pl.BlockSpec((n_p, tt, d), lambda i: (0, jnp.minimum(i, n_tiles_p - 1), 0)),
                   pl.BlockSpec((tm, d), lambda i: (jnp.maximum(i - n_tiles_p, 0), 0))),
        out_shape=(jax.ShapeDtypeStruct((n_p, n_tiles_p * tt, d), F32),
                   jax.ShapeDtypeStruct(((n_tiles - n_tiles_p) * tm, d), F32)),
        scratch_shapes=[pltpu.VMEM((2, TOP_K, tm * d // LANES, LANES), F32),
                        pltpu.VMEM((d // LANES, tm, LANES), F32),
                        pltpu.SemaphoreType.DMA((2,))],
        compiler_params=pltpu.CompilerParams(dimension_semantics=("arbitrary",), vmem_limit_bytes=VMEM_LIMIT),
        name="moe_combine",
    )(x_mid, gate, dest_tiles, dest_tiles, g_final, yb)


def _to_time_major(x, seq_block):
    n_seqs, t, d = x.shape
    n_sb = n_seqs // seq_block
    return x.reshape(n_sb, seq_block, t, d).transpose(0, 2, 1, 3).reshape(n_sb * t, seq_block, d)


def _from_time_major(x, n_seqs, seq_block):
    d = x.shape[-1]
    n_sb = n_seqs // seq_block
    t = x.size // (n_seqs * d)
    return x.reshape(n_sb, t, seq_block, d).transpose(0, 2, 1, 3).reshape(n_seqs, t, d)


def _state_to_chunks(state, seq_block):
    n_seqs, w1, d = state.shape
    return (state.reshape(n_seqs // seq_block, seq_block, w1, d // CONV_LANES, CONV_LANES)
            .transpose(0, 3, 2, 1, 4))


def _state_from_chunks(x):
    n_sb, n_lc, w1, seq_block, cl = x.shape
    return x.transpose(0, 3, 2, 1, 4).reshape(n_sb * seq_block, w1, n_lc * cl)


def _layer(xp, xs, state_a, state_b, p, norm_final_g):
    n_p, t_p, d = xp.shape
    n_s, t_s, _ = xs.shape
    n_exp = p["wgu"].shape[0]
    tm = TOKEN_TILE
    sb = tm // t_s
    n_tiles_p = n_p * t_p // tm

    (x_mid, xn2, route, gate, cnt, newa_p, newb_p, newa_s, newb_s) = _mixer_call(
        xp, _to_time_major(xs, sb), _state_to_chunks(state_a, sb), _state_to_chunks(state_b, sb),
        p, tt_p=tm // n_p, tt_s=t_s)
    n_tok = x_mid.shape[0]
    n_tiles = n_tok // tm

    counts = cnt[0, :n_exp].astype(jnp.int32)
    padded = (counts + MOE_BLOCK - 1) // MOE_BLOCK * MOE_BLOCK
    pad_end = jnp.cumsum(padded)
    pad_start = pad_end - padded
    is_e = route[:TOP_K, :, None] == jnp.arange(n_exp, dtype=jnp.int32)
    dest = jnp.sum(jnp.where(is_e, pad_start, 0), axis=-1) + route[TOP_K:]
    dest_tiles = dest.reshape(TOP_K, n_tiles, tm).transpose(1, 0, 2).reshape(n_tiles, 1, TOP_K * tm)
    n_blocks = -(-(n_tok * TOP_K) // MOE_BLOCK) + n_exp
    n_used = (pad_end[-1] // MOE_BLOCK).astype(jnp.int32)
    blk_start = jnp.minimum(jnp.arange(n_blocks, dtype=jnp.int32) * MOE_BLOCK, pad_end[-1] - 1)
    block_e = jnp.minimum(jnp.sum(blk_start[:, None] >= pad_end[None, :], axis=1), n_exp - 1).astype(jnp.int32)
    last_blocks = jnp.arange(n_blocks - n_exp, n_blocks, dtype=jnp.int32)
    zero_start = jnp.concatenate([jnp.where(padded > 0, pad_end - MOE_BLOCK, -1),
                                  jnp.where(last_blocks >= n_used, last_blocks * MOE_BLOCK, -1)]).astype(jnp.int32)

    xb = _dispatch_call(xn2, dest_tiles, zero_start, n_blocks * MOE_BLOCK, tm=tm, d=d)
    blk_ids = jnp.arange(n_blocks, dtype=jnp.int32)
    prev_e = jnp.concatenate([jnp.full((1,), -1, jnp.int32), block_e[:-1]])
    run_first = ((block_e != prev_e) & (blk_ids < n_used)).astype(jnp.int32)
    run_slot = ((jnp.cumsum(run_first) - 1) % 2).astype(jnp.int32)
    e_ids = jnp.arange(n_exp, dtype=jnp.int32)
    later = lax.cummin(jnp.where(padded > 0, e_ids, n_exp), axis=0, reverse=True)
    next_of = jnp.concatenate([later[1:], jnp.full((1,), n_exp, jnp.int32)])
    next_of = jnp.where(next_of >= n_exp, -1, next_of)
    run_next = jnp.sum(jnp.where(block_e[:, None] == e_ids[None, :], next_of[None, :], 0), axis=1).astype(jnp.int32)

    yb = _expert_call(xb, block_e, n_used.reshape(1), run_first, run_slot, run_next,
                      p["wgu"], p["bgu"], p["wd"], p["bd"])
    y_p, y_s = _combine_call(x_mid, gate, dest_tiles, norm_final_g.reshape(1, d), yb, tm=tm, n_p=n_p,
                             n_tiles_p=n_tiles_p)

    return (y_p, _from_time_major(y_s, n_s, sb),
            _state_from_chunks(newa_p[None]), _state_from_chunks(newb_p[None]),
            _state_from_chunks(newa_s), _state_from_chunks(newb_s))


def _prep_params(l, norm_mix_g, w_in, b_gates, conv_a_w, conv_a_b, w_a_out, conv_b_w, conv_b_b, ln_b_g,
                 ln_b_b, w_b_out, w_o, norm_ffn_g, w_router, b_router, w_gu, b_gu, w_down, b_down):
    d = w_in.shape[1]
    n_exp = w_router.shape[-1]
    n_lc = d // CONV_LANES
    row = lambda v: v.reshape(1, -1)
    taps = lambda w: jnp.broadcast_to(w.reshape(w.shape[0], n_lc, 1, CONV_LANES).transpose(1, 0, 2, 3),
                                      (n_lc, w.shape[0], SUBLANES, CONV_LANES))
    cbias = lambda b: jnp.broadcast_to(b.reshape(n_lc, 1, CONV_LANES), (n_lc, SUBLANES, CONV_LANES))
    col_chunks = lambda w: w.reshape(d, -1, CONV_LANES).transpose(1, 0, 2).astype(BF16)
    win = w_in[l]
    wr = jnp.zeros((d, LANES), F32).at[:, :n_exp].set(w_router[l])
    wr_hi = wr.astype(BF16)
    br = jnp.full((1, LANES), NEG_BIG, F32).at[0, :n_exp].set(b_router[l])
    return dict(
        gmix=row(norm_mix_g[l]), win14=win[:, d:5 * d].astype(BF16),
        wgate=jnp.concatenate([col_chunks(win[:, g * d:(g + 1) * d]) for g in (0, 5, 6)], axis=0),
        bg=b_gates[l],
        caw=taps(conv_a_w[l]), cab=cbias(conv_a_b[l]), waout=col_chunks(w_a_out[l]),
        cbw=taps(conv_b_w[l]), cbb=cbias(conv_b_b[l]), lng=row(ln_b_g[l]), lnb=row(ln_b_b[l]),
        wbout=w_b_out[l].astype(BF16), wo=w_o[l].astype(BF16), gffn=row(norm_ffn_g[l]),
        wrh=wr_hi, wrl=(wr - wr_hi.astype(F32)).astype(BF16), br=br,
        wgu=w_gu[l], bgu=b_gu[l], wd=w_down[l], bd=b_down[l])


def kernel(x_prompt, x_sample, state_conv_a, state_conv_b, norm_mix_g, w_in, b_gates, conv_a_w, conv_a_b, w_a_out, conv_b_w, conv_b_b, ln_b_g, ln_b_b, w_b_out, w_o, norm_ffn_g, w_router, b_router, w_gu, b_gu, w_down, b_down, norm_final_g):
    depth = w_in.shape[0]
    assert depth == 1, "the final norm is fused into the last layer's combine call"
    p = _prep_params(0, norm_mix_g, w_in, b_gates, conv_a_w, conv_a_b, w_a_out, conv_b_w, conv_b_b, ln_b_g,
                     ln_b_b, w_b_out, w_o, norm_ffn_g, w_router, b_router, w_gu, b_gu, w_down, b_down)
    y_p, y_s, na_p, nb_p, na_s, nb_s = _layer(x_prompt, x_sample, state_conv_a[0], state_conv_b[0], p,
                                               norm_final_g)
    return (y_p, y_s, na_p[None], nb_p[None], na_s[None], nb_s[None])
```

```python
import functools

import jax
import jax.numpy as jnp
from jax import lax
from jax.experimental import pallas as pl
from jax.experimental.pallas import tpu as pltpu

EPS = 1e-5
SWIGLU_ALPHA = 1.702
SWIGLU_LIMIT = 7.0
TOP_K = 4
MOE_BLOCK = 256
TOKEN_TILE = 256
LANES = 128
SUBLANES = 8
VMEM_LIMIT = 60 * 1024 * 1024
CONV_OUT_BLOCK = 8
ROW_DMA_UNROLL = 4
CONV_LANES = 256

F32 = jnp.float32
BF16 = jnp.bfloat16


def _sigmoid(v):
    return 1.0 / (1.0 + jnp.exp(-v))


def _store_row_tiles(ref, value):
    n, d = value.shape
    pitch = d // LANES
    for c in range(pitch):
        ref[pl.ds(c, n, stride=pitch), :] = value[:, c * LANES:(c + 1) * LANES]


def _load_row_tiles(ref, n, d):
    pitch = d // LANES
    return jnp.concatenate([ref[pl.ds(c, n, stride=pitch), :] for c in range(pitch)], axis=1)


def _rms(v, g):
    return v * lax.rsqrt(jnp.mean(v * v, axis=-1, keepdims=True) + EPS) * g


def _causal_conv(ext_ref, w_ref, bias, out_ref, n_out):
    width = w_ref.shape[0]
    n_seq, d = ext_ref.shape[1:]
    nb = CONV_OUT_BLOCK
    assert n_out % nb == 0 and n_seq % SUBLANES == 0 and d % CONV_LANES == 0

    def block(tb, carry):
        t0 = tb * nb
        for sg in range(n_seq // SUBLANES):
            rows = pl.ds(sg * SUBLANES, SUBLANES)
            for lc in range(d // CONV_LANES):
                lanes = pl.ds(lc * CONV_LANES, CONV_LANES)
                loaded = {}
                acc = [None] * nb
                for k in range(width):
                    wk = w_ref[k, :, lanes]
                    for j in range(nb):
                        if j + k not in loaded:
                            loaded[j + k] = ext_ref[t0 + j + k, rows, lanes]
                        term = wk * loaded[j + k]
                        acc[j] = term if acc[j] is None else acc[j] + term
                for j, a in enumerate(acc):
                    row0 = pl.multiple_of((t0 + j) * n_seq + sg * SUBLANES, SUBLANES)
                    out_ref[pl.ds(row0, SUBLANES), lanes] = a + bias[:, lc * CONV_LANES:(lc + 1) * CONV_LANES]
        return carry

    lax.fori_loop(0, n_out // nb, block, 0)
    return out_ref[...]


_MIXER_CONSTS = ("gmix", "win", "bg", "caw", "cab", "waout", "cbw", "cbb", "lng", "lnb", "wbout", "wo",
                 "gffn", "wrh", "wrl", "br")


def _time_major_rows(src_ref, buf, t0, tt):
    n_seq, _, d = src_ref.shape
    for s in range(n_seq):
        for c in range(d // LANES):
            buf[c, pl.ds(s, tt, stride=n_seq), :] = src_ref[s, t0:t0 + tt, c * LANES:(c + 1) * LANES]
    return jnp.concatenate([buf[c] for c in range(d // LANES)], axis=1)


def _store_seq_major(dst_ref, buf, value):
    n_seq, tt, d = dst_ref.shape
    for c in range(d // LANES):
        buf[c] = value[:, c * LANES:(c + 1) * LANES]
    for s in range(n_seq):
        for c in range(d // LANES):
            dst_ref[s, :, c * LANES:(c + 1) * LANES] = buf[c, pl.ds(s, tt, stride=n_seq), :]


def _mixer_tile(x, exta, extb, c, xmid_ref, xn2_ref, route_ref, gate_ref, cnt_ref, cnt_acc, conv_buf):
    rows, d = x.shape
    n_seq = exta.shape[1]
    tt = rows // n_seq
    w_a = c["caw"].shape[0]
    w_b = c["cbw"].shape[0]

    xn = _rms(x, c["gmix"][...]).astype(BF16)

    def proj(g):
        return jnp.dot(xn, c["win"][:, g * d:(g + 1) * d], preferred_element_type=F32)

    exta[w_a - 1:w_a - 1 + tt] = (proj(1) * proj(2)).reshape(tt, n_seq, d)
    conv_a = _causal_conv(exta, c["caw"], c["cab"][...], conv_buf.at[0], tt)
    y_a = jnp.dot((proj(0) * conv_a).astype(BF16), c["waout"][...], preferred_element_type=F32)
    extb[w_b - 1:w_b - 1 + tt] = (proj(3) * _sigmoid(proj(4))).reshape(tt, n_seq, d)
    conv_b = _causal_conv(extb, c["cbw"], c["cbb"][...], conv_buf.at[1], tt)
    mu = jnp.mean(conv_b, axis=-1, keepdims=True)
    cen = conv_b - mu
    ln = cen * lax.rsqrt(jnp.mean(cen * cen, axis=-1, keepdims=True) + EPS) * c["lng"][...] + c["lnb"][...]
    y_b = jnp.dot((ln * _sigmoid(ln)).astype(BF16), c["wbout"][...], preferred_element_type=F32)

    bg = c["bg"]
    merged = _sigmoid(proj(5) + bg[0:1, :]) * y_a + _sigmoid(proj(6) + bg[1:2, :]) * y_b
    x_mid = x + jnp.dot(merged.astype(BF16), c["wo"][...], preferred_element_type=F32)
    xmid_ref[...] = x_mid
    xn2 = _rms(x_mid, c["gffn"][...])
    _store_row_tiles(xn2_ref, xn2)

    n_exp = c["wrh"].shape[0]
    nt = (((1,), (1,)), ((), ()))
    x_hi = xn2.astype(BF16)
    x_lo = (xn2 - x_hi.astype(F32)).astype(BF16)
    logits = (lax.dot_general(c["wrh"][...], x_hi, nt, preferred_element_type=F32)
              + lax.dot_general(c["wrh"][...], x_lo, nt, preferred_element_type=F32)
              + lax.dot_general(c["wrl"][...], x_hi, nt, preferred_element_type=F32)) + c["br"][...]
    e_io = lax.broadcasted_iota(jnp.int32, (n_exp, rows), 0)
    work = logits
    top_v, top_i = [], []
    for _ in range(TOP_K):
        m = jnp.max(work, axis=0, keepdims=True)
        idx = jnp.min(jnp.where(work == m, e_io, n_exp), axis=0, keepdims=True)
        top_v.append(m)
        top_i.append(idx)
        work = jnp.where(e_io == idx, -jnp.inf, work)
    ex = [jnp.exp(v - top_v[0]) for v in top_v]
    den = ex[0] + ex[1] + ex[2] + ex[3]
    onehot = jnp.zeros((n_exp, rows), F32)
    for idx in top_i:
        onehot = onehot + (e_io == idx).astype(F32)
    r_io = lax.broadcasted_iota(jnp.int32, (rows, rows), 0)
    c_io = lax.broadcasted_iota(jnp.int32, (rows, rows), 1)
    before = (r_io < c_io).astype(BF16)
    cnt = cnt_acc[...]
    prefix = jnp.dot(onehot.astype(BF16), before, preferred_element_type=F32) + cnt[:, 0:1]
    pos = [jnp.sum(jnp.where(e_io == idx, prefix, 0.0), axis=0, keepdims=True) for idx in top_i]
    route_ref[...] = jnp.concatenate(top_i + [p.astype(jnp.int32) for p in pos], axis=0)
    gates = jnp.concatenate([e / den for e in ex] + [jnp.zeros((LANES - TOP_K, rows), F32)], axis=0)
    gate_ref[...] = jnp.transpose(gates)
    cnt = cnt + jnp.sum(onehot, axis=1, keepdims=True)
    cnt_acc[...] = cnt
    cnt_ref[...] = cnt


def _mixer_kernel(*refs, n_steps_p):
    n_c = len(_MIXER_CONSTS)
    xp_ref, xs_ref, hsa_ref, hsb_ref = refs[:4]
    c = dict(zip(_MIXER_CONSTS, refs[4:4 + n_c]))
    (xmid_ref, xn2_ref, route_ref, gate_ref, cnt_ref,
     newa_p_ref, newb_p_ref, newa_s_ref, newb_s_ref) = refs[4 + n_c:13 + n_c]
    exta_p, extb_p, exta_s, extb_s, cnt_acc, conv_buf, xt_buf, sem = refs[13 + n_c:]
    i = pl.program_id(0)
    tt_p = xp_ref.shape[1]
    tt_s = xs_ref.shape[0]
    w_a = c["caw"].shape[0]
    w_b = c["cbw"].shape[0]
    tile_args = (c, xmid_ref, xn2_ref, route_ref, gate_ref, cnt_ref, cnt_acc, conv_buf)

    @pl.when(i == 0)
    def _():
        exta_p[0:w_a - 1] = jnp.zeros((w_a - 1,) + exta_p.shape[1:], F32)
        extb_p[0:w_b - 1] = jnp.zeros((w_b - 1,) + extb_p.shape[1:], F32)
        cnt_acc[...] = jnp.zeros_like(cnt_acc)

    @pl.when(i < n_steps_p)
    def _():
        _mixer_tile(_time_major_rows(xp_ref, xt_buf, 0, tt_p), exta_p, extb_p, *tile_args)
        exta_p[0:w_a - 1] = exta_p[tt_p:tt_p + w_a - 1]
        extb_p[0:w_b - 1] = extb_p[tt_p:tt_p + w_b - 1]

        @pl.when(i == n_steps_p - 1)
        def _():
            cp_a = pltpu.make_async_copy(exta_p.at[pl.ds(0, w_a - 1)], newa_p_ref, sem.at[0])
            cp_b = pltpu.make_async_copy(extb_p.at[pl.ds(0, w_b - 1)], newb_p_ref, sem.at[1])
            cp_a.start()
            cp_b.start()
            cp_a.wait()
            cp_b.wait()

    @pl.when(i >= n_steps_p)
    def _():
        q = i - n_steps_p
        in_a = pltpu.make_async_copy(hsa_ref.at[pl.ds(q * (w_a - 1), w_a - 1)],
                                     exta_s.at[pl.ds(0, w_a - 1)], sem.at[0])
        in_b = pltpu.make_async_copy(hsb_ref.at[pl.ds(q * (w_b - 1), w_b - 1)],
                                     extb_s.at[pl.ds(0, w_b - 1)], sem.at[1])
        in_a.start()
        in_b.start()
        in_a.wait()
        in_b.wait()
        _mixer_tile(xs_ref[...].reshape(TOKEN_TILE, xs_ref.shape[-1]), exta_s, extb_s, *tile_args)
        out_a = pltpu.make_async_copy(exta_s.at[pl.ds(tt_s, w_a - 1)],
                                      newa_s_ref.at[pl.ds(q * (w_a - 1), w_a - 1)], sem.at[0])
        out_b = pltpu.make_async_copy(extb_s.at[pl.ds(tt_s, w_b - 1)],
                                      newb_s_ref.at[pl.ds(q * (w_b - 1), w_b - 1)], sem.at[1])
        out_a.start()
        out_b.start()
        out_a.wait()
        out_b.wait()


def _mixer_call(xp, xs_tm, hs_a, hs_b, params, *, tt_p, tt_s):
    n_p, t_p, d = xp.shape
    sb = xs_tm.shape[1]
    w_a = params["caw"].shape[0]
    w_b = params["cbw"].shape[0]
    n_exp = params["wrh"].shape[0]
    assert tt_p * n_p == tt_s * sb == TOKEN_TILE and n_exp % SUBLANES == 0
    rows = TOKEN_TILE
    n_steps_p = t_p // tt_p
    n_steps_s = xs_tm.shape[0] // tt_s
    assert n_steps_p * tt_p == t_p and n_steps_s * tt_s == xs_tm.shape[0]
    n_steps = n_steps_p + n_steps_s
    n_tok = n_steps * rows
    pitch = d // LANES
    consts = [params[n] for n in _MIXER_CONSTS]
    const_spec = lambda a: pl.BlockSpec(a.shape, lambda i, _nd=a.ndim: (0,) * _nd, pipeline_mode=pl.Buffered(1))
    any_spec = pl.BlockSpec(memory_space=pl.ANY)
    in_specs = [
        pl.BlockSpec((n_p, tt_p, d), lambda i: (0, jnp.minimum(i, n_steps_p - 1), 0)),
        pl.BlockSpec((tt_s, sb, d), lambda i: (jnp.maximum(i - n_steps_p, 0), 0, 0)),
        any_spec, any_spec,
    ] + [const_spec(a) for a in consts]
    out_shape = (
        jax.ShapeDtypeStruct((n_tok, d), F32),
        jax.ShapeDtypeStruct((n_tok * pitch, LANES), F32),
        jax.ShapeDtypeStruct((2 * TOP_K, n_tok), jnp.int32),
        jax.ShapeDtypeStruct((n_tok, LANES), F32),
        jax.ShapeDtypeStruct((n_exp, LANES), F32),
        jax.ShapeDtypeStruct((w_a - 1, n_p, d), F32),
        jax.ShapeDtypeStruct((w_b - 1, n_p, d), F32),
        jax.ShapeDtypeStruct(hs_a.shape, F32),
        jax.ShapeDtypeStruct(hs_b.shape, F32),
    )
    out_specs = (
        pl.BlockSpec((rows, d), lambda i: (i, 0)),
        pl.BlockSpec((rows * pitch, LANES), lambda i: (i, 0)),
        pl.BlockSpec((2 * TOP_K, rows), lambda i: (0, i)),
        pl.BlockSpec((rows, LANES), lambda i: (i, 0)),
        pl.BlockSpec((n_exp, LANES), lambda i: (0, 0)),
        any_spec, any_spec, any_spec, any_spec,
    )
    return pl.pallas_call(
        functools.partial(_mixer_kernel, n_steps_p=n_steps_p),
        grid=(n_steps,),
        in_specs=in_specs,
        out_specs=out_specs,
        out_shape=out_shape,
        scratch_shapes=[pltpu.VMEM((tt_p + w_a - 1, n_p, d), F32),
                        pltpu.VMEM((tt_p + w_b - 1, n_p, d), F32),
                        pltpu.VMEM((tt_s + w_a - 1, sb, d), F32),
                        pltpu.VMEM((tt_s + w_b - 1, sb, d), F32),
                        pltpu.VMEM((n_exp, LANES), F32),
                        pltpu.VMEM((2, rows, d), F32),
                        pltpu.VMEM((pitch, rows, LANES), F32),
                        pltpu.SemaphoreType.DMA((2,))],
        compiler_params=pltpu.CompilerParams(dimension_semantics=("arbitrary",), vmem_limit_bytes=VMEM_LIMIT),
        name="mixer_router",
    )(xp, xs_tm, hs_a, hs_b, *consts)


def _dispatch_kernel(zero_ref, x_ref, dest_ref, xb_ref, zero_buf, sem, zsem, *, pitch):
    tm = x_ref.shape[0] // pitch
    n_zero = zero_ref.shape[0]
    blk_rows = zero_buf.shape[0]

    @pl.when(pl.program_id(0) == 0)
    def _():
        zero_buf[...] = jnp.zeros_like(zero_buf)

        def zcopy(e):
            start = pl.multiple_of(jnp.maximum(zero_ref[e], 0) * pitch, blk_rows)
            return pltpu.make_async_copy(zero_buf, xb_ref.at[pl.ds(start, blk_rows)], zsem)

        def start(e, c):
            @pl.when(zero_ref[e] >= 0)
            def _():
                zcopy(e).start()
            return c

        def wait(e, c):
            @pl.when(zero_ref[e] >= 0)
            def _():
                zcopy(e).wait()
            return c

        lax.fori_loop(0, n_zero, start, 0)
        lax.fori_loop(0, n_zero, wait, 0)

    def row_copy(r, dst):
        src = x_ref.at[pl.ds(pl.multiple_of(r * pitch, pitch), pitch)]
        return pltpu.make_async_copy(src, xb_ref.at[pl.ds(pl.multiple_of(dst * pitch, pitch), pitch)], sem)

    def start_rows(r, c):
        for k in range(TOP_K):
            row_copy(r, dest_ref[0, k * tm + r]).start(priority=k % 2)
        return c

    lax.fori_loop(0, tm, start_rows, 0, unroll=ROW_DMA_UNROLL)
    n_all = tm * TOP_K * pitch
    pltpu.make_async_copy(xb_ref.at[pl.ds(0, n_all)], xb_ref.at[pl.ds(0, n_all)], sem).wait()


def _dispatch_call(xn2_tiles, dest_tiles, zero_start, n_rows, *, tm, d):
    pitch = d // LANES
    n_tiles = xn2_tiles.shape[0] // (tm * pitch)
    grid_spec = pltpu.PrefetchScalarGridSpec(
        num_scalar_prefetch=1,
        grid=(n_tiles,),
        in_specs=[pl.BlockSpec((tm * pitch, LANES), lambda i, z: (i, 0)),
                  pl.BlockSpec((None, 1, tm * TOP_K), lambda i, z: (i, 0, 0), memory_space=pltpu.SMEM)],
        out_specs=pl.BlockSpec(memory_space=pl.ANY),
        scratch_shapes=[pltpu.VMEM((MOE_BLOCK * pitch, LANES), F32), pltpu.SemaphoreType.DMA,
                        pltpu.SemaphoreType.DMA],
    )
    return pl.pallas_call(
        functools.partial(_dispatch_kernel, pitch=pitch),
        grid_spec=grid_spec,
        out_shape=jax.ShapeDtypeStruct((n_rows * pitch, LANES), F32),
        compiler_params=pltpu.CompilerParams(dimension_semantics=("arbitrary",)),
        name="moe_dispatch",
    )(zero_start, xn2_tiles, dest_tiles)


def _expert_kernel(be_ref, nu_ref, first_ref, slot_ref, next_ref, x_ref, wgu_hbm, bgu_ref, wd_hbm, bd_ref, y_ref,
                   wgu_st, wd_st, wgu_bf, wd_bf, sem):
    b = pl.program_id(0)

    def fetch(e, slot):
        return (pltpu.make_async_copy(wgu_hbm.at[e], wgu_st.at[slot], sem.at[slot, 0]),
                pltpu.make_async_copy(wd_hbm.at[e], wd_st.at[slot], sem.at[slot, 1]))

    @pl.when(b < nu_ref[0])
    def _():
        d_ff, d = wd_bf.shape

        @pl.when(first_ref[b] == 1)
        def _():
            slot = slot_ref[b]

            @pl.when(b == 0)
            def _():
                for cp in fetch(be_ref[b], slot):
                    cp.start()

            for cp in fetch(be_ref[b], slot):
                cp.wait()

            @pl.when(next_ref[b] >= 0)
            def _():
                for cp in fetch(next_ref[b], 1 - slot):
                    cp.start()

            wgu_bf[...] = wgu_st[slot].astype(BF16)
            wd_bf[...] = wd_st[slot].astype(BF16)

        x = _load_row_tiles(x_ref, MOE_BLOCK, d)
        h = jnp.dot(x.astype(BF16), wgu_bf[...], preferred_element_type=F32) + bgu_ref[...]
        g = jnp.minimum(h[:, :d_ff], SWIGLU_LIMIT)
        u = jnp.clip(h[:, d_ff:], -SWIGLU_LIMIT, SWIGLU_LIMIT)
        act = (u + 1.0) * (g * _sigmoid(SWIGLU_ALPHA * g))
        y = jnp.dot(act.astype(BF16), wd_bf[...], preferred_element_type=F32) + bd_ref[...]
        _store_row_tiles(y_ref, y)

    @pl.when(b >= nu_ref[0])
    def _():
        y_ref[...] = jnp.zeros_like(y_ref)


def _expert_call(xb_tiles, block_e, n_used, run_first, run_slot, run_next, wgu, bgu, wd, bd):
    n_exp, d, two_ff = wgu.shape
    d_ff = wd.shape[1]
    blk_rows = MOE_BLOCK * d // LANES
    n_blocks = xb_tiles.shape[0] // blk_rows
    per_e = lambda b, be, *_: (be[b], 0, 0)
    any_spec = pl.BlockSpec(memory_space=pl.ANY)
    grid_spec = pltpu.PrefetchScalarGridSpec(
        num_scalar_prefetch=5,
        grid=(n_blocks,),
        in_specs=[pl.BlockSpec((blk_rows, LANES), lambda b, be, nu, *_: (jnp.minimum(b, nu[0] - 1), 0)),
                  any_spec,
                  pl.BlockSpec((None, 1, two_ff), per_e),
                  any_spec,
                  pl.BlockSpec((None, 1, d), per_e)],
        out_specs=pl.BlockSpec((blk_rows, LANES), lambda b, *_: (b, 0)),
        scratch_shapes=[pltpu.VMEM((2, d, two_ff), F32), pltpu.VMEM((2, d_ff, d), F32),
                        pltpu.VMEM((d, two_ff), BF16), pltpu.VMEM((d_ff, d), BF16),
                        pltpu.SemaphoreType.DMA((2, 2))],
    )
    return pl.pallas_call(
        _expert_kernel,
        grid_spec=grid_spec,
        out_shape=jax.ShapeDtypeStruct(xb_tiles.shape, F32),
        compiler_params=pltpu.CompilerParams(dimension_semantics=("arbitrary",), vmem_limit_bytes=VMEM_LIMIT),
        name="moe_experts",
    )(block_e, n_used, run_first, run_slot, run_next, xb_tiles, wgu, bgu.reshape(n_exp, 1, two_ff), wd,
      bd.reshape(n_exp, 1, d))


def _combine_kernel(xmid_ref, gate_ref, dest_ref, dest_next_ref, gfin_ref, yb_ref, outp_ref, outs_ref,
                    ybuf, ot_buf, sem, *, n_tiles_p):
    i = pl.program_id(0)
    n_tiles = pl.num_programs(0)
    tm, d = xmid_ref.shape
    pitch = d // LANES

    def start_tile(dref, slot):
        def start_rows(r, c):
            for k in range(TOP_K):
                src = dref[0, k * tm + r]
                pltpu.make_async_copy(yb_ref.at[pl.ds(pl.multiple_of(src * pitch, pitch), pitch)],
                                      ybuf.at[slot, k, pl.ds(pl.multiple_of(r * pitch, pitch), pitch)],
                                      sem.at[slot]).start(priority=k % 2)
            return c

        lax.fori_loop(0, tm, start_rows, 0, unroll=ROW_DMA_UNROLL)

    @pl.when(i == 0)
    def _():
        start_tile(dest_ref, 0)

    slot = i % 2

    @pl.when(i + 1 < n_tiles)
    def _():
        start_tile(dest_next_ref, 1 - slot)

    pltpu.make_async_copy(ybuf.at[slot], ybuf.at[slot], sem.at[slot]).wait()
    gate = gate_ref[...]
    y = xmid_ref[...]
    for k in range(TOP_K):
        y = y + gate[:, k:k + 1] * _load_row_tiles(ybuf.at[slot, k], tm, d)
    out = _rms(y, gfin_ref[...])

    @pl.when(i < n_tiles_p)
    def _():
        _store_seq_major(outp_ref, ot_buf, out)

    @pl.when(i >= n_tiles_p)
    def _():
        outs_ref[...] = out


def _combine_call(x_mid, gate, dest_tiles, g_final, yb, *, tm, n_p, n_tiles_p):
    n_tok, d = x_mid.shape
    n_tiles = n_tok // tm
    tt = tm // n_p
    dest_spec = lambda off: pl.BlockSpec((None, 1, tm * TOP_K),
                                         lambda i: (jnp.minimum(i + off, n_tiles - 1), 0, 0),
                                         memory_space=pltpu.SMEM)
    return pl.pallas_call(
        functools.partial(_combine_kernel, n_tiles_p=n_tiles_p),
        grid=(n_tiles,),
        in_specs=[pl.BlockSpec((tm, d), lambda i: (i, 0)),
                  pl.BlockSpec((tm, LANES), lambda i: (i, 0)),
                  dest_spec(0), dest_spec(1),
                  pl.BlockSpec((1, d), lambda i: (0, 0)),
                  pl.BlockSpec(memory_space=pl.ANY)],
        out_specs=(pl.BlockSpec((n_p, tt, d), lambda i: (0, jnp.minimum(i, n_tiles_p - 1), 0)),
                   pl.BlockSpec((tm, d), lambda i: (jnp.maximum(i - n_tiles_p, 0), 0))),
        out_shape=(jax.ShapeDtypeStruct((n_p, n_tiles_p * tt, d), F32),
                   jax.ShapeDtypeStruct(((n_tiles - n_tiles_p) * tm, d), F32)),
        scratch_shapes=[pltpu.VMEM((2, TOP_K, tm * d // LANES, LANES), F32),
                        pltpu.VMEM((d // LANES, tm, LANES), F32),
                        pltpu.SemaphoreType.DMA((2,))],
        compiler_params=pltpu.CompilerParams(dimension_semantics=("arbitrary",), vmem_limit_bytes=VMEM_LIMIT),
        name="moe_combine",
    )(x_mid, gate, dest_tiles, dest_tiles, g_final, yb)


def _to_time_major(x, seq_block):
    n_seqs, t, d = x.shape
    n_sb = n_seqs // seq_block
    return x.reshape(n_sb, seq_block, t, d).transpose(0, 2, 1, 3).reshape(n_sb * t, seq_block, d)


def _from_time_major(x, n_seqs, seq_block):
    d = x.shape[-1]
    n_sb = n_seqs // seq_block
    t = x.size // (n_seqs * d)
    return x.reshape(n_sb, t, seq_block, d).transpose(0, 2, 1, 3).reshape(n_seqs, t, d)


def _layer(xp, xs, state_a, state_b, p, norm_final_g):
    n_p, t_p, d = xp.shape
    n_s, t_s, _ = xs.shape
    n_exp = p["wgu"].shape[0]
    tm = TOKEN_TILE
    sb = tm // t_s
    n_tiles_p = n_p * t_p // tm

    (x_mid, xn2, route, gate, cnt, newa_p, newb_p, newa_s, newb_s) = _mixer_call(
        xp, _to_time_major(xs, sb), _to_time_major(state_a, sb), _to_time_major(state_b, sb),
        p, tt_p=tm // n_p, tt_s=t_s)
    n_tok = x_mid.shape[0]
    n_tiles = n_tok // tm

    counts = cnt[:, 0].astype(jnp.int32)
    padded = (counts + MOE_BLOCK - 1) // MOE_BLOCK * MOE_BLOCK
    pad_end = jnp.cumsum(padded)
    pad_start = pad_end - padded
    is_e = route[:TOP_K, :, None] == jnp.arange(n_exp, dtype=jnp.int32)
    dest = jnp.sum(jnp.where(is_e, pad_start, 0), axis=-1) + route[TOP_K:]
    dest_tiles = dest.reshape(TOP_K, n_tiles, tm).transpose(1, 0, 2).reshape(n_tiles, 1, TOP_K * tm)
    n_blocks = -(-(n_tok * TOP_K) // MOE_BLOCK) + n_exp
    n_used = (pad_end[-1] // MOE_BLOCK).astype(jnp.int32)
    blk_start = jnp.minimum(jnp.arange(n_blocks, dtype=jnp.int32) * MOE_BLOCK, pad_end[-1] - 1)
    block_e = jnp.minimum(jnp.sum(blk_start[:, None] >= pad_end[None, :], axis=1), n_exp - 1).astype(jnp.int32)
    last_blocks = jnp.arange(n_blocks - n_exp, n_blocks, dtype=jnp.int32)
    zero_start = jnp.concatenate([jnp.where(padded > 0, pad_end - MOE_BLOCK, -1),
                                  jnp.where(last_blocks >= n_used, last_blocks * MOE_BLOCK, -1)]).astype(jnp.int32)

    xb = _dispatch_call(xn2, dest_tiles, zero_start, n_blocks * MOE_BLOCK, tm=tm, d=d)
    blk_ids = jnp.arange(n_blocks, dtype=jnp.int32)
    prev_e = jnp.concatenate([jnp.full((1,), -1, jnp.int32), block_e[:-1]])
    run_first = ((block_e != prev_e) & (blk_ids < n_used)).astype(jnp.int32)
    run_slot = ((jnp.cumsum(run_first) - 1) % 2).astype(jnp.int32)
    e_ids = jnp.arange(n_exp, dtype=jnp.int32)
    later = lax.cummin(jnp.where(padded > 0, e_ids, n_exp), axis=0, reverse=True)
    next_of = jnp.concatenate([later[1:], jnp.full((1,), n_exp, jnp.int32)])
    next_of = jnp.where(next_of >= n_exp, -1, next_of)
    run_next = jnp.sum(jnp.where(block_e[:, None] == e_ids[None, :], next_of[None, :], 0), axis=1).astype(jnp.int32)

    yb = _expert_call(xb, block_e, n_used.reshape(1), run_first, run_slot, run_next,
                      p["wgu"], p["bgu"], p["wd"], p["bd"])
    y_p, y_s = _combine_call(x_mid, gate, dest_tiles, norm_final_g.reshape(1, d), yb, tm=tm, n_p=n_p,
                             n_tiles_p=n_tiles_p)

    return (y_p, _from_time_major(y_s, n_s, sb),
            _from_time_major(newa_p, n_p, n_p), _from_time_major(newb_p, n_p, n_p),
            _from_time_major(newa_s, n_s, sb), _from_time_major(newb_s, n_s, sb))


def _prep_params(l, norm_mix_g, w_in, b_gates, conv_a_w, conv_a_b, w_a_out, conv_b_w, conv_b_b, ln_b_g,
                 ln_b_b, w_b_out, w_o, norm_ffn_g, w_router, b_router, w_gu, b_gu, w_down, b_down):
    row = lambda v: v.reshape(1, -1)
    taps = lambda w: jnp.broadcast_to(w[:, None, :], (w.shape[0], SUBLANES, w.shape[1]))
    wr_t = w_router[l].T
    wr_hi = wr_t.astype(BF16)
    return dict(
        gmix=row(norm_mix_g[l]), win=w_in[l].astype(BF16), bg=b_gates[l],
        caw=taps(conv_a_w[l]), cab=row(conv_a_b[l]), waout=w_a_out[l].astype(BF16),
        cbw=taps(conv_b_w[l]), cbb=row(conv_b_b[l]), lng=row(ln_b_g[l]), lnb=row(ln_b_b[l]),
        wbout=w_b_out[l].astype(BF16), wo=w_o[l].astype(BF16), gffn=row(norm_ffn_g[l]),
        wrh=wr_hi, wrl=(wr_t - wr_hi.astype(F32)).astype(BF16), br=b_router[l].reshape(-1, 1),
        wgu=w_gu[l], bgu=b_gu[l], wd=w_down[l], bd=b_down[l])


def kernel(x_prompt, x_sample, state_conv_a, state_conv_b, norm_mix_g, w_in, b_gates, conv_a_w, conv_a_b, w_a_out, conv_b_w, conv_b_b, ln_b_g, ln_b_b, w_b_out, w_o, norm_ffn_g, w_router, b_router, w_gu, b_gu, w_down, b_down, norm_final_g):
    depth = w_in.shape[0]
    assert depth == 1, "the final norm is fused into the last layer's combine call"
    p = _prep_params(0, norm_mix_g, w_in, b_gates, conv_a_w, conv_a_b, w_a_out, conv_b_w, conv_b_b, ln_b_g,
                     ln_b_b, w_b_out, w_o, norm_ffn_g, w_router, b_router, w_gu, b_gu, w_down, b_down)
    y_p, y_s, na_p, nb_p, na_s, nb_s = _layer(x_prompt, x_sample, state_conv_a[0], state_conv_b[0], p,
                                               norm_final_g)
    return (y_p, y_s, na_p[None], nb_p[None], na_s[None], nb_s[None])
```

```python
import functools

import jax
import jax.numpy as jnp
from jax import lax
from jax.experimental import pallas as pl
from jax.experimental.pallas import tpu as pltpu

EPS = 1e-5
SWIGLU_ALPHA = 1.702
SWIGLU_LIMIT = 7.0
TOP_K = 4
MOE_BLOCK = 256
TOKEN_TILE = 256
LANES = 128
SUBLANES = 8
VMEM_LIMIT = 60 * 1024 * 1024
CONV_OUT_BLOCK = 8
ROW_DMA_UNROLL = 8
CONV_LANES = 256

F32 = jnp.float32
BF16 = jnp.bfloat16


def _sigmoid(v):
    return 1.0 / (1.0 + jnp.exp(-v))


def _store_row_tiles(ref, value):
    n, d = value.shape
    pitch = d // LANES
    for c in range(pitch):
        ref[pl.ds(c, n, stride=pitch), :] = value[:, c * LANES:(c + 1) * LANES]


def _load_row_tiles(ref, n, d):
    pitch = d // LANES
    return jnp.concatenate([ref[pl.ds(c, n, stride=pitch), :] for c in range(pitch)], axis=1)


def _rms(v, g):
    return v * lax.rsqrt(jnp.mean(v * v, axis=-1, keepdims=True) + EPS) * g


def _causal_conv(ext_ref, w_ref, bias, out_ref, n_out):
    width = w_ref.shape[0]
    n_seq, d = ext_ref.shape[1:]
    nb = CONV_OUT_BLOCK
    assert n_out % nb == 0 and n_seq % SUBLANES == 0 and d % CONV_LANES == 0

    def block(tb, carry):
        t0 = tb * nb
        for sg in range(n_seq // SUBLANES):
            rows = pl.ds(sg * SUBLANES, SUBLANES)
            for lc in range(d // CONV_LANES):
                lanes = pl.ds(lc * CONV_LANES, CONV_LANES)
                loaded = {}
                acc = [None] * nb
                for k in range(width):
                    wk = w_ref[k, :, lanes]
                    for j in range(nb):
                        if j + k not in loaded:
                            loaded[j + k] = ext_ref[t0 + j + k, rows, lanes]
                        term = wk * loaded[j + k]
                        acc[j] = term if acc[j] is None else acc[j] + term
                for j, a in enumerate(acc):
                    row0 = pl.multiple_of((t0 + j) * n_seq + sg * SUBLANES, SUBLANES)
                    out_ref[pl.ds(row0, SUBLANES), lanes] = a + bias[:, lc * CONV_LANES:(lc + 1) * CONV_LANES]
        return carry

    lax.fori_loop(0, n_out // nb, block, 0)
    return out_ref[...]


_MIXER_CONSTS = ("gmix", "win", "bg", "caw", "cab", "waout", "cbw", "cbb", "lng", "lnb", "wbout", "wo",
                 "gffn", "wrh", "wrl", "br")


def _time_major_rows(src_ref, buf, t0, tt):
    n_seq, _, d = src_ref.shape
    for s in range(n_seq):
        for c in range(d // LANES):
            buf[c, pl.ds(s, tt, stride=n_seq), :] = src_ref[s, t0:t0 + tt, c * LANES:(c + 1) * LANES]
    return jnp.concatenate([buf[c] for c in range(d // LANES)], axis=1)


def _store_seq_major(dst_ref, buf, value):
    n_seq, tt, d = dst_ref.shape
    for c in range(d // LANES):
        buf[c] = value[:, c * LANES:(c + 1) * LANES]
    for s in range(n_seq):
        for c in range(d // LANES):
            dst_ref[s, :, c * LANES:(c + 1) * LANES] = buf[c, pl.ds(s, tt, stride=n_seq), :]


def _mixer_tile(x, exta, extb, c, xmid_ref, xn2_ref, route_ref, gate_ref, cnt_ref, cnt_acc, conv_buf):
    rows, d = x.shape
    n_seq = exta.shape[1]
    tt = rows // n_seq
    w_a = c["caw"].shape[0]
    w_b = c["cbw"].shape[0]

    xn = _rms(x, c["gmix"][...]).astype(BF16)

    def proj(g):
        return jnp.dot(xn, c["win"][:, g * d:(g + 1) * d], preferred_element_type=F32)

    exta[w_a - 1:w_a - 1 + tt] = (proj(1) * proj(2)).reshape(tt, n_seq, d)
    conv_a = _causal_conv(exta, c["caw"], c["cab"][...], conv_buf.at[0], tt)
    y_a = jnp.dot((proj(0) * conv_a).astype(BF16), c["waout"][...], preferred_element_type=F32)
    extb[w_b - 1:w_b - 1 + tt] = (proj(3) * _sigmoid(proj(4))).reshape(tt, n_seq, d)
    conv_b = _causal_conv(extb, c["cbw"], c["cbb"][...], conv_buf.at[1], tt)
    mu = jnp.mean(conv_b, axis=-1, keepdims=True)
    cen = conv_b - mu
    ln = cen * lax.rsqrt(jnp.mean(cen * cen, axis=-1, keepdims=True) + EPS) * c["lng"][...] + c["lnb"][...]
    y_b = jnp.dot((ln * _sigmoid(ln)).astype(BF16), c["wbout"][...], preferred_element_type=F32)

    bg = c["bg"]
    merged = _sigmoid(proj(5) + bg[0:1, :]) * y_a + _sigmoid(proj(6) + bg[1:2, :]) * y_b
    x_mid = x + jnp.dot(merged.astype(BF16), c["wo"][...], preferred_element_type=F32)
    xmid_ref[...] = x_mid
    xn2 = _rms(x_mid, c["gffn"][...])
    _store_row_tiles(xn2_ref, xn2)

    n_exp = c["wrh"].shape[0]
    nt = (((1,), (1,)), ((), ()))
    x_hi = xn2.astype(BF16)
    x_lo = (xn2 - x_hi.astype(F32)).astype(BF16)
    logits = (lax.dot_general(c["wrh"][...], x_hi, nt, preferred_element_type=F32)
              + lax.dot_general(c["wrh"][...], x_lo, nt, preferred_element_type=F32)
              + lax.dot_general(c["wrl"][...], x_hi, nt, preferred_element_type=F32)) + c["br"][...]
    e_io = lax.broadcasted_iota(jnp.int32, (n_exp, rows), 0)
    work = logits
    top_v, top_i = [], []
    for _ in range(TOP_K):
        m = jnp.max(work, axis=0, keepdims=True)
        idx = jnp.min(jnp.where(work == m, e_io, n_exp), axis=0, keepdims=True)
        top_v.append(m)
        top_i.append(idx)
        work = jnp.where(e_io == idx, -jnp.inf, work)
    ex = [jnp.exp(v - top_v[0]) for v in top_v]
    den = ex[0] + ex[1] + ex[2] + ex[3]
    onehot = jnp.zeros((n_exp, rows), F32)
    for idx in top_i:
        onehot = onehot + (e_io == idx).astype(F32)
    r_io = lax.broadcasted_iota(jnp.int32, (rows, rows), 0)
    c_io = lax.broadcasted_iota(jnp.int32, (rows, rows), 1)
    before = (r_io < c_io).astype(BF16)
    cnt = cnt_acc[...]
    prefix = jnp.dot(onehot.astype(BF16), before, preferred_element_type=F32) + cnt[:, 0:1]
    pos = [jnp.sum(jnp.where(e_io == idx, prefix, 0.0), axis=0, keepdims=True) for idx in top_i]
    route_ref[...] = jnp.concatenate(top_i + [p.astype(jnp.int32) for p in pos], axis=0)
    gates = jnp.concatenate([e / den for e in ex] + [jnp.zeros((LANES - TOP_K, rows), F32)], axis=0)
    gate_ref[...] = jnp.transpose(gates)
    cnt = cnt + jnp.sum(onehot, axis=1, keepdims=True)
    cnt_acc[...] = cnt
    cnt_ref[...] = cnt


def _mixer_kernel(*refs, n_steps_p):
    n_c = len(_MIXER_CONSTS)
    xp_ref, xs_ref, hsa_ref, hsb_ref = refs[:4]
    c = dict(zip(_MIXER_CONSTS, refs[4:4 + n_c]))
    (xmid_ref, xn2_ref, route_ref, gate_ref, cnt_ref,
     newa_p_ref, newb_p_ref, newa_s_ref, newb_s_ref) = refs[4 + n_c:13 + n_c]
    exta_p, extb_p, exta_s, extb_s, cnt_acc, conv_buf, xt_buf, sem = refs[13 + n_c:]
    i = pl.program_id(0)
    tt_p = xp_ref.shape[1]
    tt_s = xs_ref.shape[0]
    w_a = c["caw"].shape[0]
    w_b = c["cbw"].shape[0]
    tile_args = (c, xmid_ref, xn2_ref, route_ref, gate_ref, cnt_ref, cnt_acc, conv_buf)

    @pl.when(i == 0)
    def _():
        exta_p[0:w_a - 1] = jnp.zeros((w_a - 1,) + exta_p.shape[1:], F32)
        extb_p[0:w_b - 1] = jnp.zeros((w_b - 1,) + extb_p.shape[1:], F32)
        cnt_acc[...] = jnp.zeros_like(cnt_acc)

    @pl.when(i < n_steps_p)
    def _():
        _mixer_tile(_time_major_rows(xp_ref, xt_buf, 0, tt_p), exta_p, extb_p, *tile_args)
        exta_p[0:w_a - 1] = exta_p[tt_p:tt_p + w_a - 1]
        extb_p[0:w_b - 1] = extb_p[tt_p:tt_p + w_b - 1]

        @pl.when(i == n_steps_p - 1)
        def _():
            cp_a = pltpu.make_async_copy(exta_p.at[pl.ds(0, w_a - 1)], newa_p_ref, sem.at[0])
            cp_b = pltpu.make_async_copy(extb_p.at[pl.ds(0, w_b - 1)], newb_p_ref, sem.at[1])
            cp_a.start()
            cp_b.start()
            cp_a.wait()
            cp_b.wait()

    @pl.when(i >= n_steps_p)
    def _():
        q = i - n_steps_p
        in_a = pltpu.make_async_copy(hsa_ref.at[pl.ds(q * (w_a - 1), w_a - 1)],
                                     exta_s.at[pl.ds(0, w_a - 1)], sem.at[0])
        in_b = pltpu.make_async_copy(hsb_ref.at[pl.ds(q * (w_b - 1), w_b - 1)],
                                     extb_s.at[pl.ds(0, w_b - 1)], sem.at[1])
        in_a.start()
        in_b.start()
        in_a.wait()
        in_b.wait()
        _mixer_tile(xs_ref[...].reshape(TOKEN_TILE, xs_ref.shape[-1]), exta_s, extb_s, *tile_args)
        out_a = pltpu.make_async_copy(exta_s.at[pl.ds(tt_s, w_a - 1)],
                                      newa_s_ref.at[pl.ds(q * (w_a - 1), w_a - 1)], sem.at[0])
        out_b = pltpu.make_async_copy(extb_s.at[pl.ds(tt_s, w_b - 1)],
                                      newb_s_ref.at[pl.ds(q * (w_b - 1), w_b - 1)], sem.at[1])
        out_a.start()
        out_b.start()
        out_a.wait()
        out_b.wait()


def _mixer_call(xp, xs_tm, hs_a, hs_b, params, *, tt_p, tt_s):
    n_p, t_p, d = xp.shape
    sb = xs_tm.shape[1]
    w_a = params["caw"].shape[0]
    w_b = params["cbw"].shape[0]
    n_exp = params["wrh"].shape[0]
    assert tt_p * n_p == tt_s * sb == TOKEN_TILE and n_exp % SUBLANES == 0
    rows = TOKEN_TILE
    n_steps_p = t_p // tt_p
    n_steps_s = xs_tm.shape[0] // tt_s
    assert n_steps_p * tt_p == t_p and n_steps_s * tt_s == xs_tm.shape[0]
    n_steps = n_steps_p + n_steps_s
    n_tok = n_steps * rows
    pitch = d // LANES
    consts = [params[n] for n in _MIXER_CONSTS]
    const_spec = lambda a: pl.BlockSpec(a.shape, lambda i, _nd=a.ndim: (0,) * _nd, pipeline_mode=pl.Buffered(1))
    any_spec = pl.BlockSpec(memory_space=pl.ANY)
    in_specs = [
        pl.BlockSpec((n_p, tt_p, d), lambda i: (0, jnp.minimum(i, n_steps_p - 1), 0)),
        pl.BlockSpec((tt_s, sb, d), lambda i: (jnp.maximum(i - n_steps_p, 0), 0, 0)),
        any_spec, any_spec,
    ] + [const_spec(a) for a in consts]
    out_shape = (
        jax.ShapeDtypeStruct((n_tok, d), F32),
        jax.ShapeDtypeStruct((n_tok * pitch, LANES), F32),
        jax.ShapeDtypeStruct((2 * TOP_K, n_tok), jnp.int32),
        jax.ShapeDtypeStruct((n_tok, LANES), F32),
        jax.ShapeDtypeStruct((n_exp, LANES), F32),
        jax.ShapeDtypeStruct((w_a - 1, n_p, d), F32),
        jax.ShapeDtypeStruct((w_b - 1, n_p, d), F32),
        jax.ShapeDtypeStruct(hs_a.shape, F32),
        jax.ShapeDtypeStruct(hs_b.shape, F32),
    )
    out_specs = (
        pl.BlockSpec((rows, d), lambda i: (i, 0)),
        pl.BlockSpec((rows * pitch, LANES), lambda i: (i, 0)),
        pl.BlockSpec((2 * TOP_K, rows), lambda i: (0, i)),
        pl.BlockSpec((rows, LANES), lambda i: (i, 0)),
        pl.BlockSpec((n_exp, LANES), lambda i: (0, 0)),
        any_spec, any_spec, any_spec, any_spec,
    )
    return pl.pallas_call(
        functools.partial(_mixer_kernel, n_steps_p=n_steps_p),
        grid=(n_steps,),
        in_specs=in_specs,
        out_specs=out_specs,
        out_shape=out_shape,
        scratch_shapes=[pltpu.VMEM((tt_p + w_a - 1, n_p, d), F32),
                        pltpu.VMEM((tt_p + w_b - 1, n_p, d), F32),
                        pltpu.VMEM((tt_s + w_a - 1, sb, d), F32),
                        pltpu.VMEM((tt_s + w_b - 1, sb, d), F32),
                        pltpu.VMEM((n_exp, LANES), F32),
                        pltpu.VMEM((2, rows, d), F32),
                        pltpu.VMEM((pitch, rows, LANES), F32),
                        pltpu.SemaphoreType.DMA((2,))],
        compiler_params=pltpu.CompilerParams(dimension_semantics=("arbitrary",), vmem_limit_bytes=VMEM_LIMIT),
        name="mixer_router",
    )(xp, xs_tm, hs_a, hs_b, *consts)


def _dispatch_kernel(zero_ref, x_hbm, dest_ref, xb_ref, xbuf, zero_buf, in_sem, out_sem, zsem, *, pitch):
    i = pl.program_id(0)
    n_tiles = pl.num_programs(0)
    n_buf, tile_rows = xbuf.shape[:2]
    tm = tile_rows // pitch
    n_zero = zero_ref.shape[0]
    blk_rows = zero_buf.shape[0]
    n_all = tm * TOP_K * pitch

    def fetch(t):
        start = t * tile_rows if isinstance(t, int) else pl.multiple_of(t * tile_rows, tile_rows)
        return pltpu.make_async_copy(x_hbm.at[pl.ds(start, tile_rows)], xbuf.at[t % n_buf], in_sem.at[t % n_buf])

    def wait_scatters(t):
        pltpu.make_async_copy(xb_ref.at[pl.ds(0, n_all)], xb_ref.at[pl.ds(0, n_all)], out_sem.at[t % 2]).wait()

    @pl.when(i == 0)
    def _():
        fetch(0).start()
        zero_buf[...] = jnp.zeros_like(zero_buf)

        def zcopy(e):
            start = pl.multiple_of(jnp.maximum(zero_ref[e], 0) * pitch, blk_rows)
            return pltpu.make_async_copy(zero_buf, xb_ref.at[pl.ds(start, blk_rows)], zsem)

        def start(e, c):
            @pl.when(zero_ref[e] >= 0)
            def _():
                zcopy(e).start()
            return c

        def wait(e, c):
            @pl.when(zero_ref[e] >= 0)
            def _():
                zcopy(e).wait()
            return c

        lax.fori_loop(0, n_zero, start, 0)
        lax.fori_loop(0, n_zero, wait, 0)

    @pl.when(i + 1 < n_tiles)
    def _():
        fetch(i + 1).start()

    fetch(i).wait()
    slot = i % n_buf

    def start_rows(r, c):
        src = xbuf.at[slot, pl.ds(pl.multiple_of(r * pitch, pitch), pitch)]
        for k in range(TOP_K):
            dst = dest_ref[0, k * tm + r]
            pltpu.make_async_copy(src, xb_ref.at[pl.ds(pl.multiple_of(dst * pitch, pitch), pitch)],
                                  out_sem.at[i % 2]).start(priority=k % 2)
        return c

    lax.fori_loop(0, tm, start_rows, 0, unroll=ROW_DMA_UNROLL)

    @pl.when(i > 0)
    def _():
        wait_scatters(i - 1)

    @pl.when(i == n_tiles - 1)
    def _():
        wait_scatters(i)


def _dispatch_call(xn2_tiles, dest_tiles, zero_start, n_rows, *, tm, d):
    pitch = d // LANES
    n_tiles = xn2_tiles.shape[0] // (tm * pitch)
    grid_spec = pltpu.PrefetchScalarGridSpec(
        num_scalar_prefetch=1,
        grid=(n_tiles,),
        in_specs=[pl.BlockSpec(memory_space=pl.ANY),
                  pl.BlockSpec((None, 1, tm * TOP_K), lambda i, z: (i, 0, 0), memory_space=pltpu.SMEM)],
        out_specs=pl.BlockSpec(memory_space=pl.ANY),
        scratch_shapes=[pltpu.VMEM((3, tm * pitch, LANES), F32),
                        pltpu.VMEM((MOE_BLOCK * pitch, LANES), F32),
                        pltpu.SemaphoreType.DMA((3,)), pltpu.SemaphoreType.DMA((2,)),
                        pltpu.SemaphoreType.DMA],
    )
    return pl.pallas_call(
        functools.partial(_dispatch_kernel, pitch=pitch),
        grid_spec=grid_spec,
        out_shape=jax.ShapeDtypeStruct((n_rows * pitch, LANES), F32),
        compiler_params=pltpu.CompilerParams(dimension_semantics=("arbitrary",)),
        name="moe_dispatch",
    )(zero_start, xn2_tiles, dest_tiles)


def _expert_kernel(be_ref, nu_ref, first_ref, slot_ref, next_ref, x_ref, wgu_hbm, bgu_ref, wd_hbm, bd_ref, y_ref,
                   wgu_st, wd_st, wgu_bf, wd_bf, sem):
    b = pl.program_id(0)

    def fetch(e, slot):
        return (pltpu.make_async_copy(wgu_hbm.at[e], wgu_st.at[slot], sem.at[slot, 0]),
                pltpu.make_async_copy(wd_hbm.at[e], wd_st.at[slot], sem.at[slot, 1]))

    @pl.when(b < nu_ref[0])
    def _():
        d_ff, d = wd_bf.shape

        @pl.when(first_ref[b] == 1)
        def _():
            slot = slot_ref[b]

            @pl.when(b == 0)
            def _():
                for cp in fetch(be_ref[b], slot):
                    cp.start()

            for cp in fetch(be_ref[b], slot):
                cp.wait()

            @pl.when(next_ref[b] >= 0)
            def _():
                for cp in fetch(next_ref[b], 1 - slot):
                    cp.start()

            wgu_bf[...] = wgu_st[slot].astype(BF16)
            wd_bf[...] = wd_st[slot].astype(BF16)

        x = _load_row_tiles(x_ref, MOE_BLOCK, d)
        h = jnp.dot(x.astype(BF16), wgu_bf[...], preferred_element_type=F32) + bgu_ref[...]
        g = jnp.minimum(h[:, :d_ff], SWIGLU_LIMIT)
        u = jnp.clip(h[:, d_ff:], -SWIGLU_LIMIT, SWIGLU_LIMIT)
        act = (u + 1.0) * (g * _sigmoid(SWIGLU_ALPHA * g))
        y = jnp.dot(act.astype(BF16), wd_bf[...], preferred_element_type=F32) + bd_ref[...]
        _store_row_tiles(y_ref, y)

    @pl.when(b >= nu_ref[0])
    def _():
        y_ref[...] = jnp.zeros_like(y_ref)


def _expert_call(xb_tiles, block_e, n_used, run_first, run_slot, run_next, wgu, bgu, wd, bd):
    n_exp, d, two_ff = wgu.shape
    d_ff = wd.shape[1]
    blk_rows = MOE_BLOCK * d // LANES
    n_blocks = xb_tiles.shape[0] // blk_rows
    per_e = lambda b, be, *_: (be[b], 0, 0)
    any_spec = pl.BlockSpec(memory_space=pl.ANY)
    grid_spec = pltpu.PrefetchScalarGridSpec(
        num_scalar_prefetch=5,
        grid=(n_blocks,),
        in_specs=[pl.BlockSpec((blk_rows, LANES), lambda b, be, nu, *_: (jnp.minimum(b, nu[0] - 1), 0)),
                  any_spec,
                  pl.BlockSpec((None, 1, two_ff), per_e),
                  any_spec,
                  pl.BlockSpec((None, 1, d), per_e)],
        out_specs=pl.BlockSpec((blk_rows, LANES), lambda b, *_: (b, 0)),
        scratch_shapes=[pltpu.VMEM((2, d, two_ff), F32), pltpu.VMEM((2, d_ff, d), F32),
                        pltpu.VMEM((d, two_ff), BF16), pltpu.VMEM((d_ff, d), BF16),
                        pltpu.SemaphoreType.DMA((2, 2))],
    )
    return pl.pallas_call(
        _expert_kernel,
        grid_spec=grid_spec,
        out_shape=jax.ShapeDtypeStruct(xb_tiles.shape, F32),
        compiler_params=pltpu.CompilerParams(dimension_semantics=("arbitrary",), vmem_limit_bytes=VMEM_LIMIT),
        name="moe_experts",
    )(block_e, n_used, run_first, run_slot, run_next, xb_tiles, wgu, bgu.reshape(n_exp, 1, two_ff), wd,
      bd.reshape(n_exp, 1, d))


def _combine_kernel(xmid_ref, gate_ref, dest_ref, dest_next_ref, gfin_ref, yb_ref, outp_ref, outs_ref,
                    ybuf, ot_buf, sem, *, n_tiles_p):
    i = pl.program_id(0)
    n_tiles = pl.num_programs(0)
    tm, d = xmid_ref.shape
    pitch = d // LANES

    def start_tile(dref, slot):
        def start_rows(r, c):
            for k in range(TOP_K):
                src = dref[0, k * tm + r]
                pltpu.make_async_copy(yb_ref.at[pl.ds(pl.multiple_of(src * pitch, pitch), pitch)],
                                      ybuf.at[slot, k, pl.ds(pl.multiple_of(r * pitch, pitch), pitch)],
                                      sem.at[slot]).start(priority=k % 2)
            return c

        lax.fori_loop(0, tm, start_rows, 0, unroll=ROW_DMA_UNROLL)

    @pl.when(i == 0)
    def _():
        start_tile(dest_ref, 0)

    slot = i % 2

    @pl.when(i + 1 < n_tiles)
    def _():
        start_tile(dest_next_ref, 1 - slot)

    pltpu.make_async_copy(ybuf.at[slot], ybuf.at[slot], sem.at[slot]).wait()
    gate = gate_ref[...]
    y = xmid_ref[...]
    for k in range(TOP_K):
        y = y + gate[:, k:k + 1] * _load_row_tiles(ybuf.at[slot, k], tm, d)
    out = _rms(y, gfin_ref[...])

    @pl.when(i < n_tiles_p)
    def _():
        _store_seq_major(outp_ref, ot_buf, out)

    @pl.when(i >= n_tiles_p)
    def _():
        outs_ref[...] = out


def _combine_call(x_mid, gate, dest_tiles, g_final, yb, *, tm, n_p, n_tiles_p):
    n_tok, d = x_mid.shape
    n_tiles = n_tok // tm
    tt = tm // n_p
    dest_spec = lambda off: pl.BlockSpec((None, 1, tm * TOP_K),
                                         lambda i: (jnp.minimum(i + off, n_tiles - 1), 0, 0),
                                         memory_space=pltpu.SMEM)
    return pl.pallas_call(
        functools.partial(_combine_kernel, n_tiles_p=n_tiles_p),
        grid=(n_tiles,),
        in_specs=[pl.BlockSpec((tm, d), lambda i: (i, 0)),
                  pl.BlockSpec((tm, LANES), lambda i: (i, 0)),
                  dest_spec(0), dest_spec(1),
                  pl.BlockSpec((1, d), lambda i: (0, 0)),
                  pl.BlockSpec(memory_space=pl.ANY)],
        out_specs=(pl.BlockSpec((n_p, tt, d), lambda i: (0, jnp.minimum(i, n_tiles_p - 1), 0)),
                   pl.BlockSpec((tm, d), lambda i: (jnp.maximum(i - n_tiles_p, 0), 0))),
        out_shape=(jax.ShapeDtypeStruct((n_p, n_tiles_p * tt, d), F32),
                   jax.ShapeDtypeStruct(((n_tiles - n_tiles_p) * tm, d), F32)),
        scratch_shapes=[pltpu.VMEM((2, TOP_K, tm * d // LANES, LANES), F32),
                        pltpu.VMEM((d // LANES, tm, LANES), F32),
                        pltpu.SemaphoreType.DMA((2,))],
        compiler_params=pltpu.CompilerParams(dimension_semantics=("arbitrary",), vmem_limit_bytes=VMEM_LIMIT),
        name="moe_combine",
    )(x_mid, gate, dest_tiles, dest_tiles, g_final, yb)


def _to_time_major(x, seq_block):
    n_seqs, t, d = x.shape
    n_sb = n_seqs // seq_block
    return x.reshape(n_sb, seq_block, t, d).transpose(0, 2, 1, 3).reshape(n_sb * t, seq_block, d)


def _from_time_major(x, n_seqs, seq_block):
    d = x.shape[-1]
    n_sb = n_seqs // seq_block
    t = x.size // (n_seqs * d)
    return x.reshape(n_sb, t, seq_block, d).transpose(0, 2, 1, 3).reshape(n_seqs, t, d)


def _layer(xp, xs, state_a, state_b, p, norm_final_g):
    n_p, t_p, d = xp.shape
    n_s, t_s, _ = xs.shape
    n_exp = p["wgu"].shape[0]
    tm = TOKEN_TILE
    sb = tm // t_s
    n_tiles_p = n_p * t_p // tm

    (x_mid, xn2, route, gate, cnt, newa_p, newb_p, newa_s, newb_s) = _mixer_call(
        xp, _to_time_major(xs, sb), _to_time_major(state_a, sb), _to_time_major(state_b, sb),
        p, tt_p=tm // n_p, tt_s=t_s)
    n_tok = x_mid.shape[0]
    n_tiles = n_tok // tm

    counts = cnt[:, 0].astype(jnp.int32)
    padded = (counts + MOE_BLOCK - 1) // MOE_BLOCK * MOE_BLOCK
    pad_end = jnp.cumsum(padded)
    pad_start = pad_end - padded
    is_e = route[:TOP_K, :, None] == jnp.arange(n_exp, dtype=jnp.int32)
    dest = jnp.sum(jnp.where(is_e, pad_start, 0), axis=-1) + route[TOP_K:]
    dest_tiles = dest.reshape(TOP_K, n_tiles, tm).transpose(1, 0, 2).reshape(n_tiles, 1, TOP_K * tm)
    n_blocks = -(-(n_tok * TOP_K) // MOE_BLOCK) + n_exp
    n_used = (pad_end[-1] // MOE_BLOCK).astype(jnp.int32)
    blk_start = jnp.minimum(jnp.arange(n_blocks, dtype=jnp.int32) * MOE_BLOCK, pad_end[-1] - 1)
    block_e = jnp.minimum(jnp.sum(blk_start[:, None] >= pad_end[None, :], axis=1), n_exp - 1).astype(jnp.int32)
    last_blocks = jnp.arange(n_blocks - n_exp, n_blocks, dtype=jnp.int32)
    zero_start = jnp.concatenate([jnp.where(padded > 0, pad_end - MOE_BLOCK, -1),
                                  jnp.where(last_blocks >= n_used, last_blocks * MOE_BLOCK, -1)]).astype(jnp.int32)

    xb = _dispatch_call(xn2, dest_tiles, zero_start, n_blocks * MOE_BLOCK, tm=tm, d=d)
    blk_ids = jnp.arange(n_blocks, dtype=jnp.int32)
    prev_e = jnp.concatenate([jnp.full((1,), -1, jnp.int32), block_e[:-1]])
    run_first = ((block_e != prev_e) & (blk_ids < n_used)).astype(jnp.int32)
    run_slot = ((jnp.cumsum(run_first) - 1) % 2).astype(jnp.int32)
    e_ids = jnp.arange(n_exp, dtype=jnp.int32)
    later = lax.cummin(jnp.where(padded > 0, e_ids, n_exp), axis=0, reverse=True)
    next_of = jnp.concatenate([later[1:], jnp.full((1,), n_exp, jnp.int32)])
    next_of = jnp.where(next_of >= n_exp, -1, next_of)
    run_next = jnp.sum(jnp.where(block_e[:, None] == e_ids[None, :], next_of[None, :], 0), axis=1).astype(jnp.int32)

    yb = _expert_call(xb, block_e, n_used.reshape(1), run_first, run_slot, run_next,
                      p["wgu"], p["bgu"], p["wd"], p["bd"])
    y_p, y_s = _combine_call(x_mid, gate, dest_tiles, norm_final_g.reshape(1, d), yb, tm=tm, n_p=n_p,
                             n_tiles_p=n_tiles_p)

    return (y_p, _from_time_major(y_s, n_s, sb),
            _from_time_major(newa_p, n_p, n_p), _from_time_major(newb_p, n_p, n_p),
            _from_time_major(newa_s, n_s, sb), _from_time_major(newb_s, n_s, sb))


def _prep_params(l, norm_mix_g, w_in, b_gates, conv_a_w, conv_a_b, w_a_out, conv_b_w, conv_b_b, ln_b_g,
                 ln_b_b, w_b_out, w_o, norm_ffn_g, w_router, b_router, w_gu, b_gu, w_down, b_down):
    row = lambda v: v.reshape(1, -1)
    taps = lambda w: jnp.broadcast_to(w[:, None, :], (w.shape[0], SUBLANES, w.shape[1]))
    wr_t = w_router[l].T
    wr_hi = wr_t.astype(BF16)
    return dict(
        gmix=row(norm_mix_g[l]), win=w_in[l].astype(BF16), bg=b_gates[l],
        caw=taps(conv_a_w[l]), cab=row(conv_a_b[l]), waout=w_a_out[l].astype(BF16),
        cbw=taps(conv_b_w[l]), cbb=row(conv_b_b[l]), lng=row(ln_b_g[l]), lnb=row(ln_b_b[l]),
        wbout=w_b_out[l].astype(BF16), wo=w_o[l].astype(BF16), gffn=row(norm_ffn_g[l]),
        wrh=wr_hi, wrl=(wr_t - wr_hi.astype(F32)).astype(BF16), br=b_router[l].reshape(-1, 1),
        wgu=w_gu[l], bgu=b_gu[l], wd=w_down[l], bd=b_down[l])


def kernel(x_prompt, x_sample, state_conv_a, state_conv_b, norm_mix_g, w_in, b_gates, conv_a_w, conv_a_b, w_a_out, conv_b_w, conv_b_b, ln_b_g, ln_b_b, w_b_out, w_o, norm_ffn_g, w_router, b_router, w_gu, b_gu, w_down, b_down, norm_final_g):
    depth = w_in.shape[0]
    assert depth == 1, "the final norm is fused into the last layer's combine call"
    p = _prep_params(0, norm_mix_g, w_in, b_gates, conv_a_w, conv_a_b, w_a_out, conv_b_w, conv_b_b, ln_b_g,
                     ln_b_b, w_b_out, w_o, norm_ffn_g, w_router, b_router, w_gu, b_gu, w_down, b_down)
    y_p, y_s, na_p, nb_p, na_s, nb_s = _layer(x_prompt, x_sample, state_conv_a[0], state_conv_b[0], p,
                                               norm_final_g)
    return (y_p, y_s, na_p[None], nb_p[None], na_s[None], nb_s[None])
```

```python
import functools

import jax
import jax.numpy as jnp
from jax import lax
from jax.experimental import pallas as pl
from jax.experimental.pallas import tpu as pltpu

EPS = 1e-5
SWIGLU_ALPHA = 1.702
SWIGLU_LIMIT = 7.0
TOP_K = 4
MOE_BLOCK = 256
TOKEN_TILE = 256
LANES = 128
SUBLANES = 8
VMEM_LIMIT = 60 * 1024 * 1024
CONV_OUT_BLOCK = 8
ROW_DMA_UNROLL = 8
CONV_LANES = 256
EXPERT_BLOCKS_PER_STEP = 4

F32 = jnp.float32
BF16 = jnp.bfloat16


def _sigmoid(v):
    return 1.0 / (1.0 + jnp.exp(-v))


def _store_row_tiles(ref, value):
    n, d = value.shape
    pitch = d // LANES
    for c in range(pitch):
        ref[pl.ds(c, n, stride=pitch), :] = value[:, c * LANES:(c + 1) * LANES]


def _load_row_tiles(ref, n, d):
    pitch = d // LANES
    return jnp.concatenate([ref[pl.ds(c, n, stride=pitch), :] for c in range(pitch)], axis=1)


def _rms(v, g):
    return v * lax.rsqrt(jnp.mean(v * v, axis=-1, keepdims=True) + EPS) * g


def _causal_conv(ext_ref, w_ref, bias, out_ref, n_out):
    width = w_ref.shape[0]
    n_seq, d = ext_ref.shape[1:]
    nb = CONV_OUT_BLOCK
    assert n_out % nb == 0 and n_seq % SUBLANES == 0 and d % CONV_LANES == 0

    def block(tb, carry):
        t0 = tb * nb
        for sg in range(n_seq // SUBLANES):
            rows = pl.ds(sg * SUBLANES, SUBLANES)
            for lc in range(d // CONV_LANES):
                lanes = pl.ds(lc * CONV_LANES, CONV_LANES)
                loaded = {}
                acc = [None] * nb
                for k in range(width):
                    wk = w_ref[k, :, lanes]
                    for j in range(nb):
                        if j + k not in loaded:
                            loaded[j + k] = ext_ref[t0 + j + k, rows, lanes]
                        term = wk * loaded[j + k]
                        acc[j] = term if acc[j] is None else acc[j] + term
                for j, a in enumerate(acc):
                    row0 = pl.multiple_of((t0 + j) * n_seq + sg * SUBLANES, SUBLANES)
                    out_ref[pl.ds(row0, SUBLANES), lanes] = a + bias[:, lc * CONV_LANES:(lc + 1) * CONV_LANES]
        return carry

    lax.fori_loop(0, n_out // nb, block, 0)
    return out_ref[...]


_MIXER_CONSTS = ("gmix", "win", "bg", "caw", "cab", "waout", "cbw", "cbb", "lng", "lnb", "wbout", "wo",
                 "gffn", "wrh", "wrl", "br")


def _time_major_rows(src_ref, buf, t0, tt):
    n_seq, _, d = src_ref.shape
    for s in range(n_seq):
        for c in range(d // LANES):
            buf[c, pl.ds(s, tt, stride=n_seq), :] = src_ref[s, t0:t0 + tt, c * LANES:(c + 1) * LANES]
    return jnp.concatenate([buf[c] for c in range(d // LANES)], axis=1)


def _store_seq_major(dst_ref, buf, value):
    n_seq, tt, d = dst_ref.shape
    for c in range(d // LANES):
        buf[c] = value[:, c * LANES:(c + 1) * LANES]
    for s in range(n_seq):
        for c in range(d // LANES):
            dst_ref[s, :, c * LANES:(c + 1) * LANES] = buf[c, pl.ds(s, tt, stride=n_seq), :]


def _mixer_tile(x, exta, extb, c, xmid_ref, xn2_ref, route_ref, gate_ref, cnt_ref, cnt_acc, conv_buf):
    rows, d = x.shape
    n_seq = exta.shape[1]
    tt = rows // n_seq
    w_a = c["caw"].shape[0]
    w_b = c["cbw"].shape[0]

    xn = _rms(x, c["gmix"][...]).astype(BF16)

    def proj(g):
        return jnp.dot(xn, c["win"][:, g * d:(g + 1) * d], preferred_element_type=F32)

    exta[w_a - 1:w_a - 1 + tt] = (proj(1) * proj(2)).reshape(tt, n_seq, d)
    conv_a = _causal_conv(exta, c["caw"], c["cab"][...], conv_buf.at[0], tt)
    y_a = jnp.dot((proj(0) * conv_a).astype(BF16), c["waout"][...], preferred_element_type=F32)
    extb[w_b - 1:w_b - 1 + tt] = (proj(3) * _sigmoid(proj(4))).reshape(tt, n_seq, d)
    conv_b = _causal_conv(extb, c["cbw"], c["cbb"][...], conv_buf.at[1], tt)
    mu = jnp.mean(conv_b, axis=-1, keepdims=True)
    cen = conv_b - mu
    ln = cen * lax.rsqrt(jnp.mean(cen * cen, axis=-1, keepdims=True) + EPS) * c["lng"][...] + c["lnb"][...]
    y_b = jnp.dot((ln * _sigmoid(ln)).astype(BF16), c["wbout"][...], preferred_element_type=F32)

    bg = c["bg"]
    merged = _sigmoid(proj(5) + bg[0:1, :]) * y_a + _sigmoid(proj(6) + bg[1:2, :]) * y_b
    x_mid = x + jnp.dot(merged.astype(BF16), c["wo"][...], preferred_element_type=F32)
    xmid_ref[...] = x_mid
    xn2 = _rms(x_mid, c["gffn"][...])
    _store_row_tiles(xn2_ref, xn2)

    n_exp = c["wrh"].shape[0]
    nt = (((1,), (1,)), ((), ()))
    x_hi = xn2.astype(BF16)
    x_lo = (xn2 - x_hi.astype(F32)).astype(BF16)
    logits = (lax.dot_general(c["wrh"][...], x_hi, nt, preferred_element_type=F32)
              + lax.dot_general(c["wrh"][...], x_lo, nt, preferred_element_type=F32)
              + lax.dot_general(c["wrl"][...], x_hi, nt, preferred_element_type=F32)) + c["br"][...]
    e_io = lax.broadcasted_iota(jnp.int32, (n_exp, rows), 0)
    work = logits
    top_v, top_i = [], []
    for _ in range(TOP_K):
        m = jnp.max(work, axis=0, keepdims=True)
        idx = jnp.min(jnp.where(work == m, e_io, n_exp), axis=0, keepdims=True)
        top_v.append(m)
        top_i.append(idx)
        work = jnp.where(e_io == idx, -jnp.inf, work)
    ex = [jnp.exp(v - top_v[0]) for v in top_v]
    den = ex[0] + ex[1] + ex[2] + ex[3]
    onehot = jnp.zeros((n_exp, rows), F32)
    for idx in top_i:
        onehot = onehot + (e_io == idx).astype(F32)
    r_io = lax.broadcasted_iota(jnp.int32, (rows, rows), 0)
    c_io = lax.broadcasted_iota(jnp.int32, (rows, rows), 1)
    before = (r_io < c_io).astype(BF16)
    cnt = cnt_acc[...]
    prefix = jnp.dot(onehot.astype(BF16), before, preferred_element_type=F32) + cnt[:, 0:1]
    pos = [jnp.sum(jnp.where(e_io == idx, prefix, 0.0), axis=0, keepdims=True) for idx in top_i]
    route_ref[...] = jnp.concatenate(top_i + [p.astype(jnp.int32) for p in pos], axis=0)
    gates = jnp.concatenate([e / den for e in ex] + [jnp.zeros((LANES - TOP_K, rows), F32)], axis=0)
    gate_ref[...] = jnp.transpose(gates)
    cnt = cnt + jnp.sum(onehot, axis=1, keepdims=True)
    cnt_acc[...] = cnt
    cnt_ref[...] = cnt


def _mixer_kernel(*refs, n_steps_p):
    n_c = len(_MIXER_CONSTS)
    xp_ref, xs_ref, hsa_ref, hsb_ref = refs[:4]
    c = dict(zip(_MIXER_CONSTS, refs[4:4 + n_c]))
    (xmid_ref, xn2_ref, route_ref, gate_ref, cnt_ref,
     newa_p_ref, newb_p_ref, newa_s_ref, newb_s_ref) = refs[4 + n_c:13 + n_c]
    exta_p, extb_p, exta_s, extb_s, cnt_acc, conv_buf, xt_buf, sem = refs[13 + n_c:]
    i = pl.program_id(0)
    tt_p = xp_ref.shape[1]
    tt_s = xs_ref.shape[0]
    w_a = c["caw"].shape[0]
    w_b = c["cbw"].shape[0]
    tile_args = (c, xmid_ref, xn2_ref, route_ref, gate_ref, cnt_ref, cnt_acc, conv_buf)

    @pl.when(i == 0)
    def _():
        exta_p[0:w_a - 1] = jnp.zeros((w_a - 1,) + exta_p.shape[1:], F32)
        extb_p[0:w_b - 1] = jnp.zeros((w_b - 1,) + extb_p.shape[1:], F32)
        cnt_acc[...] = jnp.zeros_like(cnt_acc)

    @pl.when(i < n_steps_p)
    def _():
        _mixer_tile(_time_major_rows(xp_ref, xt_buf, 0, tt_p), exta_p, extb_p, *tile_args)
        exta_p[0:w_a - 1] = exta_p[tt_p:tt_p + w_a - 1]
        extb_p[0:w_b - 1] = extb_p[tt_p:tt_p + w_b - 1]

        @pl.when(i == n_steps_p - 1)
        def _():
            cp_a = pltpu.make_async_copy(exta_p.at[pl.ds(0, w_a - 1)], newa_p_ref, sem.at[0])
            cp_b = pltpu.make_async_copy(extb_p.at[pl.ds(0, w_b - 1)], newb_p_ref, sem.at[1])
            cp_a.start()
            cp_b.start()
            cp_a.wait()
            cp_b.wait()

    @pl.when(i >= n_steps_p)
    def _():
        q = i - n_steps_p
        in_a = pltpu.make_async_copy(hsa_ref.at[pl.ds(q * (w_a - 1), w_a - 1)],
                                     exta_s.at[pl.ds(0, w_a - 1)], sem.at[0])
        in_b = pltpu.make_async_copy(hsb_ref.at[pl.ds(q * (w_b - 1), w_b - 1)],
                                     extb_s.at[pl.ds(0, w_b - 1)], sem.at[1])
        in_a.start()
        in_b.start()
        in_a.wait()
        in_b.wait()
        _mixer_tile(xs_ref[...].reshape(TOKEN_TILE, xs_ref.shape[-1]), exta_s, extb_s, *tile_args)
        out_a = pltpu.make_async_copy(exta_s.at[pl.ds(tt_s, w_a - 1)],
                                      newa_s_ref.at[pl.ds(q * (w_a - 1), w_a - 1)], sem.at[0])
        out_b = pltpu.make_async_copy(extb_s.at[pl.ds(tt_s, w_b - 1)],
                                      newb_s_ref.at[pl.ds(q * (w_b - 1), w_b - 1)], sem.at[1])
        out_a.start()
        out_b.start()
        out_a.wait()
        out_b.wait()


def _mixer_call(xp, xs_tm, hs_a, hs_b, params, *, tt_p, tt_s):
    n_p, t_p, d = xp.shape
    sb = xs_tm.shape[1]
    w_a = params["caw"].shape[0]
    w_b = params["cbw"].shape[0]
    n_exp = params["wrh"].shape[0]
    assert tt_p * n_p == tt_s * sb == TOKEN_TILE and n_exp % SUBLANES == 0
    rows = TOKEN_TILE
    n_steps_p = t_p // tt_p
    n_steps_s = xs_tm.shape[0] // tt_s
    assert n_steps_p * tt_p == t_p and n_steps_s * tt_s == xs_tm.shape[0]
    n_steps = n_steps_p + n_steps_s
    n_tok = n_steps * rows
    pitch = d // LANES
    consts = [params[n] for n in _MIXER_CONSTS]
    const_spec = lambda a: pl.BlockSpec(a.shape, lambda i, _nd=a.ndim: (0,) * _nd, pipeline_mode=pl.Buffered(1))
    any_spec = pl.BlockSpec(memory_space=pl.ANY)
    in_specs = [
        pl.BlockSpec((n_p, tt_p, d), lambda i: (0, jnp.minimum(i, n_steps_p - 1), 0)),
        pl.BlockSpec((tt_s, sb, d), lambda i: (jnp.maximum(i - n_steps_p, 0), 0, 0)),
        any_spec, any_spec,
    ] + [const_spec(a) for a in consts]
    out_shape = (
        jax.ShapeDtypeStruct((n_tok, d), F32),
        jax.ShapeDtypeStruct((n_tok * pitch, LANES), F32),
        jax.ShapeDtypeStruct((2 * TOP_K, n_tok), jnp.int32),
        jax.ShapeDtypeStruct((n_tok, LANES), F32),
        jax.ShapeDtypeStruct((n_exp, LANES), F32),
        jax.ShapeDtypeStruct((w_a - 1, n_p, d), F32),
        jax.ShapeDtypeStruct((w_b - 1, n_p, d), F32),
        jax.ShapeDtypeStruct(hs_a.shape, F32),
        jax.ShapeDtypeStruct(hs_b.shape, F32),
    )
    out_specs = (
        pl.BlockSpec((rows, d), lambda i: (i, 0)),
        pl.BlockSpec((rows * pitch, LANES), lambda i: (i, 0)),
        pl.BlockSpec((2 * TOP_K, rows), lambda i: (0, i)),
        pl.BlockSpec((rows, LANES), lambda i: (i, 0)),
        pl.BlockSpec((n_exp, LANES), lambda i: (0, 0)),
        any_spec, any_spec, any_spec, any_spec,
    )
    return pl.pallas_call(
        functools.partial(_mixer_kernel, n_steps_p=n_steps_p),
        grid=(n_steps,),
        in_specs=in_specs,
        out_specs=out_specs,
        out_shape=out_shape,
        scratch_shapes=[pltpu.VMEM((tt_p + w_a - 1, n_p, d), F32),
                        pltpu.VMEM((tt_p + w_b - 1, n_p, d), F32),
                        pltpu.VMEM((tt_s + w_a - 1, sb, d), F32),
                        pltpu.VMEM((tt_s + w_b - 1, sb, d), F32),
                        pltpu.VMEM((n_exp, LANES), F32),
                        pltpu.VMEM((2, rows, d), F32),
                        pltpu.VMEM((pitch, rows, LANES), F32),
                        pltpu.SemaphoreType.DMA((2,))],
        compiler_params=pltpu.CompilerParams(dimension_semantics=("arbitrary",), vmem_limit_bytes=VMEM_LIMIT),
        name="mixer_router",
    )(xp, xs_tm, hs_a, hs_b, *consts)


def _dispatch_kernel(zero_ref, x_hbm, dest_ref, xb_ref, xbuf, zero_buf, in_sem, out_sem, zsem, *, pitch):
    i = pl.program_id(0)
    n_tiles = pl.num_programs(0)
    n_buf, tile_rows = xbuf.shape[:2]
    tm = tile_rows // pitch
    n_zero = zero_ref.shape[0]
    blk_rows = zero_buf.shape[0]
    n_all = tm * TOP_K * pitch

    def fetch(t):
        start = t * tile_rows if isinstance(t, int) else pl.multiple_of(t * tile_rows, tile_rows)
        return pltpu.make_async_copy(x_hbm.at[pl.ds(start, tile_rows)], xbuf.at[t % n_buf], in_sem.at[t % n_buf])

    def wait_scatters(t):
        pltpu.make_async_copy(xb_ref.at[pl.ds(0, n_all)], xb_ref.at[pl.ds(0, n_all)], out_sem.at[t % 2]).wait()

    @pl.when(i == 0)
    def _():
        fetch(0).start()
        zero_buf[...] = jnp.zeros_like(zero_buf)

        def zcopy(e):
            start = pl.multiple_of(jnp.maximum(zero_ref[e], 0) * pitch, blk_rows)
            return pltpu.make_async_copy(zero_buf, xb_ref.at[pl.ds(start, blk_rows)], zsem)

        def start(e, c):
            @pl.when(zero_ref[e] >= 0)
            def _():
                zcopy(e).start()
            return c

        def wait(e, c):
            @pl.when(zero_ref[e] >= 0)
            def _():
                zcopy(e).wait()
            return c

        lax.fori_loop(0, n_zero, start, 0)
        lax.fori_loop(0, n_zero, wait, 0)

    @pl.when(i + 1 < n_tiles)
    def _():
        fetch(i + 1).start()

    fetch(i).wait()
    slot = i % n_buf

    def start_rows(r, c):
        src = xbuf.at[slot, pl.ds(pl.multiple_of(r * pitch, pitch), pitch)]
        for k in range(TOP_K):
            dst = dest_ref[0, k * tm + r]
            pltpu.make_async_copy(src, xb_ref.at[pl.ds(pl.multiple_of(dst * pitch, pitch), pitch)],
                                  out_sem.at[i % 2]).start(priority=k % 2)
        return c

    lax.fori_loop(0, tm, start_rows, 0, unroll=ROW_DMA_UNROLL)

    @pl.when(i > 0)
    def _():
        wait_scatters(i - 1)

    @pl.when(i == n_tiles - 1)
    def _():
        wait_scatters(i)


def _dispatch_call(xn2_tiles, dest_tiles, zero_start, n_rows, *, tm, d):
    pitch = d // LANES
    n_tiles = xn2_tiles.shape[0] // (tm * pitch)
    grid_spec = pltpu.PrefetchScalarGridSpec(
        num_scalar_prefetch=1,
        grid=(n_tiles,),
        in_specs=[pl.BlockSpec(memory_space=pl.ANY),
                  pl.BlockSpec((None, 1, tm * TOP_K), lambda i, z: (i, 0, 0), memory_space=pltpu.SMEM)],
        out_specs=pl.BlockSpec(memory_space=pl.ANY),
        scratch_shapes=[pltpu.VMEM((3, tm * pitch, LANES), F32),
                        pltpu.VMEM((MOE_BLOCK * pitch, LANES), F32),
                        pltpu.SemaphoreType.DMA((3,)), pltpu.SemaphoreType.DMA((2,)),
                        pltpu.SemaphoreType.DMA],
    )
    return pl.pallas_call(
        functools.partial(_dispatch_kernel, pitch=pitch),
        grid_spec=grid_spec,
        out_shape=jax.ShapeDtypeStruct((n_rows * pitch, LANES), F32),
        compiler_params=pltpu.CompilerParams(dimension_semantics=("arbitrary",)),
        name="moe_dispatch",
    )(zero_start, xn2_tiles, dest_tiles)


def _expert_kernel(be_ref, nu_ref, first_ref, slot_ref, next_ref, x_ref, wgu_hbm, wd_hbm, *rest):
    n_sub = EXPERT_BLOCKS_PER_STEP
    bias_refs, (y_ref, wgu_st, wd_st, wgu_bf, wd_bf, sem) = rest[:2 * n_sub], rest[2 * n_sub:]
    blk_rows = x_ref.shape[0] // n_sub
    for h in range(n_sub):
        rows = pl.ds(h * blk_rows, blk_rows)
        _expert_block(pl.program_id(0) * n_sub + h, be_ref, nu_ref, first_ref, slot_ref, next_ref,
                      x_ref.at[rows], wgu_hbm, bias_refs[2 * h], wd_hbm, bias_refs[2 * h + 1], y_ref.at[rows],
                      wgu_st, wd_st, wgu_bf, wd_bf, sem)


def _expert_block(b, be_ref, nu_ref, first_ref, slot_ref, next_ref, x_ref, wgu_hbm, bgu_ref, wd_hbm, bd_ref, y_ref,
                  wgu_st, wd_st, wgu_bf, wd_bf, sem):
    def fetch(e, slot):
        return (pltpu.make_async_copy(wgu_hbm.at[e], wgu_st.at[slot], sem.at[slot, 0]),
                pltpu.make_async_copy(wd_hbm.at[e], wd_st.at[slot], sem.at[slot, 1]))

    @pl.when(b < nu_ref[0])
    def _():
        d_ff, d = wd_bf.shape

        @pl.when(first_ref[b] == 1)
        def _():
            slot = slot_ref[b]

            @pl.when(b == 0)
            def _():
                for cp in fetch(be_ref[b], slot):
                    cp.start()

            for cp in fetch(be_ref[b], slot):
                cp.wait()

            @pl.when(next_ref[b] >= 0)
            def _():
                for cp in fetch(next_ref[b], 1 - slot):
                    cp.start()

            wgu_bf[...] = wgu_st[slot].astype(BF16)
            wd_bf[...] = wd_st[slot].astype(BF16)

        x = _load_row_tiles(x_ref, MOE_BLOCK, d)
        h = jnp.dot(x.astype(BF16), wgu_bf[...], preferred_element_type=F32) + bgu_ref[...]
        g = jnp.minimum(h[:, :d_ff], SWIGLU_LIMIT)
        u = jnp.clip(h[:, d_ff:], -SWIGLU_LIMIT, SWIGLU_LIMIT)
        act = (u + 1.0) * (g * _sigmoid(SWIGLU_ALPHA * g))
        y = jnp.dot(act.astype(BF16), wd_bf[...], preferred_element_type=F32) + bd_ref[...]
        _store_row_tiles(y_ref, y)

    @pl.when(b >= nu_ref[0])
    def _():
        y_ref[...] = jnp.zeros(y_ref.shape, F32)


def _expert_call(xb_tiles, block_e, n_used, run_first, run_slot, run_next, wgu, bgu, wd, bd):
    n_exp, d, two_ff = wgu.shape
    d_ff = wd.shape[1]
    n_sub = EXPERT_BLOCKS_PER_STEP
    step_rows = n_sub * MOE_BLOCK * d // LANES
    n_steps = xb_tiles.shape[0] // step_rows
    assert n_steps * step_rows == xb_tiles.shape[0]
    any_spec = pl.BlockSpec(memory_space=pl.ANY)
    bias_specs, biases = [], []
    for h in range(n_sub):
        per_e = lambda s, be, *_, h=h: (be[s * n_sub + h], 0, 0)
        bias_specs += [pl.BlockSpec((None, 1, two_ff), per_e), pl.BlockSpec((None, 1, d), per_e)]
        biases += [bgu.reshape(n_exp, 1, two_ff), bd.reshape(n_exp, 1, d)]
    grid_spec = pltpu.PrefetchScalarGridSpec(
        num_scalar_prefetch=5,
        grid=(n_steps,),
        in_specs=[pl.BlockSpec((step_rows, LANES), lambda s, *_: (s, 0)), any_spec, any_spec] + bias_specs,
        out_specs=pl.BlockSpec((step_rows, LANES), lambda s, *_: (s, 0)),
        scratch_shapes=[pltpu.VMEM((2, d, two_ff), F32), pltpu.VMEM((2, d_ff, d), F32),
                        pltpu.VMEM((d, two_ff), BF16), pltpu.VMEM((d_ff, d), BF16),
                        pltpu.SemaphoreType.DMA((2, 2))],
    )
    return pl.pallas_call(
        _expert_kernel,
        grid_spec=grid_spec,
        out_shape=jax.ShapeDtypeStruct(xb_tiles.shape, F32),
        compiler_params=pltpu.CompilerParams(dimension_semantics=("arbitrary",), vmem_limit_bytes=VMEM_LIMIT),
        name="moe_experts",
    )(block_e, n_used, run_first, run_slot, run_next, xb_tiles, wgu, wd, *biases)


def _combine_kernel(xmid_ref, gate_ref, dest_ref, dest_next_ref, gfin_ref, yb_ref, outp_ref, outs_ref,
                    ybuf, ot_buf, sem, *, n_tiles_p):
    i = pl.program_id(0)
    n_tiles = pl.num_programs(0)
    tm, d = xmid_ref.shape
    pitch = d // LANES

    def start_tile(dref, slot):
        def start_rows(r, c):
            for k in range(TOP_K):
                src = dref[0, k * tm + r]
                pltpu.make_async_copy(yb_ref.at[pl.ds(pl.multiple_of(src * pitch, pitch), pitch)],
                                      ybuf.at[slot, k, pl.ds(pl.multiple_of(r * pitch, pitch), pitch)],
                                      sem.at[slot]).start(priority=k % 2)
            return c

        lax.fori_loop(0, tm, start_rows, 0, unroll=ROW_DMA_UNROLL)

    @pl.when(i == 0)
    def _():
        start_tile(dest_ref, 0)

    slot = i % 2

    @pl.when(i + 1 < n_tiles)
    def _():
        start_tile(dest_next_ref, 1 - slot)

    pltpu.make_async_copy(ybuf.at[slot], ybuf.at[slot], sem.at[slot]).wait()
    gate = gate_ref[...]
    y = xmid_ref[...]
    for k in range(TOP_K):
        y = y + gate[:, k:k + 1] * _load_row_tiles(ybuf.at[slot, k], tm, d)
    out = _rms(y, gfin_ref[...])

    @pl.when(i < n_tiles_p)
    def _():
        _store_seq_major(outp_ref, ot_buf, out)

    @pl.when(i >= n_tiles_p)
    def _():
        outs_ref[...] = out


def _combine_call(x_mid, gate, dest_tiles, g_final, yb, *, tm, n_p, n_tiles_p):
    n_tok, d = x_mid.shape
    n_tiles = n_tok // tm
    tt = tm // n_p
    dest_spec = lambda off: pl.BlockSpec((None, 1, tm * TOP_K),
                                         lambda i: (jnp.minimum(i + off, n_tiles - 1), 0, 0),
                                         memory_space=pltpu.SMEM)
    return pl.pallas_call(
        functools.partial(_combine_kernel, n_tiles_p=n_tiles_p),
        grid=(n_tiles,),
        in_specs=[pl.BlockSpec((tm, d), lambda i: (i, 0)),
                  pl.BlockSpec((tm, LANES), lambda i: (i, 0)),
                  dest_spec(0), dest_spec(1),
                  pl.BlockSpec((1, d), lambda i: (0, 0)),
                  pl.BlockSpec(memory_space=pl.ANY)],
        out_specs=(pl.BlockSpec((n_p, tt, d), lambda i: (0, jnp.minimum(i, n_tiles_p - 1), 0)),
                   pl.BlockSpec((tm, d), lambda i: (jnp.maximum(i - n_tiles_p, 0), 0))),
        out_shape=(jax.ShapeDtypeStruct((n_p, n_tiles_p * tt, d), F32),
                   jax.ShapeDtypeStruct(((n_tiles - n_tiles_p) * tm, d), F32)),
        scratch_shapes=[pltpu.VMEM((2, TOP_K, tm * d // LANES, LANES), F32),
                        pltpu.VMEM((d // LANES, tm, LANES), F32),
                        pltpu.SemaphoreType.DMA((2,))],
        compiler_params=pltpu.CompilerParams(dimension_semantics=("arbitrary",), vmem_limit_bytes=VMEM_LIMIT),
        name="moe_combine",
    )(x_mid, gate, dest_tiles, dest_tiles, g_final, yb)


def _to_time_major(x, seq_block):
    n_seqs, t, d = x.shape
    n_sb = n_seqs // seq_block
    return x.reshape(n_sb, seq_block, t, d).transpose(0, 2, 1, 3).reshape(n_sb * t, seq_block, d)


def _from_time_major(x, n_seqs, seq_block):
    d = x.shape[-1]
    n_sb = n_seqs // seq_block
    t = x.size // (n_seqs * d)
    return x.reshape(n_sb, t, seq_block, d).transpose(0, 2, 1, 3).reshape(n_seqs, t, d)


def _layer(xp, xs, state_a, state_b, p, norm_final_g):
    n_p, t_p, d = xp.shape
    n_s, t_s, _ = xs.shape
    n_exp = p["wgu"].shape[0]
    tm = TOKEN_TILE
    sb = tm // t_s
    n_tiles_p = n_p * t_p // tm

    (x_mid, xn2, route, gate, cnt, newa_p, newb_p, newa_s, newb_s) = _mixer_call(
        xp, _to_time_major(xs, sb), _to_time_major(state_a, sb), _to_time_major(state_b, sb),
        p, tt_p=tm // n_p, tt_s=t_s)
    n_tok = x_mid.shape[0]
    n_tiles = n_tok // tm

    counts = cnt[:, 0].astype(jnp.int32)
    padded = (counts + MOE_BLOCK - 1) // MOE_BLOCK * MOE_BLOCK
    pad_end = jnp.cumsum(padded)
    pad_start = pad_end - padded
    is_e = route[:TOP_K, :, None] == jnp.arange(n_exp, dtype=jnp.int32)
    dest = jnp.sum(jnp.where(is_e, pad_start, 0), axis=-1) + route[TOP_K:]
    dest_tiles = dest.reshape(TOP_K, n_tiles, tm).transpose(1, 0, 2).reshape(n_tiles, 1, TOP_K * tm)
    n_blocks = -(-(n_tok * TOP_K) // MOE_BLOCK) + n_exp
    n_blocks = -(-n_blocks // EXPERT_BLOCKS_PER_STEP) * EXPERT_BLOCKS_PER_STEP
    n_used = (pad_end[-1] // MOE_BLOCK).astype(jnp.int32)
    blk_start = jnp.minimum(jnp.arange(n_blocks, dtype=jnp.int32) * MOE_BLOCK, pad_end[-1] - 1)
    block_e = jnp.minimum(jnp.sum(blk_start[:, None] >= pad_end[None, :], axis=1), n_exp - 1).astype(jnp.int32)
    n_tail = n_exp + EXPERT_BLOCKS_PER_STEP - 1
    last_blocks = jnp.arange(n_blocks - n_tail, n_blocks, dtype=jnp.int32)
    zero_start = jnp.concatenate([jnp.where(padded > 0, pad_end - MOE_BLOCK, -1),
                                  jnp.where(last_blocks >= n_used, last_blocks * MOE_BLOCK, -1)]).astype(jnp.int32)

    xb = _dispatch_call(xn2, dest_tiles, zero_start, n_blocks * MOE_BLOCK, tm=tm, d=d)
    blk_ids = jnp.arange(n_blocks, dtype=jnp.int32)
    prev_e = jnp.concatenate([jnp.full((1,), -1, jnp.int32), block_e[:-1]])
    run_first = ((block_e != prev_e) & (blk_ids < n_used)).astype(jnp.int32)
    run_slot = ((jnp.cumsum(run_first) - 1) % 2).astype(jnp.int32)
    e_ids = jnp.arange(n_exp, dtype=jnp.int32)
    later = lax.cummin(jnp.where(padded > 0, e_ids, n_exp), axis=0, reverse=True)
    next_of = jnp.concatenate([later[1:], jnp.full((1,), n_exp, jnp.int32)])
    next_of = jnp.where(next_of >= n_exp, -1, next_of)
    run_next = jnp.sum(jnp.where(block_e[:, None] == e_ids[None, :], next_of[None, :], 0), axis=1).astype(jnp.int32)

    yb = _expert_call(xb, block_e, n_used.reshape(1), run_first, run_slot, run_next,
                      p["wgu"], p["bgu"], p["wd"], p["bd"])
    y_p, y_s = _combine_call(x_mid, gate, dest_tiles, norm_final_g.reshape(1, d), yb, tm=tm, n_p=n_p,
                             n_tiles_p=n_tiles_p)

    return (y_p, _from_time_major(y_s, n_s, sb),
            _from_time_major(newa_p, n_p, n_p), _from_time_major(newb_p, n_p, n_p),
            _from_time_major(newa_s, n_s, sb), _from_time_major(newb_s, n_s, sb))


def _prep_params(l, norm_mix_g, w_in, b_gates, conv_a_w, conv_a_b, w_a_out, conv_b_w, conv_b_b, ln_b_g,
                 ln_b_b, w_b_out, w_o, norm_ffn_g, w_router, b_router, w_gu, b_gu, w_down, b_down):
    row = lambda v: v.reshape(1, -1)
    taps = lambda w: jnp.broadcast_to(w[:, None, :], (w.shape[0], SUBLANES, w.shape[1]))
    wr_t = w_router[l].T
    wr_hi = wr_t.astype(BF16)
    return dict(
        gmix=row(norm_mix_g[l]), win=w_in[l].astype(BF16), bg=b_gates[l],
        caw=taps(conv_a_w[l]), cab=row(conv_a_b[l]), waout=w_a_out[l].astype(BF16),
        cbw=taps(conv_b_w[l]), cbb=row(conv_b_b[l]), lng=row(ln_b_g[l]), lnb=row(ln_b_b[l]),
        wbout=w_b_out[l].astype(BF16), wo=w_o[l].astype(BF16), gffn=row(norm_ffn_g[l]),
        wrh=wr_hi, wrl=(wr_t - wr_hi.astype(F32)).astype(BF16), br=b_router[l].reshape(-1, 1),
        wgu=w_gu[l], bgu=b_gu[l], wd=w_down[l], bd=b_down[l])


def kernel(x_prompt, x_sample, state_conv_a, state_conv_b, norm_mix_g, w_in, b_gates, conv_a_w, conv_a_b, w_a_out, conv_b_w, conv_b_b, ln_b_g, ln_b_b, w_b_out, w_o, norm_ffn_g, w_router, b_router, w_gu, b_gu, w_down, b_down, norm_final_g):
    depth = w_in.shape[0]
    assert depth == 1, "the final norm is fused into the last layer's combine call"
    p = _prep_params(0, norm_mix_g, w_in, b_gates, conv_a_w, conv_a_b, w_a_out, conv_b_w, conv_b_b, ln_b_g,
                     ln_b_b, w_b_out, w_o, norm_ffn_g, w_router, b_router, w_gu, b_gu, w_down, b_down)
    y_p, y_s, na_p, nb_p, na_s, nb_s = _layer(x_prompt, x_sample, state_conv_a[0], state_conv_b[0], p,
                                               norm_final_g)
    return (y_p, y_s, na_p[None], nb_p[None], na_s[None], nb_s[None])
```

```python
import functools

import jax
import jax.numpy as jnp
from jax import lax
from jax.experimental import pallas as pl
from jax.experimental.pallas import tpu as pltpu

EPS = 1e-5
SWIGLU_ALPHA = 1.702
SWIGLU_LIMIT = 7.0
TOP_K = 4
MOE_BLOCK = 256
TOKEN_TILE = 256
COMBINE_TILE = 512
LANES = 128
SUBLANES = 8
VMEM_LIMIT = 60 * 1024 * 1024
CONV_OUT_BLOCK = 8
ROW_DMA_UNROLL = 8
CONV_LANES = 256
EXPERT_BLOCKS_PER_STEP = 4

F32 = jnp.float32
BF16 = jnp.bfloat16


def _sigmoid(v):
    return 1.0 / (1.0 + jnp.exp(-v))


def _store_row_tiles(ref, value):
    n, d = value.shape
    pitch = d // LANES
    for c in range(pitch):
        ref[pl.ds(c, n, stride=pitch), :] = value[:, c * LANES:(c + 1) * LANES]


def _load_row_tiles(ref, n, d):
    pitch = d // LANES
    return jnp.concatenate([ref[pl.ds(c, n, stride=pitch), :] for c in range(pitch)], axis=1)


def _rms(v, g):
    return v * lax.rsqrt(jnp.mean(v * v, axis=-1, keepdims=True) + EPS) * g


def _causal_conv(ext_ref, w_ref, bias, out_ref, n_out):
    width = w_ref.shape[0]
    n_seq, d = ext_ref.shape[1:]
    nb = CONV_OUT_BLOCK
    assert n_out % nb == 0 and n_seq % SUBLANES == 0 and d % CONV_LANES == 0

    def block(tb, carry):
        t0 = tb * nb
        for sg in range(n_seq // SUBLANES):
            rows = pl.ds(sg * SUBLANES, SUBLANES)
            for lc in range(d // CONV_LANES):
                lanes = pl.ds(lc * CONV_LANES, CONV_LANES)
                loaded = {}
                acc = [None] * nb
                for k in range(width):
                    wk = w_ref[k, :, lanes]
                    for j in range(nb):
                        if j + k not in loaded:
                            loaded[j + k] = ext_ref[t0 + j + k, rows, lanes]
                        term = wk * loaded[j + k]
                        acc[j] = term if acc[j] is None else acc[j] + term
                for j, a in enumerate(acc):
                    row0 = pl.multiple_of((t0 + j) * n_seq + sg * SUBLANES, SUBLANES)
                    out_ref[pl.ds(row0, SUBLANES), lanes] = a + bias[:, lc * CONV_LANES:(lc + 1) * CONV_LANES]
        return carry

    lax.fori_loop(0, n_out // nb, block, 0)
    return out_ref[...]


_MIXER_CONSTS = ("gmix", "win", "bg", "caw", "cab", "waout", "cbw", "cbb", "lng", "lnb", "wbout", "wo",
                 "gffn", "wrh", "wrl", "br")


def _time_major_rows(src_ref, buf, t0, tt):
    n_seq, _, d = src_ref.shape
    for s in range(n_seq):
        for c in range(d // LANES):
            buf[c, pl.ds(s, tt, stride=n_seq), :] = src_ref[s, t0:t0 + tt, c * LANES:(c + 1) * LANES]
    return jnp.concatenate([buf[c] for c in range(d // LANES)], axis=1)


def _store_seq_major(dst_ref, buf, value):
    n_seq, tt, d = dst_ref.shape
    for c in range(d // LANES):
        buf[c] = value[:, c * LANES:(c + 1) * LANES]
    for s in range(n_seq):
        for c in range(d // LANES):
            dst_ref[s, :, c * LANES:(c + 1) * LANES] = buf[c, pl.ds(s, tt, stride=n_seq), :]


def _mixer_tile(x, exta, extb, c, xmid_ref, xn2_ref, route_ref, gate_ref, cnt_ref, cnt_acc, conv_buf):
    rows, d = x.shape
    n_seq = exta.shape[1]
    tt = rows // n_seq
    w_a = c["caw"].shape[0]
    w_b = c["cbw"].shape[0]

    xn = _rms(x, c["gmix"][...]).astype(BF16)

    def proj(g):
        return jnp.dot(xn, c["win"][:, g * d:(g + 1) * d], preferred_element_type=F32)

    exta[w_a - 1:w_a - 1 + tt] = (proj(1) * proj(2)).reshape(tt, n_seq, d)
    conv_a = _causal_conv(exta, c["caw"], c["cab"][...], conv_buf.at[0], tt)
    y_a = jnp.dot((proj(0) * conv_a).astype(BF16), c["waout"][...], preferred_element_type=F32)
    extb[w_b - 1:w_b - 1 + tt] = (proj(3) * _sigmoid(proj(4))).reshape(tt, n_seq, d)
    conv_b = _causal_conv(extb, c["cbw"], c["cbb"][...], conv_buf.at[1], tt)
    mu = jnp.mean(conv_b, axis=-1, keepdims=True)
    cen = conv_b - mu
    ln = cen * lax.rsqrt(jnp.mean(cen * cen, axis=-1, keepdims=True) + EPS) * c["lng"][...] + c["lnb"][...]
    y_b = jnp.dot((ln * _sigmoid(ln)).astype(BF16), c["wbout"][...], preferred_element_type=F32)

    bg = c["bg"]
    merged = _sigmoid(proj(5) + bg[0:1, :]) * y_a + _sigmoid(proj(6) + bg[1:2, :]) * y_b
    x_mid = x + jnp.dot(merged.astype(BF16), c["wo"][...], preferred_element_type=F32)
    xmid_ref[...] = x_mid
    xn2 = _rms(x_mid, c["gffn"][...])
    _store_row_tiles(xn2_ref, xn2)

    n_exp = c["wrh"].shape[0]
    nt = (((1,), (1,)), ((), ()))
    x_hi = xn2.astype(BF16)
    x_lo = (xn2 - x_hi.astype(F32)).astype(BF16)
    logits = (lax.dot_general(c["wrh"][...], x_hi, nt, preferred_element_type=F32)
              + lax.dot_general(c["wrh"][...], x_lo, nt, preferred_element_type=F32)
              + lax.dot_general(c["wrl"][...], x_hi, nt, preferred_element_type=F32)) + c["br"][...]
    e_io = lax.broadcasted_iota(jnp.int32, (n_exp, rows), 0)
    work = logits
    top_v, top_i = [], []
    for _ in range(TOP_K):
        m = jnp.max(work, axis=0, keepdims=True)
        idx = jnp.min(jnp.where(work == m, e_io, n_exp), axis=0, keepdims=True)
        top_v.append(m)
        top_i.append(idx)
        work = jnp.where(e_io == idx, -jnp.inf, work)
    ex = [jnp.exp(v - top_v[0]) for v in top_v]
    den = ex[0] + ex[1] + ex[2] + ex[3]
    onehot = jnp.zeros((n_exp, rows), F32)
    for idx in top_i:
        onehot = onehot + (e_io == idx).astype(F32)
    r_io = lax.broadcasted_iota(jnp.int32, (rows, rows), 0)
    c_io = lax.broadcasted_iota(jnp.int32, (rows, rows), 1)
    before = (r_io < c_io).astype(BF16)
    cnt = cnt_acc[...]
    prefix = jnp.dot(onehot.astype(BF16), before, preferred_element_type=F32) + cnt[:, 0:1]
    pos = [jnp.sum(jnp.where(e_io == idx, prefix, 0.0), axis=0, keepdims=True) for idx in top_i]
    route_ref[...] = jnp.concatenate(top_i + [p.astype(jnp.int32) for p in pos], axis=0)
    gates = jnp.concatenate([e / den for e in ex] + [jnp.zeros((LANES - TOP_K, rows), F32)], axis=0)
    gate_ref[...] = jnp.transpose(gates)
    cnt = cnt + jnp.sum(onehot, axis=1, keepdims=True)
    cnt_acc[...] = cnt
    cnt_ref[...] = cnt


def _mixer_kernel(*refs, n_steps_p):
    n_c = len(_MIXER_CONSTS)
    xp_ref, xs_ref, hsa_ref, hsb_ref = refs[:4]
    c = dict(zip(_MIXER_CONSTS, refs[4:4 + n_c]))
    (xmid_ref, xn2_ref, route_ref, gate_ref, cnt_ref,
     newa_p_ref, newb_p_ref, newa_s_ref, newb_s_ref) = refs[4 + n_c:13 + n_c]
    exta_p, extb_p, exta_s, extb_s, cnt_acc, conv_buf, xt_buf, sem = refs[13 + n_c:]
    i = pl.program_id(0)
    tt_p = xp_ref.shape[1]
    tt_s = xs_ref.shape[0]
    w_a = c["caw"].shape[0]
    w_b = c["cbw"].shape[0]
    tile_args = (c, xmid_ref, xn2_ref, route_ref, gate_ref, cnt_ref, cnt_acc, conv_buf)

    @pl.when(i == 0)
    def _():
        exta_p[0:w_a - 1] = jnp.zeros((w_a - 1,) + exta_p.shape[1:], F32)
        extb_p[0:w_b - 1] = jnp.zeros((w_b - 1,) + extb_p.shape[1:], F32)
        cnt_acc[...] = jnp.zeros_like(cnt_acc)

    @pl.when(i < n_steps_p)
    def _():
        _mixer_tile(_time_major_rows(xp_ref, xt_buf, 0, tt_p), exta_p, extb_p, *tile_args)
        exta_p[0:w_a - 1] = exta_p[tt_p:tt_p + w_a - 1]
        extb_p[0:w_b - 1] = extb_p[tt_p:tt_p + w_b - 1]

        @pl.when(i == n_steps_p - 1)
        def _():
            cp_a = pltpu.make_async_copy(exta_p.at[pl.ds(0, w_a - 1)], newa_p_ref, sem.at[0])
            cp_b = pltpu.make_async_copy(extb_p.at[pl.ds(0, w_b - 1)], newb_p_ref, sem.at[1])
            cp_a.start()
            cp_b.start()
            cp_a.wait()
            cp_b.wait()

    @pl.when(i >= n_steps_p)
    def _():
        q = i - n_steps_p
        in_a = pltpu.make_async_copy(hsa_ref.at[pl.ds(q * (w_a - 1), w_a - 1)],
                                     exta_s.at[pl.ds(0, w_a - 1)], sem.at[0])
        in_b = pltpu.make_async_copy(hsb_ref.at[pl.ds(q * (w_b - 1), w_b - 1)],
                                     extb_s.at[pl.ds(0, w_b - 1)], sem.at[1])
        in_a.start()
        in_b.start()
        in_a.wait()
        in_b.wait()
        _mixer_tile(xs_ref[...].reshape(TOKEN_TILE, xs_ref.shape[-1]), exta_s, extb_s, *tile_args)
        out_a = pltpu.make_async_copy(exta_s.at[pl.ds(tt_s, w_a - 1)],
                                      newa_s_ref.at[pl.ds(q * (w_a - 1), w_a - 1)], sem.at[0])
        out_b = pltpu.make_async_copy(extb_s.at[pl.ds(tt_s, w_b - 1)],
                                      newb_s_ref.at[pl.ds(q * (w_b - 1), w_b - 1)], sem.at[1])
        out_a.start()
        out_b.start()
        out_a.wait()
        out_b.wait()


def _mixer_call(xp, xs_tm, hs_a, hs_b, params, *, tt_p, tt_s):
    n_p, t_p, d = xp.shape
    sb = xs_tm.shape[1]
    w_a = params["caw"].shape[0]
    w_b = params["cbw"].shape[0]
    n_exp = params["wrh"].shape[0]
    assert tt_p * n_p == tt_s * sb == TOKEN_TILE and n_exp % SUBLANES == 0
    rows = TOKEN_TILE
    n_steps_p = t_p // tt_p
    n_steps_s = xs_tm.shape[0] // tt_s
    assert n_steps_p * tt_p == t_p and n_steps_s * tt_s == xs_tm.shape[0]
    n_steps = n_steps_p + n_steps_s
    n_tok = n_steps * rows
    pitch = d // LANES
    consts = [params[n] for n in _MIXER_CONSTS]
    const_spec = lambda a: pl.BlockSpec(a.shape, lambda i, _nd=a.ndim: (0,) * _nd, pipeline_mode=pl.Buffered(1))
    any_spec = pl.BlockSpec(memory_space=pl.ANY)
    in_specs = [
        pl.BlockSpec((n_p, tt_p, d), lambda i: (0, jnp.minimum(i, n_steps_p - 1), 0)),
        pl.BlockSpec((tt_s, sb, d), lambda i: (jnp.maximum(i - n_steps_p, 0), 0, 0)),
        any_spec, any_spec,
    ] + [const_spec(a) for a in consts]
    out_shape = (
        jax.ShapeDtypeStruct((n_tok, d), F32),
        jax.ShapeDtypeStruct((n_tok * pitch, LANES), F32),
        jax.ShapeDtypeStruct((2 * TOP_K, n_tok), jnp.int32),
        jax.ShapeDtypeStruct((n_tok, LANES), F32),
        jax.ShapeDtypeStruct((n_exp, LANES), F32),
        jax.ShapeDtypeStruct((w_a - 1, n_p, d), F32),
        jax.ShapeDtypeStruct((w_b - 1, n_p, d), F32),
        jax.ShapeDtypeStruct(hs_a.shape, F32),
        jax.ShapeDtypeStruct(hs_b.shape, F32),
    )
    out_specs = (
        pl.BlockSpec((rows, d), lambda i: (i, 0)),
        pl.BlockSpec((rows * pitch, LANES), lambda i: (i, 0)),
        pl.BlockSpec((2 * TOP_K, rows), lambda i: (0, i)),
        pl.BlockSpec((rows, LANES), lambda i: (i, 0)),
        pl.BlockSpec((n_exp, LANES), lambda i: (0, 0)),
        any_spec, any_spec, any_spec, any_spec,
    )
    return pl.pallas_call(
        functools.partial(_mixer_kernel, n_steps_p=n_steps_p),
        grid=(n_steps,),
        in_specs=in_specs,
        out_specs=out_specs,
        out_shape=out_shape,
        scratch_shapes=[pltpu.VMEM((tt_p + w_a - 1, n_p, d), F32),
                        pltpu.VMEM((tt_p + w_b - 1, n_p, d), F32),
                        pltpu.VMEM((tt_s + w_a - 1, sb, d), F32),
                        pltpu.VMEM((tt_s + w_b - 1, sb, d), F32),
                        pltpu.VMEM((n_exp, LANES), F32),
                        pltpu.VMEM((2, rows, d), F32),
                        pltpu.VMEM((pitch, rows, LANES), F32),
                        pltpu.SemaphoreType.DMA((2,))],
        compiler_params=pltpu.CompilerParams(dimension_semantics=("arbitrary",), vmem_limit_bytes=VMEM_LIMIT),
        name="mixer_router",
    )(xp, xs_tm, hs_a, hs_b, *consts)


def _dispatch_kernel(zero_ref, x_hbm, dest_ref, xb_ref, xbuf, zero_buf, in_sem, out_sem, zsem, *, pitch):
    i = pl.program_id(0)
    n_tiles = pl.num_programs(0)
    n_buf, tile_rows = xbuf.shape[:2]
    tm = tile_rows // pitch
    n_zero = zero_ref.shape[0]
    blk_rows = zero_buf.shape[0]
    n_all = tm * TOP_K * pitch

    def fetch(t):
        start = t * tile_rows if isinstance(t, int) else pl.multiple_of(t * tile_rows, tile_rows)
        return pltpu.make_async_copy(x_hbm.at[pl.ds(start, tile_rows)], xbuf.at[t % n_buf], in_sem.at[t % n_buf])

    def wait_scatters(t):
        pltpu.make_async_copy(xb_ref.at[pl.ds(0, n_all)], xb_ref.at[pl.ds(0, n_all)], out_sem.at[t % 2]).wait()

    @pl.when(i == 0)
    def _():
        fetch(0).start()
        zero_buf[...] = jnp.zeros_like(zero_buf)

        def zcopy(e):
            start = pl.multiple_of(jnp.maximum(zero_ref[e], 0) * pitch, blk_rows)
            return pltpu.make_async_copy(zero_buf, xb_ref.at[pl.ds(start, blk_rows)], zsem)

        def start(e, c):
            @pl.when(zero_ref[e] >= 0)
            def _():
                zcopy(e).start()
            return c

        def wait(e, c):
            @pl.when(zero_ref[e] >= 0)
            def _():
                zcopy(e).wait()
            return c

        lax.fori_loop(0, n_zero, start, 0)
        lax.fori_loop(0, n_zero, wait, 0)

    @pl.when(i + 1 < n_tiles)
    def _():
        fetch(i + 1).start()

    fetch(i).wait()
    slot = i % n_buf

    def start_rows(r, c):
        src = xbuf.at[slot, pl.ds(pl.multiple_of(r * pitch, pitch), pitch)]
        for k in range(TOP_K):
            dst = dest_ref[0, k * tm + r]
            pltpu.make_async_copy(src, xb_ref.at[pl.ds(pl.multiple_of(dst * pitch, pitch), pitch)],
                                  out_sem.at[i % 2]).start(priority=k % 2)
        return c

    lax.fori_loop(0, tm, start_rows, 0, unroll=ROW_DMA_UNROLL)

    @pl.when(i > 0)
    def _():
        wait_scatters(i - 1)

    @pl.when(i == n_tiles - 1)
    def _():
        wait_scatters(i)


def _dispatch_call(xn2_tiles, dest_tiles, zero_start, n_rows, *, tm, d):
    pitch = d // LANES
    n_tiles = xn2_tiles.shape[0] // (tm * pitch)
    grid_spec = pltpu.PrefetchScalarGridSpec(
        num_scalar_prefetch=1,
        grid=(n_tiles,),
        in_specs=[pl.BlockSpec(memory_space=pl.ANY),
                  pl.BlockSpec((None, 1, tm * TOP_K), lambda i, z: (i, 0, 0), memory_space=pltpu.SMEM)],
        out_specs=pl.BlockSpec(memory_space=pl.ANY),
        scratch_shapes=[pltpu.VMEM((3, tm * pitch, LANES), F32),
                        pltpu.VMEM((MOE_BLOCK * pitch, LANES), F32),
                        pltpu.SemaphoreType.DMA((3,)), pltpu.SemaphoreType.DMA((2,)),
                        pltpu.SemaphoreType.DMA],
    )
    return pl.pallas_call(
        functools.partial(_dispatch_kernel, pitch=pitch),
        grid_spec=grid_spec,
        out_shape=jax.ShapeDtypeStruct((n_rows * pitch, LANES), F32),
        compiler_params=pltpu.CompilerParams(dimension_semantics=("arbitrary",)),
        name="moe_dispatch",
    )(zero_start, xn2_tiles, dest_tiles)


def _expert_kernel(be_ref, nu_ref, first_ref, slot_ref, next_ref, x_ref, wgu_hbm, wd_hbm, *rest):
    n_sub = EXPERT_BLOCKS_PER_STEP
    bias_refs, (y_ref, wgu_st, wd_st, wgu_bf, wd_bf, sem) = rest[:2 * n_sub], rest[2 * n_sub:]
    blk_rows = x_ref.shape[0] // n_sub
    for h in range(n_sub):
        rows = pl.ds(h * blk_rows, blk_rows)
        _expert_block(pl.program_id(0) * n_sub + h, be_ref, nu_ref, first_ref, slot_ref, next_ref,
                      x_ref.at[rows], wgu_hbm, bias_refs[2 * h], wd_hbm, bias_refs[2 * h + 1], y_ref.at[rows],
                      wgu_st, wd_st, wgu_bf, wd_bf, sem)


def _expert_block(b, be_ref, nu_ref, first_ref, slot_ref, next_ref, x_ref, wgu_hbm, bgu_ref, wd_hbm, bd_ref, y_ref,
                  wgu_st, wd_st, wgu_bf, wd_bf, sem):
    def fetch(e, slot):
        return (pltpu.make_async_copy(wgu_hbm.at[e], wgu_st.at[slot], sem.at[slot, 0]),
                pltpu.make_async_copy(wd_hbm.at[e], wd_st.at[slot], sem.at[slot, 1]))

    @pl.when(b < nu_ref[0])
    def _():
        d_ff, d = wd_bf.shape

        @pl.when(first_ref[b] == 1)
        def _():
            slot = slot_ref[b]

            @pl.when(b == 0)
            def _():
                for cp in fetch(be_ref[b], slot):
                    cp.start()

            for cp in fetch(be_ref[b], slot):
                cp.wait()

            @pl.when(next_ref[b] >= 0)
            def _():
                for cp in fetch(next_ref[b], 1 - slot):
                    cp.start()

            wgu_bf[...] = wgu_st[slot].astype(BF16)
            wd_bf[...] = wd_st[slot].astype(BF16)

        x = _load_row_tiles(x_ref, MOE_BLOCK, d)
        h = jnp.dot(x.astype(BF16), wgu_bf[...], preferred_element_type=F32) + bgu_ref[...]
        g = jnp.minimum(h[:, :d_ff], SWIGLU_LIMIT)
        u = jnp.clip(h[:, d_ff:], -SWIGLU_LIMIT, SWIGLU_LIMIT)
        act = (u + 1.0) * (g * _sigmoid(SWIGLU_ALPHA * g))
        y = jnp.dot(act.astype(BF16), wd_bf[...], preferred_element_type=F32) + bd_ref[...]
        _store_row_tiles(y_ref, y)

    @pl.when(b >= nu_ref[0])
    def _():
        y_ref[...] = jnp.zeros(y_ref.shape, F32)


def _expert_call(xb_tiles, block_e, n_used, run_first, run_slot, run_next, wgu, bgu, wd, bd):
    n_exp, d, two_ff = wgu.shape
    d_ff = wd.shape[1]
    n_sub = EXPERT_BLOCKS_PER_STEP
    step_rows = n_sub * MOE_BLOCK * d // LANES
    n_steps = xb_tiles.shape[0] // step_rows
    assert n_steps * step_rows == xb_tiles.shape[0]
    any_spec = pl.BlockSpec(memory_space=pl.ANY)
    bias_specs, biases = [], []
    for h in range(n_sub):
        per_e = lambda s, be, *_, h=h: (be[s * n_sub + h], 0, 0)
        bias_specs += [pl.BlockSpec((None, 1, two_ff), per_e), pl.BlockSpec((None, 1, d), per_e)]
        biases += [bgu.reshape(n_exp, 1, two_ff), bd.reshape(n_exp, 1, d)]
    grid_spec = pltpu.PrefetchScalarGridSpec(
        num_scalar_prefetch=5,
        grid=(n_steps,),
        in_specs=[pl.BlockSpec((step_rows, LANES), lambda s, *_: (s, 0)), any_spec, any_spec] + bias_specs,
        out_specs=pl.BlockSpec((step_rows, LANES), lambda s, *_: (s, 0)),
        scratch_shapes=[pltpu.VMEM((2, d, two_ff), F32), pltpu.VMEM((2, d_ff, d), F32),
                        pltpu.VMEM((d, two_ff), BF16), pltpu.VMEM((d_ff, d), BF16),
                        pltpu.SemaphoreType.DMA((2, 2))],
    )
    return pl.pallas_call(
        _expert_kernel,
        grid_spec=grid_spec,
        out_shape=jax.ShapeDtypeStruct(xb_tiles.shape, F32),
        compiler_params=pltpu.CompilerParams(dimension_semantics=("arbitrary",), vmem_limit_bytes=VMEM_LIMIT),
        name="moe_experts",
    )(block_e, n_used, run_first, run_slot, run_next, xb_tiles, wgu, wd, *biases)


def _combine_kernel(xmid_ref, gate_ref, dest_ref, dest_next_ref, gfin_ref, yb_ref, outp_ref, outs_ref,
                    ybuf, ot_buf, sem, *, n_tiles_p):
    i = pl.program_id(0)
    n_tiles = pl.num_programs(0)
    tm, d = xmid_ref.shape
    pitch = d // LANES

    def start_tile(dref, slot):
        def start_rows(r, c):
            for k in range(TOP_K):
                src = dref[0, k * tm + r]
                pltpu.make_async_copy(yb_ref.at[pl.ds(pl.multiple_of(src * pitch, pitch), pitch)],
                                      ybuf.at[slot, k, pl.ds(pl.multiple_of(r * pitch, pitch), pitch)],
                                      sem.at[slot]).start(priority=k % 2)
            return c

        lax.fori_loop(0, tm, start_rows, 0, unroll=ROW_DMA_UNROLL)

    @pl.when(i == 0)
    def _():
        start_tile(dest_ref, 0)

    slot = i % 2

    @pl.when(i + 1 < n_tiles)
    def _():
        start_tile(dest_next_ref, 1 - slot)

    pltpu.make_async_copy(ybuf.at[slot], ybuf.at[slot], sem.at[slot]).wait()
    gate = gate_ref[...]
    y = xmid_ref[...]
    for k in range(TOP_K):
        y = y + gate[:, k:k + 1] * _load_row_tiles(ybuf.at[slot, k], tm, d)
    out = _rms(y, gfin_ref[...])

    @pl.when(i < n_tiles_p)
    def _():
        _store_seq_major(outp_ref, ot_buf, out)

    @pl.when(i >= n_tiles_p)
    def _():
        outs_ref[...] = out


def _combine_call(x_mid, gate, dest_tiles, g_final, yb, *, tm, n_p, n_tiles_p):
    n_tok, d = x_mid.shape
    n_tiles = n_tok // tm
    tt = tm // n_p
    dest_spec = lambda off: pl.BlockSpec((None, 1, tm * TOP_K),
                                         lambda i: (jnp.minimum(i + off, n_tiles - 1), 0, 0),
                                         memory_space=pltpu.SMEM)
    return pl.pallas_call(
        functools.partial(_combine_kernel, n_tiles_p=n_tiles_p),
        grid=(n_tiles,),
        in_specs=[pl.BlockSpec((tm, d), lambda i: (i, 0)),
                  pl.BlockSpec((tm, LANES), lambda i: (i, 0)),
                  dest_spec(0), dest_spec(1),
                  pl.BlockSpec((1, d), lambda i: (0, 0)),
                  pl.BlockSpec(memory_space=pl.ANY)],
        out_specs=(pl.BlockSpec((n_p, tt, d), lambda i: (0, jnp.minimum(i, n_tiles_p - 1), 0)),
                   pl.BlockSpec((tm, d), lambda i: (jnp.maximum(i - n_tiles_p, 0), 0))),
        out_shape=(jax.ShapeDtypeStruct((n_p, n_tiles_p * tt, d), F32),
                   jax.ShapeDtypeStruct(((n_tiles - n_tiles_p) * tm, d), F32)),
        scratch_shapes=[pltpu.VMEM((2, TOP_K, tm * d // LANES, LANES), F32),
                        pltpu.VMEM((d // LANES, tm, LANES), F32),
                        pltpu.SemaphoreType.DMA((2,))],
        compiler_params=pltpu.CompilerParams(dimension_semantics=("arbitrary",), vmem_limit_bytes=VMEM_LIMIT),
        name="moe_combine",
    )(x_mid, gate, dest_tiles, dest_tiles, g_final, yb)


def _to_time_major(x, seq_block):
    n_seqs, t, d = x.shape
    n_sb = n_seqs // seq_block
    return x.reshape(n_sb, seq_block, t, d).transpose(0, 2, 1, 3).reshape(n_sb * t, seq_block, d)


def _from_time_major(x, n_seqs, seq_block):
    d = x.shape[-1]
    n_sb = n_seqs // seq_block
    t = x.size // (n_seqs * d)
    return x.reshape(n_sb, t, seq_block, d).transpose(0, 2, 1, 3).reshape(n_seqs, t, d)


def _layer(xp, xs, state_a, state_b, p, norm_final_g):
    n_p, t_p, d = xp.shape
    n_s, t_s, _ = xs.shape
    n_exp = p["wgu"].shape[0]
    tm = TOKEN_TILE
    sb = tm // t_s
    n_tiles_p = n_p * t_p // tm

    (x_mid, xn2, route, gate, cnt, newa_p, newb_p, newa_s, newb_s) = _mixer_call(
        xp, _to_time_major(xs, sb), _to_time_major(state_a, sb), _to_time_major(state_b, sb),
        p, tt_p=tm // n_p, tt_s=t_s)
    n_tok = x_mid.shape[0]
    n_tiles = n_tok // tm

    counts = cnt[:, 0].astype(jnp.int32)
    padded = (counts + MOE_BLOCK - 1) // MOE_BLOCK * MOE_BLOCK
    pad_end = jnp.cumsum(padded)
    pad_start = pad_end - padded
    is_e = route[:TOP_K, :, None] == jnp.arange(n_exp, dtype=jnp.int32)
    dest = jnp.sum(jnp.where(is_e, pad_start, 0), axis=-1) + route[TOP_K:]
    per_tile = lambda t: dest.reshape(TOP_K, n_tok // t, t).transpose(1, 0, 2).reshape(n_tok // t, 1, TOP_K * t)
    dest_tiles = per_tile(tm)
    n_blocks = -(-(n_tok * TOP_K) // MOE_BLOCK) + n_exp
    n_blocks = -(-n_blocks // EXPERT_BLOCKS_PER_STEP) * EXPERT_BLOCKS_PER_STEP
    n_used = (pad_end[-1] // MOE_BLOCK).astype(jnp.int32)
    blk_start = jnp.minimum(jnp.arange(n_blocks, dtype=jnp.int32) * MOE_BLOCK, pad_end[-1] - 1)
    block_e = jnp.minimum(jnp.sum(blk_start[:, None] >= pad_end[None, :], axis=1), n_exp - 1).astype(jnp.int32)
    n_tail = n_exp + EXPERT_BLOCKS_PER_STEP - 1
    last_blocks = jnp.arange(n_blocks - n_tail, n_blocks, dtype=jnp.int32)
    zero_start = jnp.concatenate([jnp.where(padded > 0, pad_end - MOE_BLOCK, -1),
                                  jnp.where(last_blocks >= n_used, last_blocks * MOE_BLOCK, -1)]).astype(jnp.int32)

    xb = _dispatch_call(xn2, dest_tiles, zero_start, n_blocks * MOE_BLOCK, tm=tm, d=d)
    blk_ids = jnp.arange(n_blocks, dtype=jnp.int32)
    prev_e = jnp.concatenate([jnp.full((1,), -1, jnp.int32), block_e[:-1]])
    run_first = ((block_e != prev_e) & (blk_ids < n_used)).astype(jnp.int32)
    run_slot = ((jnp.cumsum(run_first) - 1) % 2).astype(jnp.int32)
    e_ids = jnp.arange(n_exp, dtype=jnp.int32)
    later = lax.cummin(jnp.where(padded > 0, e_ids, n_exp), axis=0, reverse=True)
    next_of = jnp.concatenate([later[1:], jnp.full((1,), n_exp, jnp.int32)])
    next_of = jnp.where(next_of >= n_exp, -1, next_of)
    run_next = jnp.sum(jnp.where(block_e[:, None] == e_ids[None, :], next_of[None, :], 0), axis=1).astype(jnp.int32)

    yb = _expert_call(xb, block_e, n_used.reshape(1), run_first, run_slot, run_next,
                      p["wgu"], p["bgu"], p["wd"], p["bd"])
    y_p, y_s = _combine_call(x_mid, gate, per_tile(COMBINE_TILE), norm_final_g.reshape(1, d), yb,
                             tm=COMBINE_TILE, n_p=n_p, n_tiles_p=n_p * t_p // COMBINE_TILE)

    return (y_p, _from_time_major(y_s, n_s, sb),
            _from_time_major(newa_p, n_p, n_p), _from_time_major(newb_p, n_p, n_p),
            _from_time_major(newa_s, n_s, sb), _from_time_major(newb_s, n_s, sb))


def _prep_params(l, norm_mix_g, w_in, b_gates, conv_a_w, conv_a_b, w_a_out, conv_b_w, conv_b_b, ln_b_g,
                 ln_b_b, w_b_out, w_o, norm_ffn_g, w_router, b_router, w_gu, b_gu, w_down, b_down):
    row = lambda v: v.reshape(1, -1)
    taps = lambda w: jnp.broadcast_to(w[:, None, :], (w.shape[0], SUBLANES, w.shape[1]))
    wr_t = w_router[l].T
    wr_hi = wr_t.astype(BF16)
    return dict(
        gmix=row(norm_mix_g[l]), win=w_in[l].astype(BF16), bg=b_gates[l],
        caw=taps(conv_a_w[l]), cab=row(conv_a_b[l]), waout=w_a_out[l].astype(BF16),
        cbw=taps(conv_b_w[l]), cbb=row(conv_b_b[l]), lng=row(ln_b_g[l]), lnb=row(ln_b_b[l]),
        wbout=w_b_out[l].astype(BF16), wo=w_o[l].astype(BF16), gffn=row(norm_ffn_g[l]),
        wrh=wr_hi, wrl=(wr_t - wr_hi.astype(F32)).astype(BF16), br=b_router[l].reshape(-1, 1),
        wgu=w_gu[l], bgu=b_gu[l], wd=w_down[l], bd=b_down[l])


def kernel(x_prompt, x_sample, state_conv_a, state_conv_b, norm_mix_g, w_in, b_gates, conv_a_w, conv_a_b, w_a_out, conv_b_w, conv_b_b, ln_b_g, ln_b_b, w_b_out, w_o, norm_ffn_g, w_router, b_router, w_gu, b_gu, w_down, b_down, norm_final_g):
    depth = w_in.shape[0]
    assert depth == 1, "the final norm is fused into the last layer's combine call"
    p = _prep_params(0, norm_mix_g, w_in, b_gates, conv_a_w, conv_a_b, w_a_out, conv_b_w, conv_b_b, ln_b_g,
                     ln_b_b, w_b_out, w_o, norm_ffn_g, w_router, b_router, w_gu, b_gu, w_down, b_down)
    y_p, y_s, na_p, nb_p, na_s, nb_s = _layer(x_prompt, x_sample, state_conv_a[0], state_conv_b[0], p,
                                               norm_final_g)
    return (y_p, y_s, na_p[None], nb_p[None], na_s[None], nb_s[None])
```

```python
import functools

import jax
import jax.numpy as jnp
from jax import lax
from jax.experimental import pallas as pl
from jax.experimental.pallas import tpu as pltpu

EPS = 1e-5
SWIGLU_ALPHA = 1.702
SWIGLU_LIMIT = 7.0
TOP_K = 4
MOE_BLOCK = 256
TOKEN_TILE = 256
LANES = 128
SUBLANES = 8
VMEM_LIMIT = 60 * 1024 * 1024
CONV_OUT_BLOCK = 8
ROW_DMA_UNROLL = 8
CONV_LANES = 256
EXPERT_BLOCKS_PER_STEP = 6

F32 = jnp.float32
BF16 = jnp.bfloat16


def _sigmoid(v):
    return 1.0 / (1.0 + jnp.exp(-v))


def _store_row_tiles(ref, value):
    n, d = value.shape
    pitch = d // LANES
    for c in range(pitch):
        ref[pl.ds(c, n, stride=pitch), :] = value[:, c * LANES:(c + 1) * LANES]


def _load_row_tiles(ref, n, d):
    pitch = d // LANES
    return jnp.concatenate([ref[pl.ds(c, n, stride=pitch), :] for c in range(pitch)], axis=1)


def _rms(v, g):
    return v * lax.rsqrt(jnp.mean(v * v, axis=-1, keepdims=True) + EPS) * g


def _causal_conv(ext_ref, w_ref, bias, out_ref, n_out):
    width = w_ref.shape[0]
    n_seq, d = ext_ref.shape[1:]
    nb = CONV_OUT_BLOCK
    assert n_out % nb == 0 and n_seq % SUBLANES == 0 and d % CONV_LANES == 0

    def block(tb, carry):
        t0 = tb * nb
        for sg in range(n_seq // SUBLANES):
            rows = pl.ds(sg * SUBLANES, SUBLANES)
            for lc in range(d // CONV_LANES):
                lanes = pl.ds(lc * CONV_LANES, CONV_LANES)
                loaded = {}
                acc = [None] * nb
                for k in range(width):
                    wk = w_ref[k, :, lanes]
                    for j in range(nb):
                        if j + k not in loaded:
                            loaded[j + k] = ext_ref[t0 + j + k, rows, lanes]
                        term = wk * loaded[j + k]
                        acc[j] = term if acc[j] is None else acc[j] + term
                for j, a in enumerate(acc):
                    row0 = pl.multiple_of((t0 + j) * n_seq + sg * SUBLANES, SUBLANES)
                    out_ref[pl.ds(row0, SUBLANES), lanes] = a + bias[:, lc * CONV_LANES:(lc + 1) * CONV_LANES]
        return carry

    lax.fori_loop(0, n_out // nb, block, 0)
    return out_ref[...]


_MIXER_CONSTS = ("gmix", "win", "bg", "caw", "cab", "waout", "cbw", "cbb", "lng", "lnb", "wbout", "wo",
                 "gffn", "wrh", "wrl", "br")


def _time_major_rows(src_ref, buf, t0, tt):
    n_seq, _, d = src_ref.shape
    for s in range(n_seq):
        for c in range(d // LANES):
            buf[c, pl.ds(s, tt, stride=n_seq), :] = src_ref[s, t0:t0 + tt, c * LANES:(c + 1) * LANES]
    return jnp.concatenate([buf[c] for c in range(d // LANES)], axis=1)


def _store_seq_major(dst_ref, buf, value):
    n_seq, tt, d = dst_ref.shape
    for c in range(d // LANES):
        buf[c] = value[:, c * LANES:(c + 1) * LANES]
    for s in range(n_seq):
        for c in range(d // LANES):
            dst_ref[s, :, c * LANES:(c + 1) * LANES] = buf[c, pl.ds(s, tt, stride=n_seq), :]


def _mixer_tile(x, exta, extb, c, xmid_ref, xn2_ref, route_ref, gate_ref, cnt_ref, cnt_acc, conv_buf):
    rows, d = x.shape
    n_seq = exta.shape[1]
    tt = rows // n_seq
    w_a = c["caw"].shape[0]
    w_b = c["cbw"].shape[0]

    xn = _rms(x, c["gmix"][...]).astype(BF16)

    def proj(g):
        return jnp.dot(xn, c["win"][:, g * d:(g + 1) * d], preferred_element_type=F32)

    exta[w_a - 1:w_a - 1 + tt] = (proj(1) * proj(2)).reshape(tt, n_seq, d)
    conv_a = _causal_conv(exta, c["caw"], c["cab"][...], conv_buf.at[0], tt)
    y_a = jnp.dot((proj(0) * conv_a).astype(BF16), c["waout"][...], preferred_element_type=F32)
    extb[w_b - 1:w_b - 1 + tt] = (proj(3) * _sigmoid(proj(4))).reshape(tt, n_seq, d)
    conv_b = _causal_conv(extb, c["cbw"], c["cbb"][...], conv_buf.at[1], tt)
    mu = jnp.mean(conv_b, axis=-1, keepdims=True)
    cen = conv_b - mu
    ln = cen * lax.rsqrt(jnp.mean(cen * cen, axis=-1, keepdims=True) + EPS) * c["lng"][...] + c["lnb"][...]
    y_b = jnp.dot((ln * _sigmoid(ln)).astype(BF16), c["wbout"][...], preferred_element_type=F32)

    bg = c["bg"]
    merged = _sigmoid(proj(5) + bg[0:1, :]) * y_a + _sigmoid(proj(6) + bg[1:2, :]) * y_b
    x_mid = x + jnp.dot(merged.astype(BF16), c["wo"][...], preferred_element_type=F32)
    xmid_ref[...] = x_mid
    xn2 = _rms(x_mid, c["gffn"][...])
    _store_row_tiles(xn2_ref, xn2)

    n_exp = c["wrh"].shape[0]
    nt = (((1,), (1,)), ((), ()))
    x_hi = xn2.astype(BF16)
    x_lo = (xn2 - x_hi.astype(F32)).astype(BF16)
    logits = (lax.dot_general(c["wrh"][...], x_hi, nt, preferred_element_type=F32)
              + lax.dot_general(c["wrh"][...], x_lo, nt, preferred_element_type=F32)
              + lax.dot_general(c["wrl"][...], x_hi, nt, preferred_element_type=F32)) + c["br"][...]
    e_io = lax.broadcasted_iota(jnp.int32, (n_exp, rows), 0)
    work = logits
    top_v, top_i = [], []
    for _ in range(TOP_K):
        m = jnp.max(work, axis=0, keepdims=True)
        idx = jnp.min(jnp.where(work == m, e_io, n_exp), axis=0, keepdims=True)
        top_v.append(m)
        top_i.append(idx)
        work = jnp.where(e_io == idx, -jnp.inf, work)
    ex = [jnp.exp(v - top_v[0]) for v in top_v]
    den = ex[0] + ex[1] + ex[2] + ex[3]
    onehot = jnp.zeros((n_exp, rows), F32)
    for idx in top_i:
        onehot = onehot + (e_io == idx).astype(F32)
    r_io = lax.broadcasted_iota(jnp.int32, (rows, rows), 0)
    c_io = lax.broadcasted_iota(jnp.int32, (rows, rows), 1)
    before = (r_io < c_io).astype(BF16)
    cnt = cnt_acc[...]
    prefix = jnp.dot(onehot.astype(BF16), before, preferred_element_type=F32) + cnt[:, 0:1]
    pos = [jnp.sum(jnp.where(e_io == idx, prefix, 0.0), axis=0, keepdims=True) for idx in top_i]
    route_ref[...] = jnp.concatenate(top_i + [p.astype(jnp.int32) for p in pos], axis=0)
    gates = jnp.concatenate([e / den for e in ex] + [jnp.zeros((LANES - TOP_K, rows), F32)], axis=0)
    gate_ref[...] = jnp.transpose(gates)
    cnt = cnt + jnp.sum(onehot, axis=1, keepdims=True)
    cnt_acc[...] = cnt
    cnt_ref[...] = cnt


def _mixer_kernel(*refs, n_steps_p):
    n_c = len(_MIXER_CONSTS)
    xp_ref, xs_ref, hsa_ref, hsb_ref = refs[:4]
    c = dict(zip(_MIXER_CONSTS, refs[4:4 + n_c]))
    (xmid_ref, xn2_ref, route_ref, gate_ref, cnt_ref,
     newa_p_ref, newb_p_ref, newa_s_ref, newb_s_ref) = refs[4 + n_c:13 + n_c]
    exta_p, extb_p, exta_s, extb_s, cnt_acc, conv_buf, xt_buf, sem = refs[13 + n_c:]
    i = pl.program_id(0)
    tt_p = xp_ref.shape[1]
    tt_s = xs_ref.shape[0]
    w_a = c["caw"].shape[0]
    w_b = c["cbw"].shape[0]
    tile_args = (c, xmid_ref, xn2_ref, route_ref, gate_ref, cnt_ref, cnt_acc, conv_buf)

    @pl.when(i == 0)
    def _():
        exta_p[0:w_a - 1] = jnp.zeros((w_a - 1,) + exta_p.shape[1:], F32)
        extb_p[0:w_b - 1] = jnp.zeros((w_b - 1,) + extb_p.shape[1:], F32)
        cnt_acc[...] = jnp.zeros_like(cnt_acc)

    @pl.when(i < n_steps_p)
    def _():
        _mixer_tile(_time_major_rows(xp_ref, xt_buf, 0, tt_p), exta_p, extb_p, *tile_args)
        exta_p[0:w_a - 1] = exta_p[tt_p:tt_p + w_a - 1]
        extb_p[0:w_b - 1] = extb_p[tt_p:tt_p + w_b - 1]

        @pl.when(i == n_steps_p - 1)
        def _():
            cp_a = pltpu.make_async_copy(exta_p.at[pl.ds(0, w_a - 1)], newa_p_ref, sem.at[0])
            cp_b = pltpu.make_async_copy(extb_p.at[pl.ds(0, w_b - 1)], newb_p_ref, sem.at[1])
            cp_a.start()
            cp_b.start()
            cp_a.wait()
            cp_b.wait()

    @pl.when(i >= n_steps_p)
    def _():
        q = i - n_steps_p
        in_a = pltpu.make_async_copy(hsa_ref.at[pl.ds(q * (w_a - 1), w_a - 1)],
                                     exta_s.at[pl.ds(0, w_a - 1)], sem.at[0])
        in_b = pltpu.make_async_copy(hsb_ref.at[pl.ds(q * (w_b - 1), w_b - 1)],
                                     extb_s.at[pl.ds(0, w_b - 1)], sem.at[1])
        in_a.start()
        in_b.start()
        in_a.wait()
        in_b.wait()
        _mixer_tile(xs_ref[...].reshape(TOKEN_TILE, xs_ref.shape[-1]), exta_s, extb_s, *tile_args)
        out_a = pltpu.make_async_copy(exta_s.at[pl.ds(tt_s, w_a - 1)],
                                      newa_s_ref.at[pl.ds(q * (w_a - 1), w_a - 1)], sem.at[0])
        out_b = pltpu.make_async_copy(extb_s.at[pl.ds(tt_s, w_b - 1)],
                                      newb_s_ref.at[pl.ds(q * (w_b - 1), w_b - 1)], sem.at[1])
        out_a.start()
        out_b.start()
        out_a.wait()
        out_b.wait()


def _mixer_call(xp, xs_tm, hs_a, hs_b, params, *, tt_p, tt_s):
    n_p, t_p, d = xp.shape
    sb = xs_tm.shape[1]
    w_a = params["caw"].shape[0]
    w_b = params["cbw"].shape[0]
    n_exp = params["wrh"].shape[0]
    assert tt_p * n_p == tt_s * sb == TOKEN_TILE and n_exp % SUBLANES == 0
    rows = TOKEN_TILE
    n_steps_p = t_p // tt_p
    n_steps_s = xs_tm.shape[0] // tt_s
    assert n_steps_p * tt_p == t_p and n_steps_s * tt_s == xs_tm.shape[0]
    n_steps = n_steps_p + n_steps_s
    n_tok = n_steps * rows
    pitch = d // LANES
    consts = [params[n] for n in _MIXER_CONSTS]
    const_spec = lambda a: pl.BlockSpec(a.shape, lambda i, _nd=a.ndim: (0,) * _nd, pipeline_mode=pl.Buffered(1))
    any_spec = pl.BlockSpec(memory_space=pl.ANY)
    in_specs = [
        pl.BlockSpec((n_p, tt_p, d), lambda i: (0, jnp.minimum(i, n_steps_p - 1), 0)),
        pl.BlockSpec((tt_s, sb, d), lambda i: (jnp.maximum(i - n_steps_p, 0), 0, 0)),
        any_spec, any_spec,
    ] + [const_spec(a) for a in consts]
    out_shape = (
        jax.ShapeDtypeStruct((n_tok, d), F32),
        jax.ShapeDtypeStruct((n_tok * pitch, LANES), F32),
        jax.ShapeDtypeStruct((2 * TOP_K, n_tok), jnp.int32),
        jax.ShapeDtypeStruct((n_tok, LANES), F32),
        jax.ShapeDtypeStruct((n_exp, LANES), F32),
        jax.ShapeDtypeStruct((w_a - 1, n_p, d), F32),
        jax.ShapeDtypeStruct((w_b - 1, n_p, d), F32),
        jax.ShapeDtypeStruct(hs_a.shape, F32),
        jax.ShapeDtypeStruct(hs_b.shape, F32),
    )
    out_specs = (
        pl.BlockSpec((rows, d), lambda i: (i, 0)),
        pl.BlockSpec((rows * pitch, LANES), lambda i: (i, 0)),
        pl.BlockSpec((2 * TOP_K, rows), lambda i: (0, i)),
        pl.BlockSpec((rows, LANES), lambda i: (i, 0)),
        pl.BlockSpec((n_exp, LANES), lambda i: (0, 0)),
        any_spec, any_spec, any_spec, any_spec,
    )
    return pl.pallas_call(
        functools.partial(_mixer_kernel, n_steps_p=n_steps_p),
        grid=(n_steps,),
        in_specs=in_specs,
        out_specs=out_specs,
        out_shape=out_shape,
        scratch_shapes=[pltpu.VMEM((tt_p + w_a - 1, n_p, d), F32),
                        pltpu.VMEM((tt_p + w_b - 1, n_p, d), F32),
                        pltpu.VMEM((tt_s + w_a - 1, sb, d), F32),
                        pltpu.VMEM((tt_s + w_b - 1, sb, d), F32),
                        pltpu.VMEM((n_exp, LANES), F32),
                        pltpu.VMEM((2, rows, d), F32),
                        pltpu.VMEM((pitch, rows, LANES), F32),
                        pltpu.SemaphoreType.DMA((2,))],
        compiler_params=pltpu.CompilerParams(dimension_semantics=("arbitrary",), vmem_limit_bytes=VMEM_LIMIT),
        name="mixer_router",
    )(xp, xs_tm, hs_a, hs_b, *consts)


def _dispatch_kernel(zero_ref, x_hbm, dest_ref, xb_ref, xbuf, zero_buf, in_sem, out_sem, zsem, *, pitch):
    i = pl.program_id(0)
    n_tiles = pl.num_programs(0)
    n_buf, tile_rows = xbuf.shape[:2]
    tm = tile_rows // pitch
    n_zero = zero_ref.shape[0]
    blk_rows = zero_buf.shape[0]
    n_all = tm * TOP_K * pitch

    def fetch(t):
        start = t * tile_rows if isinstance(t, int) else pl.multiple_of(t * tile_rows, tile_rows)
        return pltpu.make_async_copy(x_hbm.at[pl.ds(start, tile_rows)], xbuf.at[t % n_buf], in_sem.at[t % n_buf])

    def wait_scatters(t):
        pltpu.make_async_copy(xb_ref.at[pl.ds(0, n_all)], xb_ref.at[pl.ds(0, n_all)], out_sem.at[t % 2]).wait()

    @pl.when(i == 0)
    def _():
        fetch(0).start()
        zero_buf[...] = jnp.zeros_like(zero_buf)

        def zcopy(e):
            start = pl.multiple_of(jnp.maximum(zero_ref[e], 0) * pitch, blk_rows)
            return pltpu.make_async_copy(zero_buf, xb_ref.at[pl.ds(start, blk_rows)], zsem)

        def start(e, c):
            @pl.when(zero_ref[e] >= 0)
            def _():
                zcopy(e).start()
            return c

        def wait(e, c):
            @pl.when(zero_ref[e] >= 0)
            def _():
                zcopy(e).wait()
            return c

        lax.fori_loop(0, n_zero, start, 0)
        lax.fori_loop(0, n_zero, wait, 0)

    @pl.when(i + 1 < n_tiles)
    def _():
        fetch(i + 1).start()

    fetch(i).wait()
    slot = i % n_buf

    def start_rows(r, c):
        src = xbuf.at[slot, pl.ds(pl.multiple_of(r * pitch, pitch), pitch)]
        for k in range(TOP_K):
            dst = dest_ref[0, k * tm + r]
            pltpu.make_async_copy(src, xb_ref.at[pl.ds(pl.multiple_of(dst * pitch, pitch), pitch)],
                                  out_sem.at[i % 2]).start(priority=k % 2)
        return c

    lax.fori_loop(0, tm, start_rows, 0, unroll=ROW_DMA_UNROLL)

    @pl.when(i > 0)
    def _():
        wait_scatters(i - 1)

    @pl.when(i == n_tiles - 1)
    def _():
        wait_scatters(i)


def _dispatch_call(xn2_tiles, dest_tiles, zero_start, n_rows, *, tm, d):
    pitch = d // LANES
    n_tiles = xn2_tiles.shape[0] // (tm * pitch)
    grid_spec = pltpu.PrefetchScalarGridSpec(
        num_scalar_prefetch=1,
        grid=(n_tiles,),
        in_specs=[pl.BlockSpec(memory_space=pl.ANY),
                  pl.BlockSpec((None, 1, tm * TOP_K), lambda i, z: (i, 0, 0), memory_space=pltpu.SMEM)],
        out_specs=pl.BlockSpec(memory_space=pl.ANY),
        scratch_shapes=[pltpu.VMEM((3, tm * pitch, LANES), F32),
                        pltpu.VMEM((MOE_BLOCK * pitch, LANES), F32),
                        pltpu.SemaphoreType.DMA((3,)), pltpu.SemaphoreType.DMA((2,)),
                        pltpu.SemaphoreType.DMA],
    )
    return pl.pallas_call(
        functools.partial(_dispatch_kernel, pitch=pitch),
        grid_spec=grid_spec,
        out_shape=jax.ShapeDtypeStruct((n_rows * pitch, LANES), F32),
        compiler_params=pltpu.CompilerParams(dimension_semantics=("arbitrary",)),
        name="moe_dispatch",
    )(zero_start, xn2_tiles, dest_tiles)


def _expert_kernel(be_ref, nu_ref, first_ref, slot_ref, next_ref, x_ref, wgu_hbm, wd_hbm, *rest):
    n_sub = EXPERT_BLOCKS_PER_STEP
    bias_refs, (y_ref, wgu_st, wd_st, wgu_bf, wd_bf, sem) = rest[:2 * n_sub], rest[2 * n_sub:]
    blk_rows = x_ref.shape[0] // n_sub
    for h in range(n_sub):
        rows = pl.ds(h * blk_rows, blk_rows)
        _expert_block(pl.program_id(0) * n_sub + h, be_ref, nu_ref, first_ref, slot_ref, next_ref,
                      x_ref.at[rows], wgu_hbm, bias_refs[2 * h], wd_hbm, bias_refs[2 * h + 1], y_ref.at[rows],
                      wgu_st, wd_st, wgu_bf, wd_bf, sem)


def _expert_block(b, be_ref, nu_ref, first_ref, slot_ref, next_ref, x_ref, wgu_hbm, bgu_ref, wd_hbm, bd_ref, y_ref,
                  wgu_st, wd_st, wgu_bf, wd_bf, sem):
    def fetch(e, slot):
        return (pltpu.make_async_copy(wgu_hbm.at[e], wgu_st.at[slot], sem.at[slot, 0]),
                pltpu.make_async_copy(wd_hbm.at[e], wd_st.at[slot], sem.at[slot, 1]))

    @pl.when(b < nu_ref[0])
    def _():
        d_ff, d = wd_bf.shape

        @pl.when(first_ref[b] == 1)
        def _():
            slot = slot_ref[b]

            @pl.when(b == 0)
            def _():
                for cp in fetch(be_ref[b], slot):
                    cp.start()

            for cp in fetch(be_ref[b], slot):
                cp.wait()

            @pl.when(next_ref[b] >= 0)
            def _():
                for cp in fetch(next_ref[b], 1 - slot):
                    cp.start()

            wgu_bf[...] = wgu_st[slot].astype(BF16)
            wd_bf[...] = wd_st[slot].astype(BF16)

        x = _load_row_tiles(x_ref, MOE_BLOCK, d)
        h = jnp.dot(x.astype(BF16), wgu_bf[...], preferred_element_type=F32) + bgu_ref[...]
        g = jnp.minimum(h[:, :d_ff], SWIGLU_LIMIT)
        u = jnp.clip(h[:, d_ff:], -SWIGLU_LIMIT, SWIGLU_LIMIT)
        act = (u + 1.0) * (g * _sigmoid(SWIGLU_ALPHA * g))
        y = jnp.dot(act.astype(BF16), wd_bf[...], preferred_element_type=F32) + bd_ref[...]
        _store_row_tiles(y_ref, y)

    @pl.when(b >= nu_ref[0])
    def _():
        y_ref[...] = jnp.zeros(y_ref.shape, F32)


def _expert_call(xb_tiles, block_e, n_used, run_first, run_slot, run_next, wgu, bgu, wd, bd):
    n_exp, d, two_ff = wgu.shape
    d_ff = wd.shape[1]
    n_sub = EXPERT_BLOCKS_PER_STEP
    step_rows = n_sub * MOE_BLOCK * d // LANES
    n_steps = xb_tiles.shape[0] // step_rows
    assert n_steps * step_rows == xb_tiles.shape[0]
    any_spec = pl.BlockSpec(memory_space=pl.ANY)
    bias_specs, biases = [], []
    for h in range(n_sub):
        per_e = lambda s, be, *_, h=h: (be[s * n_sub + h], 0, 0)
        bias_specs += [pl.BlockSpec((None, 1, two_ff), per_e), pl.BlockSpec((None, 1, d), per_e)]
        biases += [bgu.reshape(n_exp, 1, two_ff), bd.reshape(n_exp, 1, d)]
    grid_spec = pltpu.PrefetchScalarGridSpec(
        num_scalar_prefetch=5,
        grid=(n_steps,),
        in_specs=[pl.BlockSpec((step_rows, LANES), lambda s, *_: (s, 0)), any_spec, any_spec] + bias_specs,
        out_specs=pl.BlockSpec((step_rows, LANES), lambda s, *_: (s, 0)),
        scratch_shapes=[pltpu.VMEM((2, d, two_ff), F32), pltpu.VMEM((2, d_ff, d), F32),
                        pltpu.VMEM((d, two_ff), BF16), pltpu.VMEM((d_ff, d), BF16),
                        pltpu.SemaphoreType.DMA((2, 2))],
    )
    return pl.pallas_call(
        _expert_kernel,
        grid_spec=grid_spec,
        out_shape=jax.ShapeDtypeStruct(xb_tiles.shape, F32),
        compiler_params=pltpu.CompilerParams(dimension_semantics=("arbitrary",), vmem_limit_bytes=VMEM_LIMIT),
        name="moe_experts",
    )(block_e, n_used, run_first, run_slot, run_next, xb_tiles, wgu, wd, *biases)


def _combine_kernel(xmid_ref, gate_ref, dest_ref, dest_next_ref, gfin_ref, yb_ref, outp_ref, outs_ref,
                    ybuf, ot_buf, sem, *, n_tiles_p):
    i = pl.program_id(0)
    n_tiles = pl.num_programs(0)
    tm, d = xmid_ref.shape
    pitch = d // LANES

    def start_tile(dref, slot):
        def start_rows(r, c):
            for k in range(TOP_K):
                src = dref[0, k * tm + r]
                pltpu.make_async_copy(yb_ref.at[pl.ds(pl.multiple_of(src * pitch, pitch), pitch)],
                                      ybuf.at[slot, k, pl.ds(pl.multiple_of(r * pitch, pitch), pitch)],
                                      sem.at[slot]).start(priority=k % 2)
            return c

        lax.fori_loop(0, tm, start_rows, 0, unroll=ROW_DMA_UNROLL)

    @pl.when(i == 0)
    def _():
        start_tile(dest_ref, 0)

    slot = i % 2

    @pl.when(i + 1 < n_tiles)
    def _():
        start_tile(dest_next_ref, 1 - slot)

    pltpu.make_async_copy(ybuf.at[slot], ybuf.at[slot], sem.at[slot]).wait()
    gate = gate_ref[...]
    y = xmid_ref[...]
    for k in range(TOP_K):
        y = y + gate[:, k:k + 1] * _load_row_tiles(ybuf.at[slot, k], tm, d)
    out = _rms(y, gfin_ref[...])

    @pl.when(i < n_tiles_p)
    def _():
        _store_seq_major(outp_ref, ot_buf, out)

    @pl.when(i >= n_tiles_p)
    def _():
        outs_ref[...] = out


def _combine_call(x_mid, gate, dest_tiles, g_final, yb, *, tm, n_p, n_tiles_p):
    n_tok, d = x_mid.shape
    n_tiles = n_tok // tm
    tt = tm // n_p
    dest_spec = lambda off: pl.BlockSpec((None, 1, tm * TOP_K),
                                         lambda i: (jnp.minimum(i + off, n_tiles - 1), 0, 0),
                                         memory_space=pltpu.SMEM)
    return pl.pallas_call(
        functools.partial(_combine_kernel, n_tiles_p=n_tiles_p),
        grid=(n_tiles,),
        in_specs=[pl.BlockSpec((tm, d), lambda i: (i, 0)),
                  pl.BlockSpec((tm, LANES), lambda i: (i, 0)),
                  dest_spec(0), dest_spec(1),
                  pl.BlockSpec((1, d), lambda i: (0, 0)),
                  pl.BlockSpec(memory_space=pl.ANY)],
        out_specs=(pl.BlockSpec((n_p, tt, d), lambda i: (0, jnp.minimum(i, n_tiles_p - 1), 0)),
                   pl.BlockSpec((tm, d), lambda i: (jnp.maximum(i - n_tiles_p, 0), 0))),
        out_shape=(jax.ShapeDtypeStruct((n_p, n_tiles_p * tt, d), F32),
                   jax.ShapeDtypeStruct(((n_tiles - n_tiles_p) * tm, d), F32)),
        scratch_shapes=[pltpu.VMEM((2, TOP_K, tm * d // LANES, LANES), F32),
                        pltpu.VMEM((d // LANES, tm, LANES), F32),
                        pltpu.SemaphoreType.DMA((2,))],
        compiler_params=pltpu.CompilerParams(dimension_semantics=("arbitrary",), vmem_limit_bytes=VMEM_LIMIT),
        name="moe_combine",
    )(x_mid, gate, dest_tiles, dest_tiles, g_final, yb)


def _to_time_major(x, seq_block):
    n_seqs, t, d = x.shape
    n_sb = n_seqs // seq_block
    return x.reshape(n_sb, seq_block, t, d).transpose(0, 2, 1, 3).reshape(n_sb * t, seq_block, d)


def _from_time_major(x, n_seqs, seq_block):
    d = x.shape[-1]
    n_sb = n_seqs // seq_block
    t = x.size // (n_seqs * d)
    return x.reshape(n_sb, t, seq_block, d).transpose(0, 2, 1, 3).reshape(n_seqs, t, d)


def _layer(xp, xs, state_a, state_b, p, norm_final_g):
    n_p, t_p, d = xp.shape
    n_s, t_s, _ = xs.shape
    n_exp = p["wgu"].shape[0]
    tm = TOKEN_TILE
    sb = tm // t_s
    n_tiles_p = n_p * t_p // tm

    (x_mid, xn2, route, gate, cnt, newa_p, newb_p, newa_s, newb_s) = _mixer_call(
        xp, _to_time_major(xs, sb), _to_time_major(state_a, sb), _to_time_major(state_b, sb),
        p, tt_p=tm // n_p, tt_s=t_s)
    n_tok = x_mid.shape[0]
    n_tiles = n_tok // tm

    counts = cnt[:, 0].astype(jnp.int32)
    padded = (counts + MOE_BLOCK - 1) // MOE_BLOCK * MOE_BLOCK
    pad_end = jnp.cumsum(padded)
    pad_start = pad_end - padded
    is_e = route[:TOP_K, :, None] == jnp.arange(n_exp, dtype=jnp.int32)
    dest = jnp.sum(jnp.where(is_e, pad_start, 0), axis=-1) + route[TOP_K:]
    dest_tiles = dest.reshape(TOP_K, n_tiles, tm).transpose(1, 0, 2).reshape(n_tiles, 1, TOP_K * tm)
    n_blocks = -(-(n_tok * TOP_K) // MOE_BLOCK) + n_exp
    n_blocks = -(-n_blocks // EXPERT_BLOCKS_PER_STEP) * EXPERT_BLOCKS_PER_STEP
    n_used = (pad_end[-1] // MOE_BLOCK).astype(jnp.int32)
    blk_start = jnp.minimum(jnp.arange(n_blocks, dtype=jnp.int32) * MOE_BLOCK, pad_end[-1] - 1)
    block_e = jnp.minimum(jnp.sum(blk_start[:, None] >= pad_end[None, :], axis=1), n_exp - 1).astype(jnp.int32)
    n_tail = n_exp + EXPERT_BLOCKS_PER_STEP - 1
    last_blocks = jnp.arange(n_blocks - n_tail, n_blocks, dtype=jnp.int32)
    zero_start = jnp.concatenate([jnp.where(padded > 0, pad_end - MOE_BLOCK, -1),
                                  jnp.where(last_blocks >= n_used, last_blocks * MOE_BLOCK, -1)]).astype(jnp.int32)

    xb = _dispatch_call(xn2, dest_tiles, zero_start, n_blocks * MOE_BLOCK, tm=tm, d=d)
    blk_ids = jnp.arange(n_blocks, dtype=jnp.int32)
    prev_e = jnp.concatenate([jnp.full((1,), -1, jnp.int32), block_e[:-1]])
    run_first = ((block_e != prev_e) & (blk_ids < n_used)).astype(jnp.int32)
    run_slot = ((jnp.cumsum(run_first) - 1) % 2).astype(jnp.int32)
    e_ids = jnp.arange(n_exp, dtype=jnp.int32)
    later = lax.cummin(jnp.where(padded > 0, e_ids, n_exp), axis=0, reverse=True)
    next_of = jnp.concatenate([later[1:], jnp.full((1,), n_exp, jnp.int32)])
    next_of = jnp.where(next_of >= n_exp, -1, next_of)
    run_next = jnp.sum(jnp.where(block_e[:, None] == e_ids[None, :], next_of[None, :], 0), axis=1).astype(jnp.int32)

    yb = _expert_call(xb, block_e, n_used.reshape(1), run_first, run_slot, run_next,
                      p["wgu"], p["bgu"], p["wd"], p["bd"])
    y_p, y_s = _combine_call(x_mid, gate, dest_tiles, norm_final_g.reshape(1, d), yb, tm=tm, n_p=n_p,
                             n_tiles_p=n_tiles_p)

    return (y_p, _from_time_major(y_s, n_s, sb),
            _from_time_major(newa_p, n_p, n_p), _from_time_major(newb_p, n_p, n_p),
            _from_time_major(newa_s, n_s, sb), _from_time_major(newb_s, n_s, sb))


def _prep_params(l, norm_mix_g, w_in, b_gates, conv_a_w, conv_a_b, w_a_out, conv_b_w, conv_b_b, ln_b_g,
                 ln_b_b, w_b_out, w_o, norm_ffn_g, w_router, b_router, w_gu, b_gu, w_down, b_down):
    row = lambda v: v.reshape(1, -1)
    taps = lambda w: jnp.broadcast_to(w[:, None, :], (w.shape[0], SUBLANES, w.shape[1]))
    wr_t = w_router[l].T
    wr_hi = wr_t.astype(BF16)
    return dict(
        gmix=row(norm_mix_g[l]), win=w_in[l].astype(BF16), bg=b_gates[l],
        caw=taps(conv_a_w[l]), cab=row(conv_a_b[l]), waout=w_a_out[l].astype(BF16),
        cbw=taps(conv_b_w[l]), cbb=row(conv_b_b[l]), lng=row(ln_b_g[l]), lnb=row(ln_b_b[l]),
        wbout=w_b_out[l].astype(BF16), wo=w_o[l].astype(BF16), gffn=row(norm_ffn_g[l]),
        wrh=wr_hi, wrl=(wr_t - wr_hi.astype(F32)).astype(BF16), br=b_router[l].reshape(-1, 1),
        wgu=w_gu[l], bgu=b_gu[l], wd=w_down[l], bd=b_down[l])


def kernel(x_prompt, x_sample, state_conv_a, state_conv_b, norm_mix_g, w_in, b_gates, conv_a_w, conv_a_b, w_a_out, conv_b_w, conv_b_b, ln_b_g, ln_b_b, w_b_out, w_o, norm_ffn_g, w_router, b_router, w_gu, b_gu, w_down, b_down, norm_final_g):
    depth = w_in.shape[0]
    assert depth == 1, "the final norm is fused into the last layer's combine call"
    p = _prep_params(0, norm_mix_g, w_in, b_gates, conv_a_w, conv_a_b, w_a_out, conv_b_w, conv_b_b, ln_b_g,
                     ln_b_b, w_b_out, w_o, norm_ffn_g, w_router, b_router, w_gu, b_gu, w_down, b_down)
    y_p, y_s, na_p, nb_p, na_s, nb_s = _layer(x_prompt, x_sample, state_conv_a[0], state_conv_b[0], p,
                                               norm_final_g)
    return (y_p, y_s, na_p[None], nb_p[None], na_s[None], nb_s[None])
```

```python
import functools

import jax
import jax.numpy as jnp
from jax import lax
from jax.experimental import pallas as pl
from jax.experimental.pallas import tpu as pltpu

EPS = 1e-5
SWIGLU_ALPHA = 1.702
SWIGLU_LIMIT = 7.0
TOP_K = 4
MOE_BLOCK = 256
TOKEN_TILE = 256
LANES = 128
SUBLANES = 8
VMEM_LIMIT = 60 * 1024 * 1024
CONV_OUT_BLOCK = 8
ROW_DMA_UNROLL = 8
CONV_LANES = 256
EXPERT_BLOCKS_PER_STEP = 4

F32 = jnp.float32
BF16 = jnp.bfloat16


def _sigmoid(v):
    return 1.0 / (1.0 + jnp.exp(-v))


def _store_row_tiles(ref, value):
    n, d = value.shape
    pitch = d // LANES
    for c in range(pitch):
        ref[pl.ds(c, n, stride=pitch), :] = value[:, c * LANES:(c + 1) * LANES]


def _load_row_tiles(ref, n, d):
    pitch = d // LANES
    return jnp.concatenate([ref[pl.ds(c, n, stride=pitch), :] for c in range(pitch)], axis=1)


def _rms(v, g):
    return v * lax.rsqrt(jnp.mean(v * v, axis=-1, keepdims=True) + EPS) * g


def _causal_conv(ext_ref, w_ref, bias, out_ref, n_out):
    width = w_ref.shape[0]
    n_seq, d = ext_ref.shape[1:]
    nb = CONV_OUT_BLOCK
    assert n_out % nb == 0 and n_seq % SUBLANES == 0 and d % CONV_LANES == 0

    def block(tb, carry):
        t0 = tb * nb
        for sg in range(n_seq // SUBLANES):
            rows = pl.ds(sg * SUBLANES, SUBLANES)
            for lc in range(d // CONV_LANES):
                lanes = pl.ds(lc * CONV_LANES, CONV_LANES)
                loaded = {}
                acc = [None] * nb
                for k in range(width):
                    wk = w_ref[k, :, lanes]
                    for j in range(nb):
                        if j + k not in loaded:
                            loaded[j + k] = ext_ref[t0 + j + k, rows, lanes]
                        term = wk * loaded[j + k]
                        acc[j] = term if acc[j] is None else acc[j] + term
                for j, a in enumerate(acc):
                    row0 = pl.multiple_of((t0 + j) * n_seq + sg * SUBLANES, SUBLANES)
                    out_ref[pl.ds(row0, SUBLANES), lanes] = a + bias[:, lc * CONV_LANES:(lc + 1) * CONV_LANES]
        return carry

    lax.fori_loop(0, n_out // nb, block, 0)
    return out_ref[...]


_MIXER_CONSTS = ("gmix", "win", "bg", "caw", "cab", "waout", "cbw", "cbb", "lng", "lnb", "wbout", "wo",
                 "gffn", "wrh", "wrl", "br")


def _time_major_rows(src_ref, buf, t0, tt):
    n_seq, _, d = src_ref.shape
    for s in range(n_seq):
        for c in range(d // LANES):
            buf[c, pl.ds(s, tt, stride=n_seq), :] = src_ref[s, t0:t0 + tt, c * LANES:(c + 1) * LANES]
    return jnp.concatenate([buf[c] for c in range(d // LANES)], axis=1)


def _store_seq_major(dst_ref, buf, value):
    n_seq, tt, d = dst_ref.shape
    for c in range(d // LANES):
        buf[c] = value[:, c * LANES:(c + 1) * LANES]
    for s in range(n_seq):
        for c in range(d // LANES):
            dst_ref[s, :, c * LANES:(c + 1) * LANES] = buf[c, pl.ds(s, tt, stride=n_seq), :]


def _mixer_tile(x, exta, extb, c, xmid_ref, xn2_ref, route_ref, gate_ref, cnt_ref, cnt_acc, conv_buf):
    rows, d = x.shape
    n_seq = exta.shape[1]
    tt = rows // n_seq
    w_a = c["caw"].shape[0]
    w_b = c["cbw"].shape[0]

    xn = _rms(x, c["gmix"][...]).astype(BF16)

    def proj(g):
        return jnp.dot(xn, c["win"][:, g * d:(g + 1) * d], preferred_element_type=F32)

    exta[w_a - 1:w_a - 1 + tt] = (proj(1) * proj(2)).reshape(tt, n_seq, d)
    conv_a = _causal_conv(exta, c["caw"], c["cab"][...], conv_buf.at[0], tt)
    y_a = jnp.dot((proj(0) * conv_a).astype(BF16), c["waout"][...], preferred_element_type=F32)
    extb[w_b - 1:w_b - 1 + tt] = (proj(3) * _sigmoid(proj(4))).reshape(tt, n_seq, d)
    conv_b = _causal_conv(extb, c["cbw"], c["cbb"][...], conv_buf.at[1], tt)
    mu = jnp.mean(conv_b, axis=-1, keepdims=True)
    cen = conv_b - mu
    ln = cen * lax.rsqrt(jnp.mean(cen * cen, axis=-1, keepdims=True) + EPS) * c["lng"][...] + c["lnb"][...]
    y_b = jnp.dot((ln * _sigmoid(ln)).astype(BF16), c["wbout"][...], preferred_element_type=F32)

    bg = c["bg"]
    merged = _sigmoid(proj(5) + bg[0:1, :]) * y_a + _sigmoid(proj(6) + bg[1:2, :]) * y_b
    x_mid = x + jnp.dot(merged.astype(BF16), c["wo"][...], preferred_element_type=F32)
    xmid_ref[...] = x_mid
    xn2 = _rms(x_mid, c["gffn"][...])
    _store_row_tiles(xn2_ref, xn2)

    n_exp = c["wrh"].shape[0]
    nt = (((1,), (1,)), ((), ()))
    x_hi = xn2.astype(BF16)
    x_lo = (xn2 - x_hi.astype(F32)).astype(BF16)
    logits = (lax.dot_general(c["wrh"][...], x_hi, nt, preferred_element_type=F32)
              + lax.dot_general(c["wrh"][...], x_lo, nt, preferred_element_type=F32)
              + lax.dot_general(c["wrl"][...], x_hi, nt, preferred_element_type=F32)) + c["br"][...]
    e_io = lax.broadcasted_iota(jnp.int32, (n_exp, rows), 0)
    work = logits
    top_v, top_i = [], []
    for _ in range(TOP_K):
        m = jnp.max(work, axis=0, keepdims=True)
        idx = jnp.min(jnp.where(work == m, e_io, n_exp), axis=0, keepdims=True)
        top_v.append(m)
        top_i.append(idx)
        work = jnp.where(e_io == idx, -jnp.inf, work)
    ex = [jnp.exp(v - top_v[0]) for v in top_v]
    den = ex[0] + ex[1] + ex[2] + ex[3]
    onehot = jnp.zeros((n_exp, rows), F32)
    for idx in top_i:
        onehot = onehot + (e_io == idx).astype(F32)
    r_io = lax.broadcasted_iota(jnp.int32, (rows, rows), 0)
    c_io = lax.broadcasted_iota(jnp.int32, (rows, rows), 1)
    before = (r_io < c_io).astype(BF16)
    cnt = cnt_acc[...]
    prefix = jnp.dot(onehot.astype(BF16), before, preferred_element_type=F32) + cnt[:, 0:1]
    pos = [jnp.sum(jnp.where(e_io == idx, prefix, 0.0), axis=0, keepdims=True) for idx in top_i]
    route_ref[...] = jnp.concatenate(top_i + [p.astype(jnp.int32) for p in pos], axis=0)
    gates = jnp.concatenate([e / den for e in ex] + [jnp.zeros((LANES - TOP_K, rows), F32)], axis=0)
    gate_ref[...] = jnp.transpose(gates)
    cnt = cnt + jnp.sum(onehot, axis=1, keepdims=True)
    cnt_acc[...] = cnt
    cnt_ref[...] = cnt


def _mixer_kernel(*refs, n_steps_p):
    n_c = len(_MIXER_CONSTS)
    xp_ref, xs_ref, hsa_ref, hsb_ref = refs[:4]
    c = dict(zip(_MIXER_CONSTS, refs[4:4 + n_c]))
    (xmid_ref, xn2_ref, route_ref, gate_ref, cnt_ref,
     newa_p_ref, newb_p_ref, newa_s_ref, newb_s_ref) = refs[4 + n_c:13 + n_c]
    exta_p, extb_p, exta_s, extb_s, cnt_acc, conv_buf, xt_buf, sem = refs[13 + n_c:]
    i = pl.program_id(0)
    tt_p = xp_ref.shape[1]
    tt_s = xs_ref.shape[0]
    w_a = c["caw"].shape[0]
    w_b = c["cbw"].shape[0]
    tile_args = (c, xmid_ref, xn2_ref, route_ref, gate_ref, cnt_ref, cnt_acc, conv_buf)

    @pl.when(i == 0)
    def _():
        exta_p[0:w_a - 1] = jnp.zeros((w_a - 1,) + exta_p.shape[1:], F32)
        extb_p[0:w_b - 1] = jnp.zeros((w_b - 1,) + extb_p.shape[1:], F32)
        cnt_acc[...] = jnp.zeros_like(cnt_acc)

    @pl.when(i < n_steps_p)
    def _():
        _mixer_tile(_time_major_rows(xp_ref, xt_buf, 0, tt_p), exta_p, extb_p, *tile_args)
        exta_p[0:w_a - 1] = exta_p[tt_p:tt_p + w_a - 1]
        extb_p[0:w_b - 1] = extb_p[tt_p:tt_p + w_b - 1]

        @pl.when(i == n_steps_p - 1)
        def _():
            cp_a = pltpu.make_async_copy(exta_p.at[pl.ds(0, w_a - 1)], newa_p_ref, sem.at[0])
            cp_b = pltpu.make_async_copy(extb_p.at[pl.ds(0, w_b - 1)], newb_p_ref, sem.at[1])
            cp_a.start()
            cp_b.start()
            cp_a.wait()
            cp_b.wait()

    @pl.when(i >= n_steps_p)
    def _():
        q = i - n_steps_p
        in_a = pltpu.make_async_copy(hsa_ref.at[pl.ds(q * (w_a - 1), w_a - 1)],
                                     exta_s.at[pl.ds(0, w_a - 1)], sem.at[0])
        in_b = pltpu.make_async_copy(hsb_ref.at[pl.ds(q * (w_b - 1), w_b - 1)],
                                     extb_s.at[pl.ds(0, w_b - 1)], sem.at[1])
        in_a.start()
        in_b.start()
        in_a.wait()
        in_b.wait()
        _mixer_tile(xs_ref[...].reshape(TOKEN_TILE, xs_ref.shape[-1]), exta_s, extb_s, *tile_args)
        out_a = pltpu.make_async_copy(exta_s.at[pl.ds(tt_s, w_a - 1)],
                                      newa_s_ref.at[pl.ds(q * (w_a - 1), w_a - 1)], sem.at[0])
        out_b = pltpu.make_async_copy(extb_s.at[pl.ds(tt_s, w_b - 1)],
                                      newb_s_ref.at[pl.ds(q * (w_b - 1), w_b - 1)], sem.at[1])
        out_a.start()
        out_b.start()
        out_a.wait()
        out_b.wait()


def _mixer_call(xp, xs_tm, hs_a, hs_b, params, *, tt_p, tt_s):
    n_p, t_p, d = xp.shape
    sb = xs_tm.shape[1]
    w_a = params["caw"].shape[0]
    w_b = params["cbw"].shape[0]
    n_exp = params["wrh"].shape[0]
    assert tt_p * n_p == tt_s * sb == TOKEN_TILE and n_exp % SUBLANES == 0
    rows = TOKEN_TILE
    n_steps_p = t_p // tt_p
    n_steps_s = xs_tm.shape[0] // tt_s
    assert n_steps_p * tt_p == t_p and n_steps_s * tt_s == xs_tm.shape[0]
    n_steps = n_steps_p + n_steps_s
    n_tok = n_steps * rows
    pitch = d // LANES
    consts = [params[n] for n in _MIXER_CONSTS]
    const_spec = lambda a: pl.BlockSpec(a.shape, lambda i, _nd=a.ndim: (0,) * _nd, pipeline_mode=pl.Buffered(1))
    any_spec = pl.BlockSpec(memory_space=pl.ANY)
    in_specs = [
        pl.BlockSpec((n_p, tt_p, d), lambda i: (0, jnp.minimum(i, n_steps_p - 1), 0)),
        pl.BlockSpec((tt_s, sb, d), lambda i: (jnp.maximum(i - n_steps_p, 0), 0, 0)),
        any_spec, any_spec,
    ] + [const_spec(a) for a in consts]
    out_shape = (
        jax.ShapeDtypeStruct((n_tok, d), F32),
        jax.ShapeDtypeStruct((n_tok * pitch, LANES), F32),
        jax.ShapeDtypeStruct((2 * TOP_K, n_tok), jnp.int32),
        jax.ShapeDtypeStruct((n_tok, LANES), F32),
        jax.ShapeDtypeStruct((n_exp, LANES), F32),
        jax.ShapeDtypeStruct((w_a - 1, n_p, d), F32),
        jax.ShapeDtypeStruct((w_b - 1, n_p, d), F32),
        jax.ShapeDtypeStruct(hs_a.shape, F32),
        jax.ShapeDtypeStruct(hs_b.shape, F32),
    )
    out_specs = (
        pl.BlockSpec((rows, d), lambda i: (i, 0)),
        pl.BlockSpec((rows * pitch, LANES), lambda i: (i, 0)),
        pl.BlockSpec((2 * TOP_K, rows), lambda i: (0, i)),
        pl.BlockSpec((rows, LANES), lambda i: (i, 0)),
        pl.BlockSpec((n_exp, LANES), lambda i: (0, 0)),
        any_spec, any_spec, any_spec, any_spec,
    )
    return pl.pallas_call(
        functools.partial(_mixer_kernel, n_steps_p=n_steps_p),
        grid=(n_steps,),
        in_specs=in_specs,
        out_specs=out_specs,
        out_shape=out_shape,
        scratch_shapes=[pltpu.VMEM((tt_p + w_a - 1, n_p, d), F32),
                        pltpu.VMEM((tt_p + w_b - 1, n_p, d), F32),
                        pltpu.VMEM((tt_s + w_a - 1, sb, d), F32),
                        pltpu.VMEM((tt_s + w_b - 1, sb, d), F32),
                        pltpu.VMEM((n_exp, LANES), F32),
                        pltpu.VMEM((2, rows, d), F32),
                        pltpu.VMEM((pitch, rows, LANES), F32),
                        pltpu.SemaphoreType.DMA((2,))],
        compiler_params=pltpu.CompilerParams(dimension_semantics=("arbitrary",), vmem_limit_bytes=VMEM_LIMIT),
        name="mixer_router",
    )(xp, xs_tm, hs_a, hs_b, *consts)


def _dispatch_kernel(zero_ref, x_hbm, dest_ref, xb_ref, xbuf, zero_buf, in_sem, out_sem, zsem, *, pitch):
    i = pl.program_id(0)
    n_tiles = pl.num_programs(0)
    n_buf, tile_rows = xbuf.shape[:2]
    tm = tile_rows // pitch
    n_zero = zero_ref.shape[0]
    blk_rows = zero_buf.shape[0]
    n_all = tm * TOP_K * pitch

    def fetch(t):
        start = t * tile_rows if isinstance(t, int) else pl.multiple_of(t * tile_rows, tile_rows)
        return pltpu.make_async_copy(x_hbm.at[pl.ds(start, tile_rows)], xbuf.at[t % n_buf], in_sem.at[t % n_buf])

    def wait_scatters(t):
        pltpu.make_async_copy(xb_ref.at[pl.ds(0, n_all)], xb_ref.at[pl.ds(0, n_all)], out_sem.at[t % 2]).wait()

    @pl.when(i == 0)
    def _():
        fetch(0).start()
        zero_buf[...] = jnp.zeros_like(zero_buf)

        def zcopy(e):
            start = pl.multiple_of(jnp.maximum(zero_ref[e], 0) * pitch, blk_rows)
            return pltpu.make_async_copy(zero_buf, xb_ref.at[pl.ds(start, blk_rows)], zsem)

        def start(e, c):
            @pl.when(zero_ref[e] >= 0)
            def _():
                zcopy(e).start()
            return c

        def wait(e, c):
            @pl.when(zero_ref[e] >= 0)
            def _():
                zcopy(e).wait()
            return c

        lax.fori_loop(0, n_zero, start, 0)
        lax.fori_loop(0, n_zero, wait, 0)

    @pl.when(i + 1 < n_tiles)
    def _():
        fetch(i + 1).start()

    fetch(i).wait()
    slot = i % n_buf

    def start_rows(r, c):
        src = xbuf.at[slot, pl.ds(pl.multiple_of(r * pitch, pitch), pitch)]
        for k in range(TOP_K):
            dst = dest_ref[0, k * tm + r]
            pltpu.make_async_copy(src, xb_ref.at[pl.ds(pl.multiple_of(dst * pitch, pitch), pitch)],
                                  out_sem.at[i % 2]).start(priority=k % 2)
        return c

    lax.fori_loop(0, tm, start_rows, 0, unroll=ROW_DMA_UNROLL)

    @pl.when(i > 0)
    def _():
        wait_scatters(i - 1)

    @pl.when(i == n_tiles - 1)
    def _():
        wait_scatters(i)


def _dispatch_call(xn2_tiles, dest_tiles, zero_start, n_rows, *, tm, d):
    pitch = d // LANES
    n_tiles = xn2_tiles.shape[0] // (tm * pitch)
    grid_spec = pltpu.PrefetchScalarGridSpec(
        num_scalar_prefetch=1,
        grid=(n_tiles,),
        in_specs=[pl.BlockSpec(memory_space=pl.ANY),
                  pl.BlockSpec((None, 1, tm * TOP_K), lambda i, z: (i, 0, 0), memory_space=pltpu.SMEM)],
        out_specs=pl.BlockSpec(memory_space=pl.ANY),
        scratch_shapes=[pltpu.VMEM((3, tm * pitch, LANES), F32),
                        pltpu.VMEM((MOE_BLOCK * pitch, LANES), F32),
                        pltpu.SemaphoreType.DMA((3,)), pltpu.SemaphoreType.DMA((2,)),
                        pltpu.SemaphoreType.DMA],
    )
    return pl.pallas_call(
        functools.partial(_dispatch_kernel, pitch=pitch),
        grid_spec=grid_spec,
        out_shape=jax.ShapeDtypeStruct((n_rows * pitch, LANES), F32),
        compiler_params=pltpu.CompilerParams(dimension_semantics=("arbitrary",)),
        name="moe_dispatch",
    )(zero_start, xn2_tiles, dest_tiles)


def _expert_kernel(be_ref, nu_ref, first_ref, slot_ref, next_ref, x_ref, wgu_hbm, wd_hbm, *rest):
    n_sub = EXPERT_BLOCKS_PER_STEP
    bias_refs, (y_ref, wgu_st, wd_st, wgu_bf, wd_bf, sem) = rest[:2 * n_sub], rest[2 * n_sub:]
    blk_rows = x_ref.shape[0] // n_sub
    for h in range(n_sub):
        rows = pl.ds(h * blk_rows, blk_rows)
        _expert_block(pl.program_id(0) * n_sub + h, be_ref, nu_ref, first_ref, slot_ref, next_ref,
                      x_ref.at[rows], wgu_hbm, bias_refs[2 * h], wd_hbm, bias_refs[2 * h + 1], y_ref.at[rows],
                      wgu_st, wd_st, wgu_bf, wd_bf, sem)


def _expert_block(b, be_ref, nu_ref, first_ref, slot_ref, next_ref, x_ref, wgu_hbm, bgu_ref, wd_hbm, bd_ref, y_ref,
                  wgu_st, wd_st, wgu_bf, wd_bf, sem):
    def fetch(e, slot):
        return (pltpu.make_async_copy(wgu_hbm.at[e], wgu_st.at[slot], sem.at[slot, 0]),
                pltpu.make_async_copy(wd_hbm.at[e], wd_st.at[slot], sem.at[slot, 1]))

    @pl.when(b < nu_ref[0])
    def _():
        d_ff, d = wd_bf.shape

        @pl.when(first_ref[b] == 1)
        def _():
            slot = slot_ref[b]

            @pl.when(b == 0)
            def _():
                for cp in fetch(be_ref[b], slot):
                    cp.start()

            for cp in fetch(be_ref[b], slot):
                cp.wait()

            @pl.when(next_ref[b] >= 0)
            def _():
                for cp in fetch(next_ref[b], 1 - slot):
                    cp.start()

            wgu_bf[...] = wgu_st[slot].astype(BF16)
            wd_bf[...] = wd_st[slot].astype(BF16)

        x = _load_row_tiles(x_ref, MOE_BLOCK, d)
        h = jnp.dot(x.astype(BF16), wgu_bf[...], preferred_element_type=F32) + bgu_ref[...]
        g = jnp.minimum(h[:, :d_ff], SWIGLU_LIMIT)
        u = jnp.clip(h[:, d_ff:], -SWIGLU_LIMIT, SWIGLU_LIMIT)
        act = (u + 1.0) * (g * _sigmoid(SWIGLU_ALPHA * g))
        y = jnp.dot(act.astype(BF16), wd_bf[...], preferred_element_type=F32) + bd_ref[...]
        _store_row_tiles(y_ref, y)

    @pl.when(b >= nu_ref[0])
    def _():
        y_ref[...] = jnp.zeros(y_ref.shape, F32)


def _expert_call(xb_tiles, block_e, n_used, run_first, run_slot, run_next, wgu, bgu, wd, bd):
    n_exp, d, two_ff = wgu.shape
    d_ff = wd.shape[1]
    n_sub = EXPERT_BLOCKS_PER_STEP
    step_rows = n_sub * MOE_BLOCK * d // LANES
    n_steps = xb_tiles.shape[0] // step_rows
    assert n_steps * step_rows == xb_tiles.shape[0]
    any_spec = pl.BlockSpec(memory_space=pl.ANY)
    bias_specs, biases = [], []
    for h in range(n_sub):
        per_e = lambda s, be, *_, h=h: (be[s * n_sub + h], 0, 0)
        bias_specs += [pl.BlockSpec((None, 1, two_ff), per_e), pl.BlockSpec((None, 1, d), per_e)]
        biases += [bgu.reshape(n_exp, 1, two_ff), bd.reshape(n_exp, 1, d)]
    grid_spec = pltpu.PrefetchScalarGridSpec(
        num_scalar_prefetch=5,
        grid=(n_steps,),
        in_specs=[pl.BlockSpec((step_rows, LANES), lambda s, *_: (s, 0)), any_spec, any_spec] + bias_specs,
        out_specs=pl.BlockSpec((step_rows, LANES), lambda s, *_: (s, 0)),
        scratch_shapes=[pltpu.VMEM((2, d, two_ff), F32), pltpu.VMEM((2, d_ff, d), F32),
                        pltpu.VMEM((d, two_ff), BF16), pltpu.VMEM((d_ff, d), BF16),
                        pltpu.SemaphoreType.DMA((2, 2))],
    )
    return pl.pallas_call(
        _expert_kernel,
        grid_spec=grid_spec,
        out_shape=jax.ShapeDtypeStruct(xb_tiles.shape, F32),
        compiler_params=pltpu.CompilerParams(dimension_semantics=("arbitrary",), vmem_limit_bytes=VMEM_LIMIT),
        name="moe_experts",
    )(block_e, n_used, run_first, run_slot, run_next, xb_tiles, wgu, wd, *biases)


def _combine_kernel(xmid_ref, gate_ref, dest_ref, dest_next_ref, gfin_ref, yb_ref, outp_ref, outs_ref,
                    ybuf, ot_buf, sem, *, n_tiles_p):
    i = pl.program_id(0)
    n_tiles = pl.num_programs(0)
    tm, d = xmid_ref.shape
    pitch = d // LANES

    def start_tile(dref, slot):
        def start_rows(r, c):
            for k in range(TOP_K):
                src = dref[0, k * tm + r]
                pltpu.make_async_copy(yb_ref.at[pl.ds(pl.multiple_of(src * pitch, pitch), pitch)],
                                      ybuf.at[slot, k, pl.ds(pl.multiple_of(r * pitch, pitch), pitch)],
                                      sem.at[slot]).start(priority=k % 2)
            return c

        lax.fori_loop(0, tm, start_rows, 0, unroll=ROW_DMA_UNROLL)

    @pl.when(i == 0)
    def _():
        start_tile(dest_ref, 0)

    slot = i % 2

    @pl.when(i + 1 < n_tiles)
    def _():
        start_tile(dest_next_ref, 1 - slot)

    pltpu.make_async_copy(ybuf.at[slot], ybuf.at[slot], sem.at[slot]).wait()
    gate = gate_ref[...]
    y = xmid_ref[...]
    for k in range(TOP_K):
        y = y + gate[:, k:k + 1] * _load_row_tiles(ybuf.at[slot, k], tm, d)
    out = _rms(y, gfin_ref[...])

    @pl.when(i < n_tiles_p)
    def _():
        _store_seq_major(outp_ref, ot_buf, out)

    @pl.when(i >= n_tiles_p)
    def _():
        outs_ref[...] = out


def _combine_call(x_mid, gate, dest_tiles, g_final, yb, *, tm, n_p, n_tiles_p):
    n_tok, d = x_mid.shape
    n_tiles = n_tok // tm
    tt = tm // n_p
    dest_spec = lambda off: pl.BlockSpec((None, 1, tm * TOP_K),
                                         lambda i: (jnp.minimum(i + off, n_tiles - 1), 0, 0),
                                         memory_space=pltpu.SMEM)
    return pl.pallas_call(
        functools.partial(_combine_kernel, n_tiles_p=n_tiles_p),
        grid=(n_tiles,),
        in_specs=[pl.BlockSpec((tm, d), lambda i: (i, 0)),
                  pl.BlockSpec((tm, LANES), lambda i: (i, 0)),
                  dest_spec(0), dest_spec(1),
                  pl.BlockSpec((1, d), lambda i: (0, 0)),
                  pl.BlockSpec(memory_space=pl.ANY)],
        out_specs=(pl.BlockSpec((n_p, tt, d), lambda i: (0, jnp.minimum(i, n_tiles_p - 1), 0)),
                   pl.BlockSpec((tm, d), lambda i: (jnp.maximum(i - n_tiles_p, 0), 0))),
        out_shape=(jax.ShapeDtypeStruct((n_p, n_tiles_p * tt, d), F32),
                   jax.ShapeDtypeStruct(((n_tiles - n_tiles_p) * tm, d), F32)),
        scratch_shapes=[pltpu.VMEM((2, TOP_K, tm * d // LANES, LANES), F32),
                        pltpu.VMEM((d // LANES, tm, LANES), F32),
                        pltpu.SemaphoreType.DMA((2,))],
        compiler_params=pltpu.CompilerParams(dimension_semantics=("arbitrary",), vmem_limit_bytes=VMEM_LIMIT),
        name="moe_combine",
    )(x_mid, gate, dest_tiles, dest_tiles, g_final, yb)


def _to_time_major(x, seq_block):
    n_seqs, t, d = x.shape
    n_sb = n_seqs // seq_block
    return x.reshape(n_sb, seq_block, t, d).transpose(0, 2, 1, 3).reshape(n_sb * t, seq_block, d)


def _from_time_major(x, n_seqs, seq_block):
    d = x.shape[-1]
    n_sb = n_seqs // seq_block
    t = x.size // (n_seqs * d)
    return x.reshape(n_sb, t, seq_block, d).transpose(0, 2, 1, 3).reshape(n_seqs, t, d)


def _layer(xp, xs, state_a, state_b, p, norm_final_g):
    n_p, t_p, d = xp.shape
    n_s, t_s, _ = xs.shape
    n_exp = p["wgu"].shape[0]
    tm = TOKEN_TILE
    sb = tm // t_s
    n_tiles_p = n_p * t_p // tm

    (x_mid, xn2, route, gate, cnt, newa_p, newb_p, newa_s, newb_s) = _mixer_call(
        xp, _to_time_major(xs, sb), _to_time_major(state_a, sb), _to_time_major(state_b, sb),
        p, tt_p=tm // n_p, tt_s=t_s)
    n_tok = x_mid.shape[0]
    n_tiles = n_tok // tm

    counts = cnt[:, 0].astype(jnp.int32)
    padded = (counts + MOE_BLOCK - 1) // MOE_BLOCK * MOE_BLOCK
    pad_end = jnp.cumsum(padded)
    pad_start = pad_end - padded
    is_e = route[:TOP_K, :, None] == jnp.arange(n_exp, dtype=jnp.int32)
    dest = jnp.sum(jnp.where(is_e, pad_start, 0), axis=-1) + route[TOP_K:]
    dest_tiles = dest.reshape(TOP_K, n_tiles, tm).transpose(1, 0, 2).reshape(n_tiles, 1, TOP_K * tm)
    n_blocks = -(-(n_tok * TOP_K) // MOE_BLOCK) + n_exp
    n_blocks = -(-n_blocks // EXPERT_BLOCKS_PER_STEP) * EXPERT_BLOCKS_PER_STEP
    n_used = (pad_end[-1] // MOE_BLOCK).astype(jnp.int32)
    blk_start = jnp.minimum(jnp.arange(n_blocks, dtype=jnp.int32) * MOE_BLOCK, pad_end[-1] - 1)
    block_e = jnp.minimum(jnp.sum(blk_start[:, None] >= pad_end[None, :], axis=1), n_exp - 1).astype(jnp.int32)
    n_tail = n_exp + EXPERT_BLOCKS_PER_STEP - 1
    last_blocks = jnp.arange(n_blocks - n_tail, n_blocks, dtype=jnp.int32)
    zero_start = jnp.concatenate([jnp.where(padded > 0, pad_end - MOE_BLOCK, -1),
                                  jnp.where(last_blocks >= n_used, last_blocks * MOE_BLOCK, -1)]).astype(jnp.int32)

    xb = _dispatch_call(xn2, dest_tiles, zero_start, n_blocks * MOE_BLOCK, tm=tm, d=d)
    blk_ids = jnp.arange(n_blocks, dtype=jnp.int32)
    prev_e = jnp.concatenate([jnp.full((1,), -1, jnp.int32), block_e[:-1]])
    run_first = ((block_e != prev_e) & (blk_ids < n_used)).astype(jnp.int32)
    run_slot = ((jnp.cumsum(run_first) - 1) % 2).astype(jnp.int32)
    e_ids = jnp.arange(n_exp, dtype=jnp.int32)
    later = lax.cummin(jnp.where(padded > 0, e_ids, n_exp), axis=0, reverse=True)
    next_of = jnp.concatenate([later[1:], jnp.full((1,), n_exp, jnp.int32)])
    next_of = jnp.where(next_of >= n_exp, -1, next_of)
    run_next = jnp.sum(jnp.where(block_e[:, None] == e_ids[None, :], next_of[None, :], 0), axis=1).astype(jnp.int32)

    yb = _expert_call(xb, block_e, n_used.reshape(1), run_first, run_slot, run_next,
                      p["wgu"], p["bgu"], p["wd"], p["bd"])
    y_p, y_s = _combine_call(x_mid, gate, dest_tiles, norm_final_g.reshape(1, d), yb, tm=tm, n_p=n_p,
                             n_tiles_p=n_tiles_p)

    return (y_p, _from_time_major(y_s, n_s, sb),
            _from_time_major(newa_p, n_p, n_p), _from_time_major(newb_p, n_p, n_p),
            _from_time_major(newa_s, n_s, sb), _from_time_major(newb_s, n_s, sb))


def _prep_params(l, norm_mix_g, w_in, b_gates, conv_a_w, conv_a_b, w_a_out, conv_b_w, conv_b_b, ln_b_g,
                 ln_b_b, w_b_out, w_o, norm_ffn_g, w_router, b_router, w_gu, b_gu, w_down, b_down):
    row = lambda v: v.reshape(1, -1)
    taps = lambda w: jnp.broadcast_to(w[:, None, :], (w.shape[0], SUBLANES, w.shape[1]))
    wr_t = w_router[l].T
    wr_hi = wr_t.astype(BF16)
    return dict(
        gmix=row(norm_mix_g[l]), win=w_in[l].astype(BF16), bg=b_gates[l],
        caw=taps(conv_a_w[l]), cab=row(conv_a_b[l]), waout=w_a_out[l].astype(BF16),
        cbw=taps(conv_b_w[l]), cbb=row(conv_b_b[l]), lng=row(ln_b_g[l]), lnb=row(ln_b_b[l]),
        wbout=w_b_out[l].astype(BF16), wo=w_o[l].astype(BF16), gffn=row(norm_ffn_g[l]),
        wrh=wr_hi, wrl=(wr_t - wr_hi.astype(F32)).astype(BF16), br=b_router[l].reshape(-1, 1),
        wgu=w_gu[l], bgu=b_gu[l], wd=w_down[l], bd=b_down[l])


def kernel(x_prompt, x_sample, state_conv_a, state_conv_b, norm_mix_g, w_in, b_gates, conv_a_w, conv_a_b, w_a_out, conv_b_w, conv_b_b, ln_b_g, ln_b_b, w_b_out, w_o, norm_ffn_g, w_router, b_router, w_gu, b_gu, w_down, b_down, norm_final_g):
    depth = w_in.shape[0]
    assert depth == 1, "the final norm is fused into the last layer's combine call"
    p = _prep_params(0, norm_mix_g, w_in, b_gates, conv_a_w, conv_a_b, w_a_out, conv_b_w, conv_b_b, ln_b_g,
                     ln_b_b, w_b_out, w_o, norm_ffn_g, w_router, b_router, w_gu, b_gu, w_down, b_down)
    y_p, y_s, na_p, nb_p, na_s, nb_s = _layer(x_prompt, x_sample, state_conv_a[0], state_conv_b[0], p,
                                               norm_final_g)
    return (y_p, y_s, na_p[None], nb_p[None], na_s[None], nb_s[None])
```

```python
import functools

import jax
import jax.numpy as jnp
from jax import lax
from jax.experimental import pallas as pl
from jax.experimental.pallas import tpu as pltpu

EPS = 1e-5
SWIGLU_ALPHA = 1.702
SWIGLU_LIMIT = 7.0
TOP_K = 4
MOE_BLOCK = 256
TOKEN_TILE = 256
LANES = 128
SUBLANES = 8
VMEM_LIMIT = 60 * 1024 * 1024
CONV_OUT_BLOCK = 8
ROW_DMA_UNROLL = 8
CONV_LANES = 256
TWO_PHASE_MIN_TAPS = 8
EXPERT_BLOCKS_PER_STEP = 4

F32 = jnp.float32
BF16 = jnp.bfloat16


def _sigmoid(v):
    return 1.0 / (1.0 + jnp.exp(-v))


def _store_row_tiles(ref, value):
    n, d = value.shape
    pitch = d // LANES
    for c in range(pitch):
        ref[pl.ds(c, n, stride=pitch), :] = value[:, c * LANES:(c + 1) * LANES]


def _load_row_tiles(ref, n, d):
    pitch = d // LANES
    return jnp.concatenate([ref[pl.ds(c, n, stride=pitch), :] for c in range(pitch)], axis=1)


def _rms(v, g):
    return v * lax.rsqrt(jnp.mean(v * v, axis=-1, keepdims=True) + EPS) * g


def _causal_conv(ext_ref, w_ref, bias, out_ref, n_out):
    width = w_ref.shape[0]
    n_seq, d = ext_ref.shape[1:]
    nb = CONV_OUT_BLOCK
    assert n_out % nb == 0 and n_seq % SUBLANES == 0 and d % CONV_LANES == 0

    mac = lambda acc, term: term if acc is None else acc + term

    def direct(e, w):
        acc = [None] * nb
        for k in range(width):
            wk = w(k)
            for j in range(nb):
                acc[j] = mac(acc[j], wk * e(j + k))
        return acc

    def two_phase(e, w):
        half, n0, n1 = nb // 2, (width + 1) // 2, width // 2
        s_cache = {}

        def s(i):
            if i not in s_cache:
                s_cache[i] = e(2 * i + 1) + e(2 * i + 2)
            return s_cache[i]

        a, b, c = [None] * (half + 1), [None] * half, [None] * half
        for j in range(n0):
            h0 = w(2 * j)
            h1 = w(2 * j + 1) if j < n1 else None
            hs = h0 if h1 is None else h0 + h1
            for m in range(half + 1):
                a[m] = mac(a[m], h0 * e(2 * (m + j)))
            for m in range(half):
                if h1 is not None:
                    b[m] = mac(b[m], h1 * e(2 * (m + j) + 1))
                c[m] = mac(c[m], hs * s(m + j))
        out = []
        for m in range(half):
            out += [a[m] + b[m], c[m] - a[m + 1] - b[m]]
        return out

    def block(tb, carry):
        t0 = tb * nb
        for sg in range(n_seq // SUBLANES):
            rows = pl.ds(sg * SUBLANES, SUBLANES)
            for lc in range(d // CONV_LANES):
                lanes = pl.ds(lc * CONV_LANES, CONV_LANES)
                loaded, taps = {}, {}

                def e(i):
                    if i not in loaded:
                        loaded[i] = ext_ref[t0 + i, rows, lanes]
                    return loaded[i]

                def w(k):
                    if k not in taps:
                        taps[k] = w_ref[k, :, lanes]
                    return taps[k]

                acc = (two_phase if width >= TWO_PHASE_MIN_TAPS else direct)(e, w)
                for j, a in enumerate(acc):
                    row0 = pl.multiple_of((t0 + j) * n_seq + sg * SUBLANES, SUBLANES)
                    out_ref[pl.ds(row0, SUBLANES), lanes] = a + bias[:, lc * CONV_LANES:(lc + 1) * CONV_LANES]
        return carry

    lax.fori_loop(0, n_out // nb, block, 0)
    return out_ref[...]


_MIXER_CONSTS = ("gmix", "win", "bg", "caw", "cab", "waout", "cbw", "cbb", "lng", "lnb", "wbout", "wo",
                 "gffn", "wrh", "wrl", "br")


def _time_major_rows(src_ref, buf, t0, tt):
    n_seq, _, d = src_ref.shape
    for s in range(n_seq):
        for c in range(d // LANES):
            buf[c, pl.ds(s, tt, stride=n_seq), :] = src_ref[s, t0:t0 + tt, c * LANES:(c + 1) * LANES]
    return jnp.concatenate([buf[c] for c in range(d // LANES)], axis=1)


def _store_seq_major(dst_ref, buf, value):
    n_seq, tt, d = dst_ref.shape
    for c in range(d // LANES):
        buf[c] = value[:, c * LANES:(c + 1) * LANES]
    for s in range(n_seq):
        for c in range(d // LANES):
            dst_ref[s, :, c * LANES:(c + 1) * LANES] = buf[c, pl.ds(s, tt, stride=n_seq), :]


def _mixer_tile(x, exta, extb, c, xmid_ref, xn2_ref, route_ref, gate_ref, cnt_ref, cnt_acc, conv_buf):
    rows, d = x.shape
    n_seq = exta.shape[1]
    tt = rows // n_seq
    w_a = c["caw"].shape[0]
    w_b = c["cbw"].shape[0]

    xn = _rms(x, c["gmix"][...]).astype(BF16)

    def proj(g):
        return jnp.dot(xn, c["win"][:, g * d:(g + 1) * d], preferred_element_type=F32)

    exta[w_a - 1:w_a - 1 + tt] = (proj(1) * proj(2)).reshape(tt, n_seq, d)
    conv_a = _causal_conv(exta, c["caw"], c["cab"][...], conv_buf.at[0], tt)
    y_a = jnp.dot((proj(0) * conv_a).astype(BF16), c["waout"][...], preferred_element_type=F32)
    extb[w_b - 1:w_b - 1 + tt] = (proj(3) * _sigmoid(proj(4))).reshape(tt, n_seq, d)
    conv_b = _causal_conv(extb, c["cbw"], c["cbb"][...], conv_buf.at[1], tt)
    mu = jnp.mean(conv_b, axis=-1, keepdims=True)
    cen = conv_b - mu
    ln = cen * lax.rsqrt(jnp.mean(cen * cen, axis=-1, keepdims=True) + EPS) * c["lng"][...] + c["lnb"][...]
    y_b = jnp.dot((ln * _sigmoid(ln)).astype(BF16), c["wbout"][...], preferred_element_type=F32)

    bg = c["bg"]
    merged = _sigmoid(proj(5) + bg[0:1, :]) * y_a + _sigmoid(proj(6) + bg[1:2, :]) * y_b
    x_mid = x + jnp.dot(merged.astype(BF16), c["wo"][...], preferred_element_type=F32)
    xmid_ref[...] = x_mid
    xn2 = _rms(x_mid, c["gffn"][...])
    _store_row_tiles(xn2_ref, xn2)

    n_exp = c["wrh"].shape[0]
    nt = (((1,), (1,)), ((), ()))
    x_hi = xn2.astype(BF16)
    x_lo = (xn2 - x_hi.astype(F32)).astype(BF16)
    logits = (lax.dot_general(c["wrh"][...], x_hi, nt, preferred_element_type=F32)
              + lax.dot_general(c["wrh"][...], x_lo, nt, preferred_element_type=F32)
              + lax.dot_general(c["wrl"][...], x_hi, nt, preferred_element_type=F32)) + c["br"][...]
    e_io = lax.broadcasted_iota(jnp.int32, (n_exp, rows), 0)
    work = logits
    top_v, top_i = [], []
    for _ in range(TOP_K):
        m = jnp.max(work, axis=0, keepdims=True)
        idx = jnp.min(jnp.where(work == m, e_io, n_exp), axis=0, keepdims=True)
        top_v.append(m)
        top_i.append(idx)
        work = jnp.where(e_io == idx, -jnp.inf, work)
    ex = [jnp.exp(v - top_v[0]) for v in top_v]
    den = ex[0] + ex[1] + ex[2] + ex[3]
    onehot = jnp.zeros((n_exp, rows), F32)
    for idx in top_i:
        onehot = onehot + (e_io == idx).astype(F32)
    r_io = lax.broadcasted_iota(jnp.int32, (rows, rows), 0)
    c_io = lax.broadcasted_iota(jnp.int32, (rows, rows), 1)
    before = (r_io < c_io).astype(BF16)
    cnt = cnt_acc[...]
    prefix = jnp.dot(onehot.astype(BF16), before, preferred_element_type=F32) + cnt[:, 0:1]
    pos = [jnp.sum(jnp.where(e_io == idx, prefix, 0.0), axis=0, keepdims=True) for idx in top_i]
    route_ref[...] = jnp.concatenate(top_i + [p.astype(jnp.int32) for p in pos], axis=0)
    gates = jnp.concatenate([e / den for e in ex] + [jnp.zeros((LANES - TOP_K, rows), F32)], axis=0)
    gate_ref[...] = jnp.transpose(gates)
    cnt = cnt + jnp.sum(onehot, axis=1, keepdims=True)
    cnt_acc[...] = cnt
    cnt_ref[...] = cnt


def _mixer_kernel(*refs, n_steps_p):
    n_c = len(_MIXER_CONSTS)
    xp_ref, xs_ref, hsa_ref, hsb_ref = refs[:4]
    c = dict(zip(_MIXER_CONSTS, refs[4:4 + n_c]))
    (xmid_ref, xn2_ref, route_ref, gate_ref, cnt_ref,
     newa_p_ref, newb_p_ref, newa_s_ref, newb_s_ref) = refs[4 + n_c:13 + n_c]
    exta_p, extb_p, exta_s, extb_s, cnt_acc, conv_buf, xt_buf, sem = refs[13 + n_c:]
    i = pl.program_id(0)
    tt_p = xp_ref.shape[1]
    tt_s = xs_ref.shape[0]
    w_a = c["caw"].shape[0]
    w_b = c["cbw"].shape[0]
    tile_args = (c, xmid_ref, xn2_ref, route_ref, gate_ref, cnt_ref, cnt_acc, conv_buf)

    @pl.when(i == 0)
    def _():
        exta_p[0:w_a - 1] = jnp.zeros((w_a - 1,) + exta_p.shape[1:], F32)
        extb_p[0:w_b - 1] = jnp.zeros((w_b - 1,) + extb_p.shape[1:], F32)
        for ext in (exta_p, extb_p, exta_s, extb_s):
            ext[ext.shape[0] - 1] = jnp.zeros(ext.shape[1:], F32)
        cnt_acc[...] = jnp.zeros_like(cnt_acc)

    @pl.when(i < n_steps_p)
    def _():
        _mixer_tile(_time_major_rows(xp_ref, xt_buf, 0, tt_p), exta_p, extb_p, *tile_args)
        exta_p[0:w_a - 1] = exta_p[tt_p:tt_p + w_a - 1]
        extb_p[0:w_b - 1] = extb_p[tt_p:tt_p + w_b - 1]

        @pl.when(i == n_steps_p - 1)
        def _():
            cp_a = pltpu.make_async_copy(exta_p.at[pl.ds(0, w_a - 1)], newa_p_ref, sem.at[0])
            cp_b = pltpu.make_async_copy(extb_p.at[pl.ds(0, w_b - 1)], newb_p_ref, sem.at[1])
            cp_a.start()
            cp_b.start()
            cp_a.wait()
            cp_b.wait()

    @pl.when(i >= n_steps_p)
    def _():
        q = i - n_steps_p
        in_a = pltpu.make_async_copy(hsa_ref.at[pl.ds(q * (w_a - 1), w_a - 1)],
                                     exta_s.at[pl.ds(0, w_a - 1)], sem.at[0])
        in_b = pltpu.make_async_copy(hsb_ref.at[pl.ds(q * (w_b - 1), w_b - 1)],
                                     extb_s.at[pl.ds(0, w_b - 1)], sem.at[1])
        in_a.start()
        in_b.start()
        in_a.wait()
        in_b.wait()
        _mixer_tile(xs_ref[...].reshape(TOKEN_TILE, xs_ref.shape[-1]), exta_s, extb_s, *tile_args)
        out_a = pltpu.make_async_copy(exta_s.at[pl.ds(tt_s, w_a - 1)],
                                      newa_s_ref.at[pl.ds(q * (w_a - 1), w_a - 1)], sem.at[0])
        out_b = pltpu.make_async_copy(extb_s.at[pl.ds(tt_s, w_b - 1)],
                                      newb_s_ref.at[pl.ds(q * (w_b - 1), w_b - 1)], sem.at[1])
        out_a.start()
        out_b.start()
        out_a.wait()
        out_b.wait()


def _mixer_call(xp, xs_tm, hs_a, hs_b, params, *, tt_p, tt_s):
    n_p, t_p, d = xp.shape
    sb = xs_tm.shape[1]
    w_a = params["caw"].shape[0]
    w_b = params["cbw"].shape[0]
    n_exp = params["wrh"].shape[0]
    assert tt_p * n_p == tt_s * sb == TOKEN_TILE and n_exp % SUBLANES == 0
    rows = TOKEN_TILE
    n_steps_p = t_p // tt_p
    n_steps_s = xs_tm.shape[0] // tt_s
    assert n_steps_p * tt_p == t_p and n_steps_s * tt_s == xs_tm.shape[0]
    n_steps = n_steps_p + n_steps_s
    n_tok = n_steps * rows
    pitch = d // LANES
    consts = [params[n] for n in _MIXER_CONSTS]
    const_spec = lambda a: pl.BlockSpec(a.shape, lambda i, _nd=a.ndim: (0,) * _nd, pipeline_mode=pl.Buffered(1))
    any_spec = pl.BlockSpec(memory_space=pl.ANY)
    in_specs = [
        pl.BlockSpec((n_p, tt_p, d), lambda i: (0, jnp.minimum(i, n_steps_p - 1), 0)),
        pl.BlockSpec((tt_s, sb, d), lambda i: (jnp.maximum(i - n_steps_p, 0), 0, 0)),
        any_spec, any_spec,
    ] + [const_spec(a) for a in consts]
    out_shape = (
        jax.ShapeDtypeStruct((n_tok, d), F32),
        jax.ShapeDtypeStruct((n_tok * pitch, LANES), F32),
        jax.ShapeDtypeStruct((2 * TOP_K, n_tok), jnp.int32),
        jax.ShapeDtypeStruct((n_tok, LANES), F32),
        jax.ShapeDtypeStruct((n_exp, LANES), F32),
        jax.ShapeDtypeStruct((w_a - 1, n_p, d), F32),
        jax.ShapeDtypeStruct((w_b - 1, n_p, d), F32),
        jax.ShapeDtypeStruct(hs_a.shape, F32),
        jax.ShapeDtypeStruct(hs_b.shape, F32),
    )
    out_specs = (
        pl.BlockSpec((rows, d), lambda i: (i, 0)),
        pl.BlockSpec((rows * pitch, LANES), lambda i: (i, 0)),
        pl.BlockSpec((2 * TOP_K, rows), lambda i: (0, i)),
        pl.BlockSpec((rows, LANES), lambda i: (i, 0)),
        pl.BlockSpec((n_exp, LANES), lambda i: (0, 0)),
        any_spec, any_spec, any_spec, any_spec,
    )
    return pl.pallas_call(
        functools.partial(_mixer_kernel, n_steps_p=n_steps_p),
        grid=(n_steps,),
        in_specs=in_specs,
        out_specs=out_specs,
        out_shape=out_shape,
        scratch_shapes=[pltpu.VMEM((tt_p + w_a, n_p, d), F32),
                        pltpu.VMEM((tt_p + w_b, n_p, d), F32),
                        pltpu.VMEM((tt_s + w_a, sb, d), F32),
                        pltpu.VMEM((tt_s + w_b, sb, d), F32),
                        pltpu.VMEM((n_exp, LANES), F32),
                        pltpu.VMEM((2, rows, d), F32),
                        pltpu.VMEM((pitch, rows, LANES), F32),
                        pltpu.SemaphoreType.DMA((2,))],
        compiler_params=pltpu.CompilerParams(dimension_semantics=("arbitrary",), vmem_limit_bytes=VMEM_LIMIT),
        name="mixer_router",
    )(xp, xs_tm, hs_a, hs_b, *consts)


def _dispatch_kernel(zero_ref, x_hbm, dest_ref, xb_ref, xbuf, zero_buf, in_sem, out_sem, zsem, *, pitch):
    i = pl.program_id(0)
    n_tiles = pl.num_programs(0)
    n_buf, tile_rows = xbuf.shape[:2]
    tm = tile_rows // pitch
    n_zero = zero_ref.shape[0]
    blk_rows = zero_buf.shape[0]
    n_all = tm * TOP_K * pitch

    def fetch(t):
        start = t * tile_rows if isinstance(t, int) else pl.multiple_of(t * tile_rows, tile_rows)
        return pltpu.make_async_copy(x_hbm.at[pl.ds(start, tile_rows)], xbuf.at[t % n_buf], in_sem.at[t % n_buf])

    def wait_scatters(t):
        pltpu.make_async_copy(xb_ref.at[pl.ds(0, n_all)], xb_ref.at[pl.ds(0, n_all)], out_sem.at[t % 2]).wait()

    @pl.when(i == 0)
    def _():
        fetch(0).start()
        zero_buf[...] = jnp.zeros_like(zero_buf)

        def zcopy(e):
            start = pl.multiple_of(jnp.maximum(zero_ref[e], 0) * pitch, blk_rows)
            return pltpu.make_async_copy(zero_buf, xb_ref.at[pl.ds(start, blk_rows)], zsem)

        def start(e, c):
            @pl.when(zero_ref[e] >= 0)
            def _():
                zcopy(e).start()
            return c

        def wait(e, c):
            @pl.when(zero_ref[e] >= 0)
            def _():
                zcopy(e).wait()
            return c

        lax.fori_loop(0, n_zero, start, 0)
        lax.fori_loop(0, n_zero, wait, 0)

    @pl.when(i + 1 < n_tiles)
    def _():
        fetch(i + 1).start()

    fetch(i).wait()
    slot = i % n_buf

    def start_rows(r, c):
        src = xbuf.at[slot, pl.ds(pl.multiple_of(r * pitch, pitch), pitch)]
        for k in range(TOP_K):
            dst = dest_ref[0, k * tm + r]
            pltpu.make_async_copy(src, xb_ref.at[pl.ds(pl.multiple_of(dst * pitch, pitch), pitch)],
                                  out_sem.at[i % 2]).start(priority=k % 2)
        return c

    lax.fori_loop(0, tm, start_rows, 0, unroll=ROW_DMA_UNROLL)

    @pl.when(i > 0)
    def _():
        wait_scatters(i - 1)

    @pl.when(i == n_tiles - 1)
    def _():
        wait_scatters(i)


def _dispatch_call(xn2_tiles, dest_tiles, zero_start, n_rows, *, tm, d):
    pitch = d // LANES
    n_tiles = xn2_tiles.shape[0] // (tm * pitch)
    grid_spec = pltpu.PrefetchScalarGridSpec(
        num_scalar_prefetch=1,
        grid=(n_tiles,),
        in_specs=[pl.BlockSpec(memory_space=pl.ANY),
                  pl.BlockSpec((None, 1, tm * TOP_K), lambda i, z: (i, 0, 0), memory_space=pltpu.SMEM)],
        out_specs=pl.BlockSpec(memory_space=pl.ANY),
        scratch_shapes=[pltpu.VMEM((3, tm * pitch, LANES), F32),
                        pltpu.VMEM((MOE_BLOCK * pitch, LANES), F32),
                        pltpu.SemaphoreType.DMA((3,)), pltpu.SemaphoreType.DMA((2,)),
                        pltpu.SemaphoreType.DMA],
    )
    return pl.pallas_call(
        functools.partial(_dispatch_kernel, pitch=pitch),
        grid_spec=grid_spec,
        out_shape=jax.ShapeDtypeStruct((n_rows * pitch, LANES), F32),
        compiler_params=pltpu.CompilerParams(dimension_semantics=("arbitrary",)),
        name="moe_dispatch",
    )(zero_start, xn2_tiles, dest_tiles)


def _expert_kernel(be_ref, nu_ref, first_ref, slot_ref, next_ref, x_ref, wgu_hbm, wd_hbm, *rest):
    n_sub = EXPERT_BLOCKS_PER_STEP
    bias_refs, (y_ref, wgu_st, wd_st, wgu_bf, wd_bf, sem) = rest[:2 * n_sub], rest[2 * n_sub:]
    blk_rows = x_ref.shape[0] // n_sub
    for h in range(n_sub):
        rows = pl.ds(h * blk_rows, blk_rows)
        _expert_block(pl.program_id(0) * n_sub + h, be_ref, nu_ref, first_ref, slot_ref, next_ref,
                      x_ref.at[rows], wgu_hbm, bias_refs[2 * h], wd_hbm, bias_refs[2 * h + 1], y_ref.at[rows],
                      wgu_st, wd_st, wgu_bf, wd_bf, sem)


def _expert_block(b, be_ref, nu_ref, first_ref, slot_ref, next_ref, x_ref, wgu_hbm, bgu_ref, wd_hbm, bd_ref, y_ref,
                  wgu_st, wd_st, wgu_bf, wd_bf, sem):
    def fetch(e, slot):
        return (pltpu.make_async_copy(wgu_hbm.at[e], wgu_st.at[slot], sem.at[slot, 0]),
                pltpu.make_async_copy(wd_hbm.at[e], wd_st.at[slot], sem.at[slot, 1]))

    @pl.when(b < nu_ref[0])
    def _():
        d_ff, d = wd_bf.shape

        @pl.when(first_ref[b] == 1)
        def _():
            slot = slot_ref[b]

            @pl.when(b == 0)
            def _():
                for cp in fetch(be_ref[b], slot):
                    cp.start()

            for cp in fetch(be_ref[b], slot):
                cp.wait()

            @pl.when(next_ref[b] >= 0)
            def _():
                for cp in fetch(next_ref[b], 1 - slot):
                    cp.start()

            wgu_bf[...] = wgu_st[slot].astype(BF16)
            wd_bf[...] = wd_st[slot].astype(BF16)

        x = _load_row_tiles(x_ref, MOE_BLOCK, d)
        h = jnp.dot(x.astype(BF16), wgu_bf[...], preferred_element_type=F32) + bgu_ref[...]
        g = jnp.minimum(h[:, :d_ff], SWIGLU_LIMIT)
        u = jnp.clip(h[:, d_ff:], -SWIGLU_LIMIT, SWIGLU_LIMIT)
        act = (u + 1.0) * (g * _sigmoid(SWIGLU_ALPHA * g))
        y = jnp.dot(act.astype(BF16), wd_bf[...], preferred_element_type=F32) + bd_ref[...]
        _store_row_tiles(y_ref, y)

    @pl.when(b >= nu_ref[0])
    def _():
        y_ref[...] = jnp.zeros(y_ref.shape, F32)


def _expert_call(xb_tiles, block_e, n_used, run_first, run_slot, run_next, wgu, bgu, wd, bd):
    n_exp, d, two_ff = wgu.shape
    d_ff = wd.shape[1]
    n_sub = EXPERT_BLOCKS_PER_STEP
    step_rows = n_sub * MOE_BLOCK * d // LANES
    n_steps = xb_tiles.shape[0] // step_rows
    assert n_steps * step_rows == xb_tiles.shape[0]
    any_spec = pl.BlockSpec(memory_space=pl.ANY)
    bias_specs, biases = [], []
    for h in range(n_sub):
        per_e = lambda s, be, *_, h=h: (be[s * n_sub + h], 0, 0)
        bias_specs += [pl.BlockSpec((None, 1, two_ff), per_e), pl.BlockSpec((None, 1, d), per_e)]
        biases += [bgu.reshape(n_exp, 1, two_ff), bd.reshape(n_exp, 1, d)]
    grid_spec = pltpu.PrefetchScalarGridSpec(
        num_scalar_prefetch=5,
        grid=(n_steps,),
        in_specs=[pl.BlockSpec((step_rows, LANES), lambda s, *_: (s, 0)), any_spec, any_spec] + bias_specs,
        out_specs=pl.BlockSpec((step_rows, LANES), lambda s, *_: (s, 0)),
        scratch_shapes=[pltpu.VMEM((2, d, two_ff), F32), pltpu.VMEM((2, d_ff, d), F32),
                        pltpu.VMEM((d, two_ff), BF16), pltpu.VMEM((d_ff, d), BF16),
                        pltpu.SemaphoreType.DMA((2, 2))],
    )
    return pl.pallas_call(
        _expert_kernel,
        grid_spec=grid_spec,
        out_shape=jax.ShapeDtypeStruct(xb_tiles.shape, F32),
        compiler_params=pltpu.CompilerParams(dimension_semantics=("arbitrary",), vmem_limit_bytes=VMEM_LIMIT),
        name="moe_experts",
    )(block_e, n_used, run_first, run_slot, run_next, xb_tiles, wgu, wd, *biases)


def _combine_kernel(xmid_ref, gate_ref, dest_ref, dest_next_ref, gfin_ref, yb_ref, outp_ref, outs_ref,
                    ybuf, ot_buf, sem, *, n_tiles_p):
    i = pl.program_id(0)
    n_tiles = pl.num_programs(0)
    tm, d = xmid_ref.shape
    pitch = d // LANES

    def start_tile(dref, slot):
        def start_rows(r, c):
            for k in range(TOP_K):
                src = dref[0, k * tm + r]
                pltpu.make_async_copy(yb_ref.at[pl.ds(pl.multiple_of(src * pitch, pitch), pitch)],
                                      ybuf.at[slot, k, pl.ds(pl.multiple_of(r * pitch, pitch), pitch)],
                                      sem.at[slot]).start(priority=k % 2)
            return c

        lax.fori_loop(0, tm, start_rows, 0, unroll=ROW_DMA_UNROLL)

    @pl.when(i == 0)
    def _():
        start_tile(dest_ref, 0)

    slot = i % 2

    @pl.when(i + 1 < n_tiles)
    def _():
        start_tile(dest_next_ref, 1 - slot)

    pltpu.make_async_copy(ybuf.at[slot], ybuf.at[slot], sem.at[slot]).wait()
    gate = gate_ref[...]
    y = xmid_ref[...]
    for k in range(TOP_K):
        y = y + gate[:, k:k + 1] * _load_row_tiles(ybuf.at[slot, k], tm, d)
    out = _rms(y, gfin_ref[...])

    @pl.when(i < n_tiles_p)
    def _():
        _store_seq_major(outp_ref, ot_buf, out)

    @pl.when(i >= n_tiles_p)
    def _():
        outs_ref[...] = out


def _combine_call(x_mid, gate, dest_tiles, g_final, yb, *, tm, n_p, n_tiles_p):
    n_tok, d = x_mid.shape
    n_tiles = n_tok // tm
    tt = tm // n_p
    dest_spec = lambda off: pl.BlockSpec((None, 1, tm * TOP_K),
                                         lambda i: (jnp.minimum(i + off, n_tiles - 1), 0, 0),
                                         memory_space=pltpu.SMEM)
    return pl.pallas_call(
        functools.partial(_combine_kernel, n_tiles_p=n_tiles_p),
        grid=(n_tiles,),
        in_specs=[pl.BlockSpec((tm, d), lambda i: (i, 0)),
                  pl.BlockSpec((tm, LANES), lambda i: (i, 0)),
                  dest_spec(0), dest_spec(1),
                  pl.BlockSpec((1, d), lambda i: (0, 0)),
                  pl.BlockSpec(memory_space=pl.ANY)],
        out_specs=(pl.BlockSpec((n_p, tt, d), lambda i: (0, jnp.minimum(i, n_tiles_p - 1), 0)),
                   pl.BlockSpec((tm, d), lambda i: (jnp.maximum(i - n_tiles_p, 0), 0))),
        out_shape=(jax.ShapeDtypeStruct((n_p, n_tiles_p * tt, d), F32),
                   jax.ShapeDtypeStruct(((n_tiles - n_tiles_p) * tm, d), F32)),
        scratch_shapes=[pltpu.VMEM((2, TOP_K, tm * d // LANES, LANES), F32),
                        pltpu.VMEM((d // LANES, tm, LANES), F32),
                        pltpu.SemaphoreType.DMA((2,))],
        compiler_params=pltpu.CompilerParams(dimension_semantics=("arbitrary",), vmem_limit_bytes=VMEM_LIMIT),
        name="moe_combine",
    )(x_mid, gate, dest_tiles, dest_tiles, g_final, yb)


def _to_time_major(x, seq_block):
    n_seqs, t, d = x.shape
    n_sb = n_seqs // seq_block
    return x.reshape(n_sb, seq_block, t, d).transpose(0, 2, 1, 3).reshape(n_sb * t, seq_block, d)


def _from_time_major(x, n_seqs, seq_block):
    d = x.shape[-1]
    n_sb = n_seqs // seq_block
    t = x.size // (n_seqs * d)
    return x.reshape(n_sb, t, seq_block, d).transpose(0, 2, 1, 3).reshape(n_seqs, t, d)


def _layer(xp, xs, state_a, state_b, p, norm_final_g):
    n_p, t_p, d = xp.shape
    n_s, t_s, _ = xs.shape
    n_exp = p["wgu"].shape[0]
    tm = TOKEN_TILE
    sb = tm // t_s
    n_tiles_p = n_p * t_p // tm

    (x_mid, xn2, route, gate, cnt, newa_p, newb_p, newa_s, newb_s) = _mixer_call(
        xp, _to_time_major(xs, sb), _to_time_major(state_a, sb), _to_time_major(state_b, sb),
        p, tt_p=tm // n_p, tt_s=t_s)
    n_tok = x_mid.shape[0]
    n_tiles = n_tok // tm

    counts = cnt[:, 0].astype(jnp.int32)
    padded = (counts + MOE_BLOCK - 1) // MOE_BLOCK * MOE_BLOCK
    pad_end = jnp.cumsum(padded)
    pad_start = pad_end - padded
    is_e = route[:TOP_K, :, None] == jnp.arange(n_exp, dtype=jnp.int32)
    dest = jnp.sum(jnp.where(is_e, pad_start, 0), axis=-1) + route[TOP_K:]
    dest_tiles = dest.reshape(TOP_K, n_tiles, tm).transpose(1, 0, 2).reshape(n_tiles, 1, TOP_K * tm)
    n_blocks = -(-(n_tok * TOP_K) // MOE_BLOCK) + n_exp
    n_blocks = -(-n_blocks // EXPERT_BLOCKS_PER_STEP) * EXPERT_BLOCKS_PER_STEP
    n_used = (pad_end[-1] // MOE_BLOCK).astype(jnp.int32)
    blk_start = jnp.minimum(jnp.arange(n_blocks, dtype=jnp.int32) * MOE_BLOCK, pad_end[-1] - 1)
    block_e = jnp.minimum(jnp.sum(blk_start[:, None] >= pad_end[None, :], axis=1), n_exp - 1).astype(jnp.int32)
    n_tail = n_exp + EXPERT_BLOCKS_PER_STEP - 1
    last_blocks = jnp.arange(n_blocks - n_tail, n_blocks, dtype=jnp.int32)
    zero_start = jnp.concatenate([jnp.where(padded > 0, pad_end - MOE_BLOCK, -1),
                                  jnp.where(last_blocks >= n_used, last_blocks * MOE_BLOCK, -1)]).astype(jnp.int32)

    xb = _dispatch_call(xn2, dest_tiles, zero_start, n_blocks * MOE_BLOCK, tm=tm, d=d)
    blk_ids = jnp.arange(n_blocks, dtype=jnp.int32)
    prev_e = jnp.concatenate([jnp.full((1,), -1, jnp.int32), block_e[:-1]])
    run_first = ((block_e != prev_e) & (blk_ids < n_used)).astype(jnp.int32)
    run_slot = ((jnp.cumsum(run_first) - 1) % 2).astype(jnp.int32)
    e_ids = jnp.arange(n_exp, dtype=jnp.int32)
    later = lax.cummin(jnp.where(padded > 0, e_ids, n_exp), axis=0, reverse=True)
    next_of = jnp.concatenate([later[1:], jnp.full((1,), n_exp, jnp.int32)])
    next_of = jnp.where(next_of >= n_exp, -1, next_of)
    run_next = jnp.sum(jnp.where(block_e[:, None] == e_ids[None, :], next_of[None, :], 0), axis=1).astype(jnp.int32)

    yb = _expert_call(xb, block_e, n_used.reshape(1), run_first, run_slot, run_next,
                      p["wgu"], p["bgu"], p["wd"], p["bd"])
    y_p, y_s = _combine_call(x_mid, gate, dest_tiles, norm_final_g.reshape(1, d), yb, tm=tm, n_p=n_p,
                             n_tiles_p=n_tiles_p)

    return (y_p, _from_time_major(y_s, n_s, sb),
            _from_time_major(newa_p, n_p, n_p), _from_time_major(newb_p, n_p, n_p),
            _from_time_major(newa_s, n_s, sb), _from_time_major(newb_s, n_s, sb))


def _prep_params(l, norm_mix_g, w_in, b_gates, conv_a_w, conv_a_b, w_a_out, conv_b_w, conv_b_b, ln_b_g,
                 ln_b_b, w_b_out, w_o, norm_ffn_g, w_router, b_router, w_gu, b_gu, w_down, b_down):
    row = lambda v: v.reshape(1, -1)
    taps = lambda w: jnp.broadcast_to(w[:, None, :], (w.shape[0], SUBLANES, w.shape[1]))
    wr_t = w_router[l].T
    wr_hi = wr_t.astype(BF16)
    return dict(
        gmix=row(norm_mix_g[l]), win=w_in[l].astype(BF16), bg=b_gates[l],
        caw=taps(conv_a_w[l]), cab=row(conv_a_b[l]), waout=w_a_out[l].astype(BF16),
        cbw=taps(conv_b_w[l]), cbb=row(conv_b_b[l]), lng=row(ln_b_g[l]), lnb=row(ln_b_b[l]),
        wbout=w_b_out[l].astype(BF16), wo=w_o[l].astype(BF16), gffn=row(norm_ffn_g[l]),
        wrh=wr_hi, wrl=(wr_t - wr_hi.astype(F32)).astype(BF16), br=b_router[l].reshape(-1, 1),
        wgu=w_gu[l], bgu=b_gu[l], wd=w_down[l], bd=b_down[l])


def kernel(x_prompt, x_sample, state_conv_a, state_conv_b, norm_mix_g, w_in, b_gates, conv_a_w, conv_a_b, w_a_out, conv_b_w, conv_b_b, ln_b_g, ln_b_b, w_b_out, w_o, norm_ffn_g, w_router, b_router, w_gu, b_gu, w_down, b_down, norm_final_g):
    depth = w_in.shape[0]
    assert depth == 1, "the final norm is fused into the last layer's combine call"
    p = _prep_params(0, norm_mix_g, w_in, b_gates, conv_a_w, conv_a_b, w_a_out, conv_b_w, conv_b_b, ln_b_g,
                     ln_b_b, w_b_out, w_o, norm_ffn_g, w_router, b_router, w_gu, b_gu, w_down, b_down)
    y_p, y_s, na_p, nb_p, na_s, nb_s = _layer(x_prompt, x_sample, state_conv_a[0], state_conv_b[0], p,
                                               norm_final_g)
    return (y_p, y_s, na_p[None], nb_p[None], na_s[None], nb_s[None])
```

```python
import functools

import jax
import jax.numpy as jnp
from jax import lax
from jax.experimental import pallas as pl
from jax.experimental.pallas import tpu as pltpu

EPS = 1e-5
SWIGLU_ALPHA = 1.702
SWIGLU_LIMIT = 7.0
TOP_K = 4
MOE_BLOCK = 256
TOKEN_TILE = 256
LANES = 128
SUBLANES = 8
VMEM_LIMIT = 60 * 1024 * 1024
CONV_OUT_BLOCK = 16
CONV_FIR_LEVELS = 2
ROW_DMA_UNROLL = 8
CONV_LANES = 128
TWO_PHASE_MIN_TAPS = 8
EXPERT_BLOCKS_PER_STEP = 4

F32 = jnp.float32
BF16 = jnp.bfloat16


def _sigmoid(v):
    return 1.0 / (1.0 + jnp.exp(-v))


def _store_row_tiles(ref, value):
    n, d = value.shape
    pitch = d // LANES
    for c in range(pitch):
        ref[pl.ds(c, n, stride=pitch), :] = value[:, c * LANES:(c + 1) * LANES]


def _load_row_tiles(ref, n, d):
    pitch = d // LANES
    return jnp.concatenate([ref[pl.ds(c, n, stride=pitch), :] for c in range(pitch)], axis=1)


def _rms(v, g):
    return v * lax.rsqrt(jnp.mean(v * v, axis=-1, keepdims=True) + EPS) * g


def _mac(acc, term):
    return term if acc is None else acc + term


def _fir(e, w, n_out, width, levels):
    if levels == 0 or width < TWO_PHASE_MIN_TAPS:
        acc = [None] * n_out
        for k in range(width):
            wk = w(k)
            for j in range(n_out):
                acc[j] = _mac(acc[j], wk * e(j + k))
        return acc
    half, n0, n1 = (n_out + 1) // 2, (width + 1) // 2, width // 2
    s_cache, h_cache = {}, {}

    def s(i):
        if i not in s_cache:
            s_cache[i] = e(2 * i + 1) + e(2 * i + 2)
        return s_cache[i]

    def hs(j):
        if j not in h_cache:
            h_cache[j] = w(2 * j) + w(2 * j + 1) if j < n1 else w(2 * j)
        return h_cache[j]

    a = _fir(lambda i: e(2 * i), lambda j: w(2 * j), half + 1, n0, levels - 1)
    b = _fir(lambda i: e(2 * i + 1), lambda j: w(2 * j + 1), half, n1, levels - 1)
    c = _fir(s, hs, half, n0, levels - 1)
    out = []
    for m in range(half):
        out += [a[m] + b[m], c[m] - a[m + 1] - b[m]]
    return out[:n_out]


def _fir_reach(n_out, width, levels):
    seen = [0]

    def e(i):
        seen[0] = max(seen[0], i)
        return 0.0

    _fir(e, lambda k: 1.0, n_out, width, levels)
    return seen[0]


def _conv_block(n_out):
    return min(CONV_OUT_BLOCK, n_out)


def _conv_spare_rows(n_out, width):
    nb = _conv_block(n_out)
    return _fir_reach(nb, width, CONV_FIR_LEVELS) - (nb + width - 2)


def _causal_conv(ext_ref, w_ref, bias, out_ref, n_out):
    width = w_ref.shape[0]
    n_seq, d = ext_ref.shape[1:]
    nb = _conv_block(n_out)
    assert n_out % nb == 0 and n_seq % SUBLANES == 0 and d % CONV_LANES == 0
    assert ext_ref.shape[0] >= n_out + width - 1 + _conv_spare_rows(n_out, width)

    def block(tb, carry):
        t0 = tb * nb
        for sg in range(n_seq // SUBLANES):
            rows = pl.ds(sg * SUBLANES, SUBLANES)
            for lc in range(d // CONV_LANES):
                lanes = pl.ds(lc * CONV_LANES, CONV_LANES)
                loaded, taps = {}, {}

                def e(i):
                    if i not in loaded:
                        loaded[i] = ext_ref[t0 + i, rows, lanes]
                    return loaded[i]

                def w(k):
                    if k not in taps:
                        taps[k] = w_ref[k, :, lanes]
                    return taps[k]

                for j, a in enumerate(_fir(e, w, nb, width, CONV_FIR_LEVELS)):
                    row0 = pl.multiple_of((t0 + j) * n_seq + sg * SUBLANES, SUBLANES)
                    out_ref[pl.ds(row0, SUBLANES), lanes] = a + bias[:, lc * CONV_LANES:(lc + 1) * CONV_LANES]
        return carry

    lax.fori_loop(0, n_out // nb, block, 0)
    return out_ref[...]


_MIXER_CONSTS = ("gmix", "win", "bg", "caw", "cab", "waout", "cbw", "cbb", "lng", "lnb", "wbout", "wo",
                 "gffn", "wrh", "wrl", "br")


def _time_major_rows(src_ref, buf, t0, tt):
    n_seq, _, d = src_ref.shape
    for s in range(n_seq):
        for c in range(d // LANES):
            buf[c, pl.ds(s, tt, stride=n_seq), :] = src_ref[s, t0:t0 + tt, c * LANES:(c + 1) * LANES]
    return jnp.concatenate([buf[c] for c in range(d // LANES)], axis=1)


def _store_seq_major(dst_ref, buf, value):
    n_seq, tt, d = dst_ref.shape
    for c in range(d // LANES):
        buf[c] = value[:, c * LANES:(c + 1) * LANES]
    for s in range(n_seq):
        for c in range(d // LANES):
            dst_ref[s, :, c * LANES:(c + 1) * LANES] = buf[c, pl.ds(s, tt, stride=n_seq), :]


def _mixer_tile(x, exta, extb, c, xmid_ref, xn2_ref, route_ref, gate_ref, cnt_ref, cnt_acc, conv_buf):
    rows, d = x.shape
    n_seq = exta.shape[1]
    tt = rows // n_seq
    w_a = c["caw"].shape[0]
    w_b = c["cbw"].shape[0]

    xn = _rms(x, c["gmix"][...]).astype(BF16)

    def proj(g):
        return jnp.dot(xn, c["win"][:, g * d:(g + 1) * d], preferred_element_type=F32)

    exta[w_a - 1:w_a - 1 + tt] = (proj(1) * proj(2)).reshape(tt, n_seq, d)
    conv_a = _causal_conv(exta, c["caw"], c["cab"][...], conv_buf.at[0], tt)
    y_a = jnp.dot((proj(0) * conv_a).astype(BF16), c["waout"][...], preferred_element_type=F32)
    extb[w_b - 1:w_b - 1 + tt] = (proj(3) * _sigmoid(proj(4))).reshape(tt, n_seq, d)
    conv_b = _causal_conv(extb, c["cbw"], c["cbb"][...], conv_buf.at[1], tt)
    mu = jnp.mean(conv_b, axis=-1, keepdims=True)
    cen = conv_b - mu
    ln = cen * lax.rsqrt(jnp.mean(cen * cen, axis=-1, keepdims=True) + EPS) * c["lng"][...] + c["lnb"][...]
    y_b = jnp.dot((ln * _sigmoid(ln)).astype(BF16), c["wbout"][...], preferred_element_type=F32)

    bg = c["bg"]
    merged = _sigmoid(proj(5) + bg[0:1, :]) * y_a + _sigmoid(proj(6) + bg[1:2, :]) * y_b
    x_mid = x + jnp.dot(merged.astype(BF16), c["wo"][...], preferred_element_type=F32)
    xmid_ref[...] = x_mid
    xn2 = _rms(x_mid, c["gffn"][...])
    _store_row_tiles(xn2_ref, xn2)

    n_exp = c["wrh"].shape[0]
    nt = (((1,), (1,)), ((), ()))
    x_hi = xn2.astype(BF16)
    x_lo = (xn2 - x_hi.astype(F32)).astype(BF16)
    logits = (lax.dot_general(c["wrh"][...], x_hi, nt, preferred_element_type=F32)
              + lax.dot_general(c["wrh"][...], x_lo, nt, preferred_element_type=F32)
              + lax.dot_general(c["wrl"][...], x_hi, nt, preferred_element_type=F32)) + c["br"][...]
    e_io = lax.broadcasted_iota(jnp.int32, (n_exp, rows), 0)
    work = logits
    top_v, top_i = [], []
    for _ in range(TOP_K):
        m = jnp.max(work, axis=0, keepdims=True)
        idx = jnp.min(jnp.where(work == m, e_io, n_exp), axis=0, keepdims=True)
        top_v.append(m)
        top_i.append(idx)
        work = jnp.where(e_io == idx, -jnp.inf, work)
    ex = [jnp.exp(v - top_v[0]) for v in top_v]
    den = ex[0] + ex[1] + ex[2] + ex[3]
    onehot = jnp.zeros((n_exp, rows), F32)
    for idx in top_i:
        onehot = onehot + (e_io == idx).astype(F32)
    r_io = lax.broadcasted_iota(jnp.int32, (rows, rows), 0)
    c_io = lax.broadcasted_iota(jnp.int32, (rows, rows), 1)
    before = (r_io < c_io).astype(BF16)
    cnt = cnt_acc[...]
    prefix = jnp.dot(onehot.astype(BF16), before, preferred_element_type=F32) + cnt[:, 0:1]
    pos = [jnp.sum(jnp.where(e_io == idx, prefix, 0.0), axis=0, keepdims=True) for idx in top_i]
    route_ref[...] = jnp.concatenate(top_i + [p.astype(jnp.int32) for p in pos], axis=0)
    gates = jnp.concatenate([e / den for e in ex] + [jnp.zeros((LANES - TOP_K, rows), F32)], axis=0)
    gate_ref[...] = jnp.transpose(gates)
    cnt = cnt + jnp.sum(onehot, axis=1, keepdims=True)
    cnt_acc[...] = cnt
    cnt_ref[...] = cnt


def _mixer_kernel(*refs, n_steps_p):
    n_c = len(_MIXER_CONSTS)
    xp_ref, xs_ref, hsa_ref, hsb_ref = refs[:4]
    c = dict(zip(_MIXER_CONSTS, refs[4:4 + n_c]))
    (xmid_ref, xn2_ref, route_ref, gate_ref, cnt_ref,
     newa_p_ref, newb_p_ref, newa_s_ref, newb_s_ref) = refs[4 + n_c:13 + n_c]
    exta_p, extb_p, exta_s, extb_s, cnt_acc, conv_buf, xt_buf, sem = refs[13 + n_c:]
    i = pl.program_id(0)
    tt_p = xp_ref.shape[1]
    tt_s = xs_ref.shape[0]
    w_a = c["caw"].shape[0]
    w_b = c["cbw"].shape[0]
    tile_args = (c, xmid_ref, xn2_ref, route_ref, gate_ref, cnt_ref, cnt_acc, conv_buf)

    @pl.when(i == 0)
    def _():
        exta_p[0:w_a - 1] = jnp.zeros((w_a - 1,) + exta_p.shape[1:], F32)
        extb_p[0:w_b - 1] = jnp.zeros((w_b - 1,) + extb_p.shape[1:], F32)
        for ext, tt, w in ((exta_p, tt_p, w_a), (extb_p, tt_p, w_b), (exta_s, tt_s, w_a), (extb_s, tt_s, w_b)):
            if ext.shape[0] > tt + w - 1:
                ext[tt + w - 1:] = jnp.zeros((ext.shape[0] - (tt + w - 1),) + ext.shape[1:], F32)
        cnt_acc[...] = jnp.zeros_like(cnt_acc)

    @pl.when(i < n_steps_p)
    def _():
        _mixer_tile(_time_major_rows(xp_ref, xt_buf, 0, tt_p), exta_p, extb_p, *tile_args)
        exta_p[0:w_a - 1] = exta_p[tt_p:tt_p + w_a - 1]
        extb_p[0:w_b - 1] = extb_p[tt_p:tt_p + w_b - 1]

        @pl.when(i == n_steps_p - 1)
        def _():
            cp_a = pltpu.make_async_copy(exta_p.at[pl.ds(0, w_a - 1)], newa_p_ref, sem.at[0])
            cp_b = pltpu.make_async_copy(extb_p.at[pl.ds(0, w_b - 1)], newb_p_ref, sem.at[1])
            cp_a.start()
            cp_b.start()
            cp_a.wait()
            cp_b.wait()

    @pl.when(i >= n_steps_p)
    def _():
        q = i - n_steps_p
        in_a = pltpu.make_async_copy(hsa_ref.at[pl.ds(q * (w_a - 1), w_a - 1)],
                                     exta_s.at[pl.ds(0, w_a - 1)], sem.at[0])
        in_b = pltpu.make_async_copy(hsb_ref.at[pl.ds(q * (w_b - 1), w_b - 1)],
                                     extb_s.at[pl.ds(0, w_b - 1)], sem.at[1])
        in_a.start()
        in_b.start()
        in_a.wait()
        in_b.wait()
        _mixer_tile(xs_ref[...].reshape(TOKEN_TILE, xs_ref.shape[-1]), exta_s, extb_s, *tile_args)
        out_a = pltpu.make_async_copy(exta_s.at[pl.ds(tt_s, w_a - 1)],
                                      newa_s_ref.at[pl.ds(q * (w_a - 1), w_a - 1)], sem.at[0])
        out_b = pltpu.make_async_copy(extb_s.at[pl.ds(tt_s, w_b - 1)],
                                      newb_s_ref.at[pl.ds(q * (w_b - 1), w_b - 1)], sem.at[1])
        out_a.start()
        out_b.start()
        out_a.wait()
        out_b.wait()


def _mixer_call(xp, xs_tm, hs_a, hs_b, params, *, tt_p, tt_s):
    n_p, t_p, d = xp.shape
    sb = xs_tm.shape[1]
    w_a = params["caw"].shape[0]
    w_b = params["cbw"].shape[0]
    n_exp = params["wrh"].shape[0]
    assert tt_p * n_p == tt_s * sb == TOKEN_TILE and n_exp % SUBLANES == 0
    rows = TOKEN_TILE
    n_steps_p = t_p // tt_p
    n_steps_s = xs_tm.shape[0] // tt_s
    assert n_steps_p * tt_p == t_p and n_steps_s * tt_s == xs_tm.shape[0]
    n_steps = n_steps_p + n_steps_s
    n_tok = n_steps * rows
    pitch = d // LANES
    consts = [params[n] for n in _MIXER_CONSTS]
    ext_rows = lambda tt, w: tt + w - 1 + _conv_spare_rows(tt, w)
    const_spec = lambda a: pl.BlockSpec(a.shape, lambda i, _nd=a.ndim: (0,) * _nd, pipeline_mode=pl.Buffered(1))
    any_spec = pl.BlockSpec(memory_space=pl.ANY)
    in_specs = [
        pl.BlockSpec((n_p, tt_p, d), lambda i: (0, jnp.minimum(i, n_steps_p - 1), 0)),
        pl.BlockSpec((tt_s, sb, d), lambda i: (jnp.maximum(i - n_steps_p, 0), 0, 0)),
        any_spec, any_spec,
    ] + [const_spec(a) for a in consts]
    out_shape = (
        jax.ShapeDtypeStruct((n_tok, d), F32),
        jax.ShapeDtypeStruct((n_tok * pitch, LANES), F32),
        jax.ShapeDtypeStruct((2 * TOP_K, n_tok), jnp.int32),
        jax.ShapeDtypeStruct((n_tok, LANES), F32),
        jax.ShapeDtypeStruct((n_exp, LANES), F32),
        jax.ShapeDtypeStruct((w_a - 1, n_p, d), F32),
        jax.ShapeDtypeStruct((w_b - 1, n_p, d), F32),
        jax.ShapeDtypeStruct(hs_a.shape, F32),
        jax.ShapeDtypeStruct(hs_b.shape, F32),
    )
    out_specs = (
        pl.BlockSpec((rows, d), lambda i: (i, 0)),
        pl.BlockSpec((rows * pitch, LANES), lambda i: (i, 0)),
        pl.BlockSpec((2 * TOP_K, rows), lambda i: (0, i)),
        pl.BlockSpec((rows, LANES), lambda i: (i, 0)),
        pl.BlockSpec((n_exp, LANES), lambda i: (0, 0)),
        any_spec, any_spec, any_spec, any_spec,
    )
    return pl.pallas_call(
        functools.partial(_mixer_kernel, n_steps_p=n_steps_p),
        grid=(n_steps,),
        in_specs=in_specs,
        out_specs=out_specs,
        out_shape=out_shape,
        scratch_shapes=[pltpu.VMEM((ext_rows(tt_p, w_a), n_p, d), F32),
                        pltpu.VMEM((ext_rows(tt_p, w_b), n_p, d), F32),
                        pltpu.VMEM((ext_rows(tt_s, w_a), sb, d), F32),
                        pltpu.VMEM((ext_rows(tt_s, w_b), sb, d), F32),
                        pltpu.VMEM((n_exp, LANES), F32),
                        pltpu.VMEM((2, rows, d), F32),
                        pltpu.VMEM((pitch, rows, LANES), F32),
                        pltpu.SemaphoreType.DMA((2,))],
        compiler_params=pltpu.CompilerParams(dimension_semantics=("arbitrary",), vmem_limit_bytes=VMEM_LIMIT),
        name="mixer_router",
    )(xp, xs_tm, hs_a, hs_b, *consts)


def _dispatch_kernel(zero_ref, x_hbm, dest_ref, xb_ref, xbuf, zero_buf, in_sem, out_sem, zsem, *, pitch):
    i = pl.program_id(0)
    n_tiles = pl.num_programs(0)
    n_buf, tile_rows = xbuf.shape[:2]
    tm = tile_rows // pitch
    n_zero = zero_ref.shape[0]
    blk_rows = zero_buf.shape[0]
    n_all = tm * TOP_K * pitch

    def fetch(t):
        start = t * tile_rows if isinstance(t, int) else pl.multiple_of(t * tile_rows, tile_rows)
        return pltpu.make_async_copy(x_hbm.at[pl.ds(start, tile_rows)], xbuf.at[t % n_buf], in_sem.at[t % n_buf])

    def wait_scatters(t):
        pltpu.make_async_copy(xb_ref.at[pl.ds(0, n_all)], xb_ref.at[pl.ds(0, n_all)], out_sem.at[t % 2]).wait()

    @pl.when(i == 0)
    def _():
        fetch(0).start()
        zero_buf[...] = jnp.zeros_like(zero_buf)

        def zcopy(e):
            start = pl.multiple_of(jnp.maximum(zero_ref[e], 0) * pitch, blk_rows)
            return pltpu.make_async_copy(zero_buf, xb_ref.at[pl.ds(start, blk_rows)], zsem)

        def start(e, c):
            @pl.when(zero_ref[e] >= 0)
            def _():
                zcopy(e).start()
            return c

        def wait(e, c):
            @pl.when(zero_ref[e] >= 0)
            def _():
                zcopy(e).wait()
            return c

        lax.fori_loop(0, n_zero, start, 0)
        lax.fori_loop(0, n_zero, wait, 0)

    @pl.when(i + 1 < n_tiles)
    def _():
        fetch(i + 1).start()

    fetch(i).wait()
    slot = i % n_buf

    def start_rows(r, c):
        src = xbuf.at[slot, pl.ds(pl.multiple_of(r * pitch, pitch), pitch)]
        for k in range(TOP_K):
            dst = dest_ref[0, k * tm + r]
            pltpu.make_async_copy(src, xb_ref.at[pl.ds(pl.multiple_of(dst * pitch, pitch), pitch)],
                                  out_sem.at[i % 2]).start(priority=k % 2)
        return c

    lax.fori_loop(0, tm, start_rows, 0, unroll=ROW_DMA_UNROLL)

    @pl.when(i > 0)
    def _():
        wait_scatters(i - 1)

    @pl.when(i == n_tiles - 1)
    def _():
        wait_scatters(i)


def _dispatch_call(xn2_tiles, dest_tiles, zero_start, n_rows, *, tm, d):
    pitch = d // LANES
    n_tiles = xn2_tiles.shape[0] // (tm * pitch)
    grid_spec = pltpu.PrefetchScalarGridSpec(
        num_scalar_prefetch=1,
        grid=(n_tiles,),
        in_specs=[pl.BlockSpec(memory_space=pl.ANY),
                  pl.BlockSpec((None, 1, tm * TOP_K), lambda i, z: (i, 0, 0), memory_space=pltpu.SMEM)],
        out_specs=pl.BlockSpec(memory_space=pl.ANY),
        scratch_shapes=[pltpu.VMEM((3, tm * pitch, LANES), F32),
                        pltpu.VMEM((MOE_BLOCK * pitch, LANES), F32),
                        pltpu.SemaphoreType.DMA((3,)), pltpu.SemaphoreType.DMA((2,)),
                        pltpu.SemaphoreType.DMA],
    )
    return pl.pallas_call(
        functools.partial(_dispatch_kernel, pitch=pitch),
        grid_spec=grid_spec,
        out_shape=jax.ShapeDtypeStruct((n_rows * pitch, LANES), F32),
        compiler_params=pltpu.CompilerParams(dimension_semantics=("arbitrary",)),
        name="moe_dispatch",
    )(zero_start, xn2_tiles, dest_tiles)


def _expert_kernel(be_ref, nu_ref, first_ref, slot_ref, next_ref, x_ref, wgu_hbm, wd_hbm, *rest):
    n_sub = EXPERT_BLOCKS_PER_STEP
    bias_refs, (y_ref, wgu_st, wd_st, wgu_bf, wd_bf, sem) = rest[:2 * n_sub], rest[2 * n_sub:]
    blk_rows = x_ref.shape[0] // n_sub
    for h in range(n_sub):
        rows = pl.ds(h * blk_rows, blk_rows)
        _expert_block(pl.program_id(0) * n_sub + h, be_ref, nu_ref, first_ref, slot_ref, next_ref,
                      x_ref.at[rows], wgu_hbm, bias_refs[2 * h], wd_hbm, bias_refs[2 * h + 1], y_ref.at[rows],
                      wgu_st, wd_st, wgu_bf, wd_bf, sem)


def _expert_block(b, be_ref, nu_ref, first_ref, slot_ref, next_ref, x_ref, wgu_hbm, bgu_ref, wd_hbm, bd_ref, y_ref,
                  wgu_st, wd_st, wgu_bf, wd_bf, sem):
    def fetch(e, slot):
        return (pltpu.make_async_copy(wgu_hbm.at[e], wgu_st.at[slot], sem.at[slot, 0]),
                pltpu.make_async_copy(wd_hbm.at[e], wd_st.at[slot], sem.at[slot, 1]))

    @pl.when(b < nu_ref[0])
    def _():
        d_ff, d = wd_bf.shape

        @pl.when(first_ref[b] == 1)
        def _():
            slot = slot_ref[b]

            @pl.when(b == 0)
            def _():
                for cp in fetch(be_ref[b], slot):
                    cp.start()

            for cp in fetch(be_ref[b], slot):
                cp.wait()

            @pl.when(next_ref[b] >= 0)
            def _():
                for cp in fetch(next_ref[b], 1 - slot):
                    cp.start()

            wgu_bf[...] = wgu_st[slot].astype(BF16)
            wd_bf[...] = wd_st[slot].astype(BF16)

        x = _load_row_tiles(x_ref, MOE_BLOCK, d)
        h = jnp.dot(x.astype(BF16), wgu_bf[...], preferred_element_type=F32) + bgu_ref[...]
        g = jnp.minimum(h[:, :d_ff], SWIGLU_LIMIT)
        u = jnp.clip(h[:, d_ff:], -SWIGLU_LIMIT, SWIGLU_LIMIT)
        act = (u + 1.0) * (g * _sigmoid(SWIGLU_ALPHA * g))
        y = jnp.dot(act.astype(BF16), wd_bf[...], preferred_element_type=F32) + bd_ref[...]
        _store_row_tiles(y_ref, y)

    @pl.when(b >= nu_ref[0])
    def _():
        y_ref[...] = jnp.zeros(y_ref.shape, F32)


def _expert_call(xb_tiles, block_e, n_used, run_first, run_slot, run_next, wgu, bgu, wd, bd):
    n_exp, d, two_ff = wgu.shape
    d_ff = wd.shape[1]
    n_sub = EXPERT_BLOCKS_PER_STEP
    step_rows = n_sub * MOE_BLOCK * d // LANES
    n_steps = xb_tiles.shape[0] // step_rows
    assert n_steps * step_rows == xb_tiles.shape[0]
    any_spec = pl.BlockSpec(memory_space=pl.ANY)
    bias_specs, biases = [], []
    for h in range(n_sub):
        per_e = lambda s, be, *_, h=h: (be[s * n_sub + h], 0, 0)
        bias_specs += [pl.BlockSpec((None, 1, two_ff), per_e), pl.BlockSpec((None, 1, d), per_e)]
        biases += [bgu.reshape(n_exp, 1, two_ff), bd.reshape(n_exp, 1, d)]
    grid_spec = pltpu.PrefetchScalarGridSpec(
        num_scalar_prefetch=5,
        grid=(n_steps,),
        in_specs=[pl.BlockSpec((step_rows, LANES), lambda s, *_: (s, 0)), any_spec, any_spec] + bias_specs,
        out_specs=pl.BlockSpec((step_rows, LANES), lambda s, *_: (s, 0)),
        scratch_shapes=[pltpu.VMEM((2, d, two_ff), F32), pltpu.VMEM((2, d_ff, d), F32),
                        pltpu.VMEM((d, two_ff), BF16), pltpu.VMEM((d_ff, d), BF16),
                        pltpu.SemaphoreType.DMA((2, 2))],
    )
    return pl.pallas_call(
        _expert_kernel,
        grid_spec=grid_spec,
        out_shape=jax.ShapeDtypeStruct(xb_tiles.shape, F32),
        compiler_params=pltpu.CompilerParams(dimension_semantics=("arbitrary",), vmem_limit_bytes=VMEM_LIMIT),
        name="moe_experts",
    )(block_e, n_used, run_first, run_slot, run_next, xb_tiles, wgu, wd, *biases)


def _combine_kernel(xmid_ref, gate_ref, dest_ref, dest_next_ref, gfin_ref, yb_ref, outp_ref, outs_ref,
                    ybuf, ot_buf, sem, *, n_tiles_p):
    i = pl.program_id(0)
    n_tiles = pl.num_programs(0)
    tm, d = xmid_ref.shape
    pitch = d // LANES

    def start_tile(dref, slot):
        def start_rows(r, c):
            for k in range(TOP_K):
                src = dref[0, k * tm + r]
                pltpu.make_async_copy(yb_ref.at[pl.ds(pl.multiple_of(src * pitch, pitch), pitch)],
                                      ybuf.at[slot, k, pl.ds(pl.multiple_of(r * pitch, pitch), pitch)],
                                      sem.at[slot]).start(priority=k % 2)
            return c

        lax.fori_loop(0, tm, start_rows, 0, unroll=ROW_DMA_UNROLL)

    @pl.when(i == 0)
    def _():
        start_tile(dest_ref, 0)

    slot = i % 2

    @pl.when(i + 1 < n_tiles)
    def _():
        start_tile(dest_next_ref, 1 - slot)

    pltpu.make_async_copy(ybuf.at[slot], ybuf.at[slot], sem.at[slot]).wait()
    gate = gate_ref[...]
    y = xmid_ref[...]
    for k in range(TOP_K):
        y = y + gate[:, k:k + 1] * _load_row_tiles(ybuf.at[slot, k], tm, d)
    out = _rms(y, gfin_ref[...])

    @pl.when(i < n_tiles_p)
    def _():
        _store_seq_major(outp_ref, ot_buf, out)

    @pl.when(i >= n_tiles_p)
    def _():
        outs_ref[...] = out


def _combine_call(x_mid, gate, dest_tiles, g_final, yb, *, tm, n_p, n_tiles_p):
    n_tok, d = x_mid.shape
    n_tiles = n_tok // tm
    tt = tm // n_p
    dest_spec = lambda off: pl.BlockSpec((None, 1, tm * TOP_K),
                                         lambda i: (jnp.minimum(i + off, n_tiles - 1), 0, 0),
                                         memory_space=pltpu.SMEM)
    return pl.pallas_call(
        functools.partial(_combine_kernel, n_tiles_p=n_tiles_p),
        grid=(n_tiles,),
        in_specs=[pl.BlockSpec((tm, d), lambda i: (i, 0)),
                  pl.BlockSpec((tm, LANES), lambda i: (i, 0)),
                  dest_spec(0), dest_spec(1),
                  pl.BlockSpec((1, d), lambda i: (0, 0)),
                  pl.BlockSpec(memory_space=pl.ANY)],
        out_specs=(pl.BlockSpec((n_p, tt, d), lambda i: (0, jnp.minimum(i, n_tiles_p - 1), 0)),
                   pl.BlockSpec((tm, d), lambda i: (jnp.maximum(i - n_tiles_p, 0), 0))),
        out_shape=(jax.ShapeDtypeStruct((n_p, n_tiles_p * tt, d), F32),
                   jax.ShapeDtypeStruct(((n_tiles - n_tiles_p) * tm, d), F32)),
        scratch_shapes=[pltpu.VMEM((2, TOP_K, tm * d // LANES, LANES), F32),
                        pltpu.VMEM((d // LANES, tm, LANES), F32),
                        pltpu.SemaphoreType.DMA((2,))],
        compiler_params=pltpu.CompilerParams(dimension_semantics=("arbitrary",), vmem_limit_bytes=VMEM_LIMIT),
        name="moe_combine",
    )(x_mid, gate, dest_tiles, dest_tiles, g_final, yb)


def _to_time_major(x, seq_block):
    n_seqs, t, d = x.shape
    n_sb = n_seqs // seq_block
    return x.reshape(n_sb, seq_block, t, d).transpose(0, 2, 1, 3).reshape(n_sb * t, seq_block, d)


def _from_time_major(x, n_seqs, seq_block):
    d = x.shape[-1]
    n_sb = n_seqs // seq_block
    t = x.size // (n_seqs * d)
    return x.reshape(n_sb, t, seq_block, d).transpose(0, 2, 1, 3).reshape(n_seqs, t, d)


def _layer(xp, xs, state_a, state_b, p, norm_final_g):
    n_p, t_p, d = xp.shape
    n_s, t_s, _ = xs.shape
    n_exp = p["wgu"].shape[0]
    tm = TOKEN_TILE
    sb = tm // t_s
    n_tiles_p = n_p * t_p // tm

    (x_mid, xn2, route, gate, cnt, newa_p, newb_p, newa_s, newb_s) = _mixer_call(
        xp, _to_time_major(xs, sb), _to_time_major(state_a, sb), _to_time_major(state_b, sb),
        p, tt_p=tm // n_p, tt_s=t_s)
    n_tok = x_mid.shape[0]
    n_tiles = n_tok // tm

    counts = cnt[:, 0].astype(jnp.int32)
    padded = (counts + MOE_BLOCK - 1) // MOE_BLOCK * MOE_BLOCK
    pad_end = jnp.cumsum(padded)
    pad_start = pad_end - padded
    is_e = route[:TOP_K, :, None] == jnp.arange(n_exp, dtype=jnp.int32)
    dest = jnp.sum(jnp.where(is_e, pad_start, 0), axis=-1) + route[TOP_K:]
    dest_tiles = dest.reshape(TOP_K, n_tiles, tm).transpose(1, 0, 2).reshape(n_tiles, 1, TOP_K * tm)
    n_blocks = -(-(n_tok * TOP_K) // MOE_BLOCK) + n_exp
    n_blocks = -(-n_blocks // EXPERT_BLOCKS_PER_STEP) * EXPERT_BLOCKS_PER_STEP
    n_used = (pad_end[-1] // MOE_BLOCK).astype(jnp.int32)
    blk_start = jnp.minimum(jnp.arange(n_blocks, dtype=jnp.int32) * MOE_BLOCK, pad_end[-1] - 1)
    block_e = jnp.minimum(jnp.sum(blk_start[:, None] >= pad_end[None, :], axis=1), n_exp - 1).astype(jnp.int32)
    n_tail = n_exp + EXPERT_BLOCKS_PER_STEP - 1
    last_blocks = jnp.arange(n_blocks - n_tail, n_blocks, dtype=jnp.int32)
    zero_start = jnp.concatenate([jnp.where(padded > 0, pad_end - MOE_BLOCK, -1),
                                  jnp.where(last_blocks >= n_used, last_blocks * MOE_BLOCK, -1)]).astype(jnp.int32)

    xb = _dispatch_call(xn2, dest_tiles, zero_start, n_blocks * MOE_BLOCK, tm=tm, d=d)
    blk_ids = jnp.arange(n_blocks, dtype=jnp.int32)
    prev_e = jnp.concatenate([jnp.full((1,), -1, jnp.int32), block_e[:-1]])
    run_first = ((block_e != prev_e) & (blk_ids < n_used)).astype(jnp.int32)
    run_slot = ((jnp.cumsum(run_first) - 1) % 2).astype(jnp.int32)
    e_ids = jnp.arange(n_exp, dtype=jnp.int32)
    later = lax.cummin(jnp.where(padded > 0, e_ids, n_exp), axis=0, reverse=True)
    next_of = jnp.concatenate([later[1:], jnp.full((1,), n_exp, jnp.int32)])
    next_of = jnp.where(next_of >= n_exp, -1, next_of)
    run_next = jnp.sum(jnp.where(block_e[:, None] == e_ids[None, :], next_of[None, :], 0), axis=1).astype(jnp.int32)

    yb = _expert_call(xb, block_e, n_used.reshape(1), run_first, run_slot, run_next,
                      p["wgu"], p["bgu"], p["wd"], p["bd"])
    y_p, y_s = _combine_call(x_mid, gate, dest_tiles, norm_final_g.reshape(1, d), yb, tm=tm, n_p=n_p,
                             n_tiles_p=n_tiles_p)

    return (y_p, _from_time_major(y_s, n_s, sb),
            _from_time_major(newa_p, n_p, n_p), _from_time_major(newb_p, n_p, n_p),
            _from_time_major(newa_s, n_s, sb), _from_time_major(newb_s, n_s, sb))


def _prep_params(l, norm_mix_g, w_in, b_gates, conv_a_w, conv_a_b, w_a_out, conv_b_w, conv_b_b, ln_b_g,
                 ln_b_b, w_b_out, w_o, norm_ffn_g, w_router, b_router, w_gu, b_gu, w_down, b_down):
    row = lambda v: v.reshape(1, -1)
    taps = lambda w: jnp.broadcast_to(w[:, None, :], (w.shape[0], SUBLANES, w.shape[1]))
    wr_t = w_router[l].T
    wr_hi = wr_t.astype(BF16)
    return dict(
        gmix=row(norm_mix_g[l]), win=w_in[l].astype(BF16), bg=b_gates[l],
        caw=taps(conv_a_w[l]), cab=row(conv_a_b[l]), waout=w_a_out[l].astype(BF16),
        cbw=taps(conv_b_w[l]), cbb=row(conv_b_b[l]), lng=row(ln_b_g[l]), lnb=row(ln_b_b[l]),
        wbout=w_b_out[l].astype(BF16), wo=w_o[l].astype(BF16), gffn=row(norm_ffn_g[l]),
        wrh=wr_hi, wrl=(wr_t - wr_hi.astype(F32)).astype(BF16), br=b_router[l].reshape(-1, 1),
        wgu=w_gu[l], bgu=b_gu[l], wd=w_down[l], bd=b_down[l])


def kernel(x_prompt, x_sample, state_conv_a, state_conv_b, norm_mix_g, w_in, b_gates, conv_a_w, conv_a_b, w_a_out, conv_b_w, conv_b_b, ln_b_g, ln_b_b, w_b_out, w_o, norm_ffn_g, w_router, b_router, w_gu, b_gu, w_down, b_down, norm_final_g):
    depth = w_in.shape[0]
    assert depth == 1, "the final norm is fused into the last layer's combine call"
    p = _prep_params(0, norm_mix_g, w_in, b_gates, conv_a_w, conv_a_b, w_a_out, conv_b_w, conv_b_b, ln_b_g,
                     ln_b_b, w_b_out, w_o, norm_ffn_g, w_router, b_router, w_gu, b_gu, w_down, b_down)
    y_p, y_s, na_p, nb_p, na_s, nb_s = _layer(x_prompt, x_sample, state_conv_a[0], state_conv_b[0], p,
                                               norm_final_g)
    return (y_p, y_s, na_p[None], nb_p[None], na_s[None], nb_s[None])
```

```python
import functools

import jax
import jax.numpy as jnp
from jax import lax
from jax.experimental import pallas as pl
from jax.experimental.pallas import tpu as pltpu

EPS = 1e-5
SWIGLU_ALPHA = 1.702
SWIGLU_LIMIT = 7.0
TOP_K = 4
MOE_BLOCK = 256
TOKEN_TILE = 256
LANES = 128
SUBLANES = 8
VMEM_LIMIT = 60 * 1024 * 1024
CONV_OUT_BLOCK = 16
CONV_FIR_LEVELS = 2
ROW_DMA_UNROLL = 8
CONV_LANES = 128
TWO_PHASE_MIN_TAPS = 8
EXPERT_BLOCKS_PER_STEP = 4

F32 = jnp.float32
BF16 = jnp.bfloat16


def _sigmoid(v):
    return 1.0 / (1.0 + jnp.exp(-v))


def _store_row_tiles(ref, value):
    n, d = value.shape
    pitch = d // LANES
    for c in range(pitch):
        ref[pl.ds(c, n, stride=pitch), :] = value[:, c * LANES:(c + 1) * LANES]


def _load_row_tiles(ref, n, d):
    pitch = d // LANES
    return jnp.concatenate([ref[pl.ds(c, n, stride=pitch), :] for c in range(pitch)], axis=1)


def _rms(v, g):
    return v * lax.rsqrt(jnp.mean(v * v, axis=-1, keepdims=True) + EPS) * g


def _mac(acc, term):
    return term if acc is None else acc + term


def _fir(e, w, n_out, width, levels):
    if levels == 0 or width < TWO_PHASE_MIN_TAPS:
        acc = [None] * n_out
        for k in range(width):
            wk = w(k)
            for j in range(n_out):
                acc[j] = _mac(acc[j], wk * e(j + k))
        return acc
    half, n0, n1 = (n_out + 1) // 2, (width + 1) // 2, width // 2
    s_cache, h_cache = {}, {}

    def s(i):
        if i not in s_cache:
            s_cache[i] = e(2 * i + 1) + e(2 * i + 2)
        return s_cache[i]

    def hs(j):
        if j not in h_cache:
            h_cache[j] = w(2 * j) + w(2 * j + 1) if j < n1 else w(2 * j)
        return h_cache[j]

    a = _fir(lambda i: e(2 * i), lambda j: w(2 * j), half + 1, n0, levels - 1)
    b = _fir(lambda i: e(2 * i + 1), lambda j: w(2 * j + 1), half, n1, levels - 1)
    c = _fir(s, hs, half, n0, levels - 1)
    out = []
    for m in range(half):
        out += [a[m] + b[m], c[m] - a[m + 1] - b[m]]
    return out[:n_out]


def _fir_reach(n_out, width, levels):
    seen = [0]

    def e(i):
        seen[0] = max(seen[0], i)
        return 0.0

    _fir(e, lambda k: 1.0, n_out, width, levels)
    return seen[0]


def _conv_block(n_out):
    return min(CONV_OUT_BLOCK, n_out)


def _conv_spare_rows(n_out, width):
    nb = _conv_block(n_out)
    return _fir_reach(nb, width, CONV_FIR_LEVELS) - (nb + width - 2)


def _causal_conv(ext_ref, w_ref, bias, out_ref, n_out):
    width = w_ref.shape[0]
    n_seq, d = ext_ref.shape[1:]
    nb = _conv_block(n_out)
    assert n_out % nb == 0 and n_seq % SUBLANES == 0 and d % CONV_LANES == 0
    assert ext_ref.shape[0] >= n_out + width - 1 + _conv_spare_rows(n_out, width)

    def block(tb, carry):
        t0 = tb * nb
        for sg in range(n_seq // SUBLANES):
            rows = pl.ds(sg * SUBLANES, SUBLANES)
            for lc in range(d // CONV_LANES):
                lanes = pl.ds(lc * CONV_LANES, CONV_LANES)
                loaded, taps = {}, {}

                def e(i):
                    if i not in loaded:
                        loaded[i] = ext_ref[t0 + i, rows, lanes]
                    return loaded[i]

                def w(k):
                    if k not in taps:
                        taps[k] = w_ref[k, :, lanes]
                    return taps[k]

                for j, a in enumerate(_fir(e, w, nb, width, CONV_FIR_LEVELS)):
                    row0 = pl.multiple_of((t0 + j) * n_seq + sg * SUBLANES, SUBLANES)
                    out_ref[pl.ds(row0, SUBLANES), lanes] = a + bias[:, lc * CONV_LANES:(lc + 1) * CONV_LANES]
        return carry

    lax.fori_loop(0, n_out // nb, block, 0)
    return out_ref[...]


_MIXER_CONSTS = ("gmix", "win", "bg", "caw", "cab", "waout", "cbw", "cbb", "lng", "lnb", "wbout", "wo",
                 "gffn", "wrh", "wrl", "br")


def _time_major_rows(src_ref, buf, t0, tt):
    n_seq, _, d = src_ref.shape
    for s in range(n_seq):
        for c in range(d // LANES):
            buf[c, pl.ds(s, tt, stride=n_seq), :] = src_ref[s, t0:t0 + tt, c * LANES:(c + 1) * LANES]
    return jnp.concatenate([buf[c] for c in range(d // LANES)], axis=1)


def _store_seq_major(dst_ref, buf, value):
    n_seq, tt, d = dst_ref.shape
    for c in range(d // LANES):
        buf[c] = value[:, c * LANES:(c + 1) * LANES]
    for s in range(n_seq):
        for c in range(d // LANES):
            dst_ref[s, :, c * LANES:(c + 1) * LANES] = buf[c, pl.ds(s, tt, stride=n_seq), :]


def _mixer_tile(x, exta, extb, c, xmid_ref, xn2_ref, route_ref, gate_ref, cnt_ref, cnt_acc, conv_buf):
    rows, d = x.shape
    n_seq = exta.shape[1]
    tt = rows // n_seq
    w_a = c["caw"].shape[0]
    w_b = c["cbw"].shape[0]

    xn = _rms(x, c["gmix"][...]).astype(BF16)

    def proj(g):
        return jnp.dot(xn, c["win"][:, g * d:(g + 1) * d], preferred_element_type=F32)

    exta[w_a - 1:w_a - 1 + tt] = (proj(1) * proj(2)).reshape(tt, n_seq, d)
    conv_a = _causal_conv(exta, c["caw"], c["cab"][...], conv_buf.at[0], tt)
    y_a = jnp.dot((proj(0) * conv_a).astype(BF16), c["waout"][...], preferred_element_type=F32)
    extb[w_b - 1:w_b - 1 + tt] = (proj(3) * _sigmoid(proj(4))).reshape(tt, n_seq, d)
    conv_b = _causal_conv(extb, c["cbw"], c["cbb"][...], conv_buf.at[1], tt)
    mu = jnp.mean(conv_b, axis=-1, keepdims=True)
    cen = conv_b - mu
    ln = cen * lax.rsqrt(jnp.mean(cen * cen, axis=-1, keepdims=True) + EPS) * c["lng"][...] + c["lnb"][...]
    y_b = jnp.dot((ln * _sigmoid(ln)).astype(BF16), c["wbout"][...], preferred_element_type=F32)

    bg = c["bg"]
    merged = _sigmoid(proj(5) + bg[0:1, :]) * y_a + _sigmoid(proj(6) + bg[1:2, :]) * y_b
    x_mid = x + jnp.dot(merged.astype(BF16), c["wo"][...], preferred_element_type=F32)
    xmid_ref[...] = x_mid
    xn2 = _rms(x_mid, c["gffn"][...])
    _store_row_tiles(xn2_ref, xn2)

    n_exp = c["wrh"].shape[0]
    nt = (((1,), (1,)), ((), ()))
    x_hi = xn2.astype(BF16)
    x_lo = (xn2 - x_hi.astype(F32)).astype(BF16)
    logits = (lax.dot_general(c["wrh"][...], x_hi, nt, preferred_element_type=F32)
              + lax.dot_general(c["wrh"][...], x_lo, nt, preferred_element_type=F32)
              + lax.dot_general(c["wrl"][...], x_hi, nt, preferred_element_type=F32)) + c["br"][...]
    e_io = lax.broadcasted_iota(jnp.int32, (n_exp, rows), 0)
    work = logits
    top_v, top_i = [], []
    for _ in range(TOP_K):
        m = jnp.max(work, axis=0, keepdims=True)
        idx = jnp.min(jnp.where(work == m, e_io, n_exp), axis=0, keepdims=True)
        top_v.append(m)
        top_i.append(idx)
        work = jnp.where(e_io == idx, -jnp.inf, work)
    ex = [jnp.exp(v - top_v[0]) for v in top_v]
    den = ex[0] + ex[1] + ex[2] + ex[3]
    onehot = jnp.zeros((n_exp, rows), F32)
    for idx in top_i:
        onehot = onehot + (e_io == idx).astype(F32)
    r_io = lax.broadcasted_iota(jnp.int32, (rows, rows), 0)
    c_io = lax.broadcasted_iota(jnp.int32, (rows, rows), 1)
    before = (r_io < c_io).astype(BF16)
    cnt = cnt_acc[...]
    prefix = jnp.dot(onehot.astype(BF16), before, preferred_element_type=F32) + cnt[:, 0:1]
    pos = [jnp.sum(jnp.where(e_io == idx, prefix, 0.0), axis=0, keepdims=True) for idx in top_i]
    route_ref[...] = jnp.concatenate(top_i + [p.astype(jnp.int32) for p in pos], axis=0)
    gates = jnp.concatenate([e / den for e in ex] + [jnp.zeros((LANES - TOP_K, rows), F32)], axis=0)
    gate_ref[...] = jnp.transpose(gates)
    cnt = cnt + jnp.sum(onehot, axis=1, keepdims=True)
    cnt_acc[...] = cnt
    cnt_ref[...] = cnt


def _mixer_kernel(*refs, n_steps_p):
    n_c = len(_MIXER_CONSTS)
    xp_ref, xs_ref, hsa_ref, hsb_ref = refs[:4]
    c = dict(zip(_MIXER_CONSTS, refs[4:4 + n_c]))
    (xmid_ref, xn2_ref, route_ref, gate_ref, cnt_ref,
     newa_p_ref, newb_p_ref, newa_s_ref, newb_s_ref) = refs[4 + n_c:13 + n_c]
    exta_p, extb_p, exta_s, extb_s, cnt_acc, conv_buf, xt_buf, sem = refs[13 + n_c:]
    i = pl.program_id(0)
    tt_p = xp_ref.shape[1]
    tt_s = xs_ref.shape[0]
    w_a = c["caw"].shape[0]
    w_b = c["cbw"].shape[0]
    tile_args = (c, xmid_ref, xn2_ref, route_ref, gate_ref, cnt_ref, cnt_acc, conv_buf)

    @pl.when(i == 0)
    def _():
        exta_p[0:w_a - 1] = jnp.zeros((w_a - 1,) + exta_p.shape[1:], F32)
        extb_p[0:w_b - 1] = jnp.zeros((w_b - 1,) + extb_p.shape[1:], F32)
        for ext, tt, w in ((exta_p, tt_p, w_a), (extb_p, tt_p, w_b), (exta_s, tt_s, w_a), (extb_s, tt_s, w_b)):
            if ext.shape[0] > tt + w - 1:
                ext[tt + w - 1:] = jnp.zeros((ext.shape[0] - (tt + w - 1),) + ext.shape[1:], F32)
        cnt_acc[...] = jnp.zeros_like(cnt_acc)

    @pl.when(i < n_steps_p)
    def _():
        _mixer_tile(_time_major_rows(xp_ref, xt_buf, 0, tt_p), exta_p, extb_p, *tile_args)
        exta_p[0:w_a - 1] = exta_p[tt_p:tt_p + w_a - 1]
        extb_p[0:w_b - 1] = extb_p[tt_p:tt_p + w_b - 1]

        @pl.when(i == n_steps_p - 1)
        def _():
            cp_a = pltpu.make_async_copy(exta_p.at[pl.ds(0, w_a - 1)], newa_p_ref, sem.at[0])
            cp_b = pltpu.make_async_copy(extb_p.at[pl.ds(0, w_b - 1)], newb_p_ref, sem.at[1])
            cp_a.start()
            cp_b.start()
            cp_a.wait()
            cp_b.wait()

    @pl.when(i >= n_steps_p)
    def _():
        q = i - n_steps_p
        in_a = pltpu.make_async_copy(hsa_ref.at[pl.ds(q * (w_a - 1), w_a - 1)],
                                     exta_s.at[pl.ds(0, w_a - 1)], sem.at[0])
        in_b = pltpu.make_async_copy(hsb_ref.at[pl.ds(q * (w_b - 1), w_b - 1)],
                                     extb_s.at[pl.ds(0, w_b - 1)], sem.at[1])
        in_a.start()
        in_b.start()
        in_a.wait()
        in_b.wait()
        _mixer_tile(xs_ref[...].reshape(TOKEN_TILE, xs_ref.shape[-1]), exta_s, extb_s, *tile_args)
        out_a = pltpu.make_async_copy(exta_s.at[pl.ds(tt_s, w_a - 1)],
                                      newa_s_ref.at[pl.ds(q * (w_a - 1), w_a - 1)], sem.at[0])
        out_b = pltpu.make_async_copy(extb_s.at[pl.ds(tt_s, w_b - 1)],
                                      newb_s_ref.at[pl.ds(q * (w_b - 1), w_b - 1)], sem.at[1])
        out_a.start()
        out_b.start()
        out_a.wait()
        out_b.wait()


def _mixer_call(xp, xs_tm, hs_a, hs_b, params, *, tt_p, tt_s):
    n_p, t_p, d = xp.shape
    sb = xs_tm.shape[1]
    w_a = params["caw"].shape[0]
    w_b = params["cbw"].shape[0]
    n_exp = params["wrh"].shape[0]
    assert tt_p * n_p == tt_s * sb == TOKEN_TILE and n_exp % SUBLANES == 0
    rows = TOKEN_TILE
    n_steps_p = t_p // tt_p
    n_steps_s = xs_tm.shape[0] // tt_s
    assert n_steps_p * tt_p == t_p and n_steps_s * tt_s == xs_tm.shape[0]
    n_steps = n_steps_p + n_steps_s
    n_tok = n_steps * rows
    pitch = d // LANES
    consts = [params[n] for n in _MIXER_CONSTS]
    ext_rows = lambda tt, w: tt + w - 1 + _conv_spare_rows(tt, w)
    const_spec = lambda a: pl.BlockSpec(a.shape, lambda i, _nd=a.ndim: (0,) * _nd, pipeline_mode=pl.Buffered(1))
    any_spec = pl.BlockSpec(memory_space=pl.ANY)
    in_specs = [
        pl.BlockSpec((n_p, tt_p, d), lambda i: (0, jnp.minimum(i, n_steps_p - 1), 0)),
        pl.BlockSpec((tt_s, sb, d), lambda i: (jnp.maximum(i - n_steps_p, 0), 0, 0)),
        any_spec, any_spec,
    ] + [const_spec(a) for a in consts]
    out_shape = (
        jax.ShapeDtypeStruct((n_tok, d), F32),
        jax.ShapeDtypeStruct((n_tok * pitch, LANES), F32),
        jax.ShapeDtypeStruct((2 * TOP_K, n_tok), jnp.int32),
        jax.ShapeDtypeStruct((n_tok, LANES), F32),
        jax.ShapeDtypeStruct((n_exp, LANES), F32),
        jax.ShapeDtypeStruct((w_a - 1, n_p, d), F32),
        jax.ShapeDtypeStruct((w_b - 1, n_p, d), F32),
        jax.ShapeDtypeStruct(hs_a.shape, F32),
        jax.ShapeDtypeStruct(hs_b.shape, F32),
    )
    out_specs = (
        pl.BlockSpec((rows, d), lambda i: (i, 0)),
        pl.BlockSpec((rows * pitch, LANES), lambda i: (i, 0)),
        pl.BlockSpec((2 * TOP_K, rows), lambda i: (0, i)),
        pl.BlockSpec((rows, LANES), lambda i: (i, 0)),
        pl.BlockSpec((n_exp, LANES), lambda i: (0, 0)),
        any_spec, any_spec, any_spec, any_spec,
    )
    return pl.pallas_call(
        functools.partial(_mixer_kernel, n_steps_p=n_steps_p),
        grid=(n_steps,),
        in_specs=in_specs,
        out_specs=out_specs,
        out_shape=out_shape,
        scratch_shapes=[pltpu.VMEM((ext_rows(tt_p, w_a), n_p, d), F32),
                        pltpu.VMEM((ext_rows(tt_p, w_b), n_p, d), F32),
                        pltpu.VMEM((ext_rows(tt_s, w_a), sb, d), F32),
                        pltpu.VMEM((ext_rows(tt_s, w_b), sb, d), F32),
                        pltpu.VMEM((n_exp, LANES), F32),
                        pltpu.VMEM((2, rows, d), F32),
                        pltpu.VMEM((pitch, rows, LANES), F32),
                        pltpu.SemaphoreType.DMA((2,))],
        compiler_params=pltpu.CompilerParams(dimension_semantics=("arbitrary",), vmem_limit_bytes=VMEM_LIMIT),
        name="mixer_router",
    )(xp, xs_tm, hs_a, hs_b, *consts)


def _dispatch_kernel(zero_ref, x_hbm, dest_ref, xb_ref, xbuf, zero_buf, in_sem, out_sem, zsem, *, pitch):
    i = pl.program_id(0)
    n_tiles = pl.num_programs(0)
    n_buf, tile_rows = xbuf.shape[:2]
    tm = tile_rows // pitch
    n_zero = zero_ref.shape[0]
    blk_rows = zero_buf.shape[0]
    n_all = tm * TOP_K * pitch

    def fetch(t):
        start = t * tile_rows if isinstance(t, int) else pl.multiple_of(t * tile_rows, tile_rows)
        return pltpu.make_async_copy(x_hbm.at[pl.ds(start, tile_rows)], xbuf.at[t % n_buf], in_sem.at[t % n_buf])

    def wait_scatters(t):
        pltpu.make_async_copy(xb_ref.at[pl.ds(0, n_all)], xb_ref.at[pl.ds(0, n_all)], out_sem.at[t % 2]).wait()

    @pl.when(i == 0)
    def _():
        fetch(0).start()
        zero_buf[...] = jnp.zeros_like(zero_buf)

        def zcopy(e):
            start = pl.multiple_of(jnp.maximum(zero_ref[e], 0) * pitch, blk_rows)
            return pltpu.make_async_copy(zero_buf, xb_ref.at[pl.ds(start, blk_rows)], zsem)

        def start(e, c):
            @pl.when(zero_ref[e] >= 0)
            def _():
                zcopy(e).start()
            return c

        def wait(e, c):
            @pl.when(zero_ref[e] >= 0)
            def _():
                zcopy(e).wait()
            return c

        lax.fori_loop(0, n_zero, start, 0)
        lax.fori_loop(0, n_zero, wait, 0)

    @pl.when(i + 1 < n_tiles)
    def _():
        fetch(i + 1).start()

    fetch(i).wait()
    slot = i % n_buf

    def start_rows(r, c):
        src = xbuf.at[slot, pl.ds(pl.multiple_of(r * pitch, pitch), pitch)]
        for k in range(TOP_K):
            dst = dest_ref[0, k * tm + r]
            pltpu.make_async_copy(src, xb_ref.at[pl.ds(pl.multiple_of(dst * pitch, pitch), pitch)],
                                  out_sem.at[i % 2]).start(priority=k % 2)
        return c

    lax.fori_loop(0, tm, start_rows, 0, unroll=ROW_DMA_UNROLL)

    @pl.when(i > 0)
    def _():
        wait_scatters(i - 1)

    @pl.when(i == n_tiles - 1)
    def _():
        wait_scatters(i)


def _dispatch_call(xn2_tiles, dest_tiles, zero_start, n_rows, *, tm, d):
    pitch = d // LANES
    n_tiles = xn2_tiles.shape[0] // (tm * pitch)
    grid_spec = pltpu.PrefetchScalarGridSpec(
        num_scalar_prefetch=1,
        grid=(n_tiles,),
        in_specs=[pl.BlockSpec(memory_space=pl.ANY),
                  pl.BlockSpec((None, 1, tm * TOP_K), lambda i, z: (i, 0, 0), memory_space=pltpu.SMEM)],
        out_specs=pl.BlockSpec(memory_space=pl.ANY),
        scratch_shapes=[pltpu.VMEM((3, tm * pitch, LANES), F32),
                        pltpu.VMEM((MOE_BLOCK * pitch, LANES), F32),
                        pltpu.SemaphoreType.DMA((3,)), pltpu.SemaphoreType.DMA((2,)),
                        pltpu.SemaphoreType.DMA],
    )
    return pl.pallas_call(
        functools.partial(_dispatch_kernel, pitch=pitch),
        grid_spec=grid_spec,
        out_shape=jax.ShapeDtypeStruct((n_rows * pitch, LANES), F32),
        compiler_params=pltpu.CompilerParams(dimension_semantics=("arbitrary",)),
        name="moe_dispatch",
    )(zero_start, xn2_tiles, dest_tiles)


def _expert_kernel(be_ref, nu_ref, first_ref, slot_ref, next_ref, valid_ref, x_ref, wgu_hbm, wd_hbm, *rest):
    n_sub = EXPERT_BLOCKS_PER_STEP
    bias_refs, (y_ref, wgu_st, wd_st, wgu_bf, wd_bf, sem) = rest[:2 * n_sub], rest[2 * n_sub:]
    blk_rows = x_ref.shape[0] // n_sub
    for h in range(n_sub):
        rows = pl.ds(h * blk_rows, blk_rows)
        _expert_block(pl.program_id(0) * n_sub + h, be_ref, nu_ref, first_ref, slot_ref, next_ref, valid_ref,
                      x_ref.at[rows], wgu_hbm, bias_refs[2 * h], wd_hbm, bias_refs[2 * h + 1], y_ref.at[rows],
                      wgu_st, wd_st, wgu_bf, wd_bf, sem)


def _expert_block(b, be_ref, nu_ref, first_ref, slot_ref, next_ref, valid_ref, x_ref, wgu_hbm, bgu_ref, wd_hbm, bd_ref,
                  y_ref,
                  wgu_st, wd_st, wgu_bf, wd_bf, sem):
    def fetch(e, slot):
        return (pltpu.make_async_copy(wgu_hbm.at[e], wgu_st.at[slot], sem.at[slot, 0]),
                pltpu.make_async_copy(wd_hbm.at[e], wd_st.at[slot], sem.at[slot, 1]))

    @pl.when(b < nu_ref[0])
    def _():
        d_ff, d = wd_bf.shape

        @pl.when(first_ref[b] == 1)
        def _():
            slot = slot_ref[b]

            @pl.when(b == 0)
            def _():
                for cp in fetch(be_ref[b], slot):
                    cp.start()

            for cp in fetch(be_ref[b], slot):
                cp.wait()

            @pl.when(next_ref[b] >= 0)
            def _():
                for cp in fetch(next_ref[b], 1 - slot):
                    cp.start()

            wgu_bf[...] = wgu_st[slot].astype(BF16)
            wd_bf[...] = wd_st[slot].astype(BF16)

        pitch = d // LANES
        half = MOE_BLOCK // 2

        def mlp(n_rows):
            x = _load_row_tiles(x_ref.at[pl.ds(0, n_rows * pitch)], n_rows, d)
            h = jnp.dot(x.astype(BF16), wgu_bf[...], preferred_element_type=F32) + bgu_ref[...]
            g = jnp.minimum(h[:, :d_ff], SWIGLU_LIMIT)
            u = jnp.clip(h[:, d_ff:], -SWIGLU_LIMIT, SWIGLU_LIMIT)
            act = (u + 1.0) * (g * _sigmoid(SWIGLU_ALPHA * g))
            y = jnp.dot(act.astype(BF16), wd_bf[...], preferred_element_type=F32) + bd_ref[...]
            _store_row_tiles(y_ref.at[pl.ds(0, n_rows * pitch)], y)

        @pl.when(valid_ref[b] > half)
        def _():
            mlp(MOE_BLOCK)

        @pl.when(valid_ref[b] <= half)
        def _():
            mlp(half)
            y_ref[half * pitch:, :] = jnp.zeros((half * pitch, LANES), F32)

    @pl.when(b >= nu_ref[0])
    def _():
        y_ref[...] = jnp.zeros(y_ref.shape, F32)


def _expert_call(xb_tiles, block_e, n_used, run_first, run_slot, run_next, block_valid, wgu, bgu, wd, bd):
    n_exp, d, two_ff = wgu.shape
    d_ff = wd.shape[1]
    n_sub = EXPERT_BLOCKS_PER_STEP
    step_rows = n_sub * MOE_BLOCK * d // LANES
    n_steps = xb_tiles.shape[0] // step_rows
    assert n_steps * step_rows == xb_tiles.shape[0]
    any_spec = pl.BlockSpec(memory_space=pl.ANY)
    bias_specs, biases = [], []
    for h in range(n_sub):
        per_e = lambda s, be, *_, h=h: (be[s * n_sub + h], 0, 0)
        bias_specs += [pl.BlockSpec((None, 1, two_ff), per_e), pl.BlockSpec((None, 1, d), per_e)]
        biases += [bgu.reshape(n_exp, 1, two_ff), bd.reshape(n_exp, 1, d)]
    grid_spec = pltpu.PrefetchScalarGridSpec(
        num_scalar_prefetch=6,
        grid=(n_steps,),
        in_specs=[pl.BlockSpec((step_rows, LANES), lambda s, *_: (s, 0)), any_spec, any_spec] + bias_specs,
        out_specs=pl.BlockSpec((step_rows, LANES), lambda s, *_: (s, 0)),
        scratch_shapes=[pltpu.VMEM((2, d, two_ff), F32), pltpu.VMEM((2, d_ff, d), F32),
                        pltpu.VMEM((d, two_ff), BF16), pltpu.VMEM((d_ff, d), BF16),
                        pltpu.SemaphoreType.DMA((2, 2))],
    )
    return pl.pallas_call(
        _expert_kernel,
        grid_spec=grid_spec,
        out_shape=jax.ShapeDtypeStruct(xb_tiles.shape, F32),
        compiler_params=pltpu.CompilerParams(dimension_semantics=("arbitrary",), vmem_limit_bytes=VMEM_LIMIT),
        name="moe_experts",
    )(block_e, n_used, run_first, run_slot, run_next, block_valid, xb_tiles, wgu, wd, *biases)


def _combine_kernel(xmid_ref, gate_ref, dest_ref, dest_next_ref, gfin_ref, yb_ref, outp_ref, outs_ref,
                    ybuf, ot_buf, sem, *, n_tiles_p):
    i = pl.program_id(0)
    n_tiles = pl.num_programs(0)
    tm, d = xmid_ref.shape
    pitch = d // LANES

    def start_tile(dref, slot):
        def start_rows(r, c):
            for k in range(TOP_K):
                src = dref[0, k * tm + r]
                pltpu.make_async_copy(yb_ref.at[pl.ds(pl.multiple_of(src * pitch, pitch), pitch)],
                                      ybuf.at[slot, k, pl.ds(pl.multiple_of(r * pitch, pitch), pitch)],
                                      sem.at[slot]).start(priority=k % 2)
            return c

        lax.fori_loop(0, tm, start_rows, 0, unroll=ROW_DMA_UNROLL)

    @pl.when(i == 0)
    def _():
        start_tile(dest_ref, 0)

    slot = i % 2

    @pl.when(i + 1 < n_tiles)
    def _():
        start_tile(dest_next_ref, 1 - slot)

    pltpu.make_async_copy(ybuf.at[slot], ybuf.at[slot], sem.at[slot]).wait()
    gate = gate_ref[...]
    y = xmid_ref[...]
    for k in range(TOP_K):
        y = y + gate[:, k:k + 1] * _load_row_tiles(ybuf.at[slot, k], tm, d)
    out = _rms(y, gfin_ref[...])

    @pl.when(i < n_tiles_p)
    def _():
        _store_seq_major(outp_ref, ot_buf, out)

    @pl.when(i >= n_tiles_p)
    def _():
        outs_ref[...] = out


def _combine_call(x_mid, gate, dest_tiles, g_final, yb, *, tm, n_p, n_tiles_p):
    n_tok, d = x_mid.shape
    n_tiles = n_tok // tm
    tt = tm // n_p
    dest_spec = lambda off: pl.BlockSpec((None, 1, tm * TOP_K),
                                         lambda i: (jnp.minimum(i + off, n_tiles - 1), 0, 0),
                                         memory_space=pltpu.SMEM)
    return pl.pallas_call(
        functools.partial(_combine_kernel, n_tiles_p=n_tiles_p),
        grid=(n_tiles,),
        in_specs=[pl.BlockSpec((tm, d), lambda i: (i, 0)),
                  pl.BlockSpec((tm, LANES), lambda i: (i, 0)),
                  dest_spec(0), dest_spec(1),
                  pl.BlockSpec((1, d), lambda i: (0, 0)),
                  pl.BlockSpec(memory_space=pl.ANY)],
        out_specs=(pl.BlockSpec((n_p, tt, d), lambda i: (0, jnp.minimum(i, n_tiles_p - 1), 0)),
                   pl.BlockSpec((tm, d), lambda i: (jnp.maximum(i - n_tiles_p, 0), 0))),
        out_shape=(jax.ShapeDtypeStruct((n_p, n_tiles_p * tt, d), F32),
                   jax.ShapeDtypeStruct(((n_tiles - n_tiles_p) * tm, d), F32)),
        scratch_shapes=[pltpu.VMEM((2, TOP_K, tm * d // LANES, LANES), F32),
                        pltpu.VMEM((d // LANES, tm, LANES), F32),
                        pltpu.SemaphoreType.DMA((2,))],
        compiler_params=pltpu.CompilerParams(dimension_semantics=("arbitrary",), vmem_limit_bytes=VMEM_LIMIT),
        name="moe_combine",
    )(x_mid, gate, dest_tiles, dest_tiles, g_final, yb)


def _to_time_major(x, seq_block):
    n_seqs, t, d = x.shape
    n_sb = n_seqs // seq_block
    return x.reshape(n_sb, seq_block, t, d).transpose(0, 2, 1, 3).reshape(n_sb * t, seq_block, d)


def _from_time_major(x, n_seqs, seq_block):
    d = x.shape[-1]
    n_sb = n_seqs // seq_block
    t = x.size // (n_seqs * d)
    return x.reshape(n_sb, t, seq_block, d).transpose(0, 2, 1, 3).reshape(n_seqs, t, d)


def _layer(xp, xs, state_a, state_b, p, norm_final_g):
    n_p, t_p, d = xp.shape
    n_s, t_s, _ = xs.shape
    n_exp = p["wgu"].shape[0]
    tm = TOKEN_TILE
    sb = tm // t_s
    n_tiles_p = n_p * t_p // tm

    (x_mid, xn2, route, gate, cnt, newa_p, newb_p, newa_s, newb_s) = _mixer_call(
        xp, _to_time_major(xs, sb), _to_time_major(state_a, sb), _to_time_major(state_b, sb),
        p, tt_p=tm // n_p, tt_s=t_s)
    n_tok = x_mid.shape[0]
    n_tiles = n_tok // tm

    counts = cnt[:, 0].astype(jnp.int32)
    padded = (counts + MOE_BLOCK - 1) // MOE_BLOCK * MOE_BLOCK
    pad_end = jnp.cumsum(padded)
    pad_start = pad_end - padded
    is_e = route[:TOP_K, :, None] == jnp.arange(n_exp, dtype=jnp.int32)
    dest = jnp.sum(jnp.where(is_e, pad_start, 0), axis=-1) + route[TOP_K:]
    dest_tiles = dest.reshape(TOP_K, n_tiles, tm).transpose(1, 0, 2).reshape(n_tiles, 1, TOP_K * tm)
    n_blocks = -(-(n_tok * TOP_K) // MOE_BLOCK) + n_exp
    n_blocks = -(-n_blocks // EXPERT_BLOCKS_PER_STEP) * EXPERT_BLOCKS_PER_STEP
    n_used = (pad_end[-1] // MOE_BLOCK).astype(jnp.int32)
    blk_start = jnp.minimum(jnp.arange(n_blocks, dtype=jnp.int32) * MOE_BLOCK, pad_end[-1] - 1)
    block_e = jnp.minimum(jnp.sum(blk_start[:, None] >= pad_end[None, :], axis=1), n_exp - 1).astype(jnp.int32)
    n_tail = n_exp + EXPERT_BLOCKS_PER_STEP - 1
    last_blocks = jnp.arange(n_blocks - n_tail, n_blocks, dtype=jnp.int32)
    zero_start = jnp.concatenate([jnp.where(padded > 0, pad_end - MOE_BLOCK, -1),
                                  jnp.where(last_blocks >= n_used, last_blocks * MOE_BLOCK, -1)]).astype(jnp.int32)

    xb = _dispatch_call(xn2, dest_tiles, zero_start, n_blocks * MOE_BLOCK, tm=tm, d=d)
    blk_ids = jnp.arange(n_blocks, dtype=jnp.int32)
    prev_e = jnp.concatenate([jnp.full((1,), -1, jnp.int32), block_e[:-1]])
    run_first = ((block_e != prev_e) & (blk_ids < n_used)).astype(jnp.int32)
    run_slot = ((jnp.cumsum(run_first) - 1) % 2).astype(jnp.int32)
    e_ids = jnp.arange(n_exp, dtype=jnp.int32)
    later = lax.cummin(jnp.where(padded > 0, e_ids, n_exp), axis=0, reverse=True)
    next_of = jnp.concatenate([later[1:], jnp.full((1,), n_exp, jnp.int32)])
    next_of = jnp.where(next_of >= n_exp, -1, next_of)
    run_next = jnp.sum(jnp.where(block_e[:, None] == e_ids[None, :], next_of[None, :], 0), axis=1).astype(jnp.int32)

    in_block = lambda v: jnp.sum(jnp.where(block_e[:, None] == e_ids[None, :], v[None, :], 0), axis=1)
    block_valid = jnp.clip(in_block(pad_start + counts) - blk_ids * MOE_BLOCK, 0, MOE_BLOCK).astype(jnp.int32)
    yb = _expert_call(xb, block_e, n_used.reshape(1), run_first, run_slot, run_next, block_valid,
                      p["wgu"], p["bgu"], p["wd"], p["bd"])
    y_p, y_s = _combine_call(x_mid, gate, dest_tiles, norm_final_g.reshape(1, d), yb, tm=tm, n_p=n_p,
                             n_tiles_p=n_tiles_p)

    return (y_p, _from_time_major(y_s, n_s, sb),
            _from_time_major(newa_p, n_p, n_p), _from_time_major(newb_p, n_p, n_p),
            _from_time_major(newa_s, n_s, sb), _from_time_major(newb_s, n_s, sb))


def _prep_params(l, norm_mix_g, w_in, b_gates, conv_a_w, conv_a_b, w_a_out, conv_b_w, conv_b_b, ln_b_g,
                 ln_b_b, w_b_out, w_o, norm_ffn_g, w_router, b_router, w_gu, b_gu, w_down, b_down):
    row = lambda v: v.reshape(1, -1)
    taps = lambda w: jnp.broadcast_to(w[:, None, :], (w.shape[0], SUBLANES, w.shape[1]))
    wr_t = w_router[l].T
    wr_hi = wr_t.astype(BF16)
    return dict(
        gmix=row(norm_mix_g[l]), win=w_in[l].astype(BF16), bg=b_gates[l],
        caw=taps(conv_a_w[l]), cab=row(conv_a_b[l]), waout=w_a_out[l].astype(BF16),
        cbw=taps(conv_b_w[l]), cbb=row(conv_b_b[l]), lng=row(ln_b_g[l]), lnb=row(ln_b_b[l]),
        wbout=w_b_out[l].astype(BF16), wo=w_o[l].astype(BF16), gffn=row(norm_ffn_g[l]),
        wrh=wr_hi, wrl=(wr_t - wr_hi.astype(F32)).astype(BF16), br=b_router[l].reshape(-1, 1),
        wgu=w_gu[l], bgu=b_gu[l], wd=w_down[l], bd=b_down[l])


def kernel(x_prompt, x_sample, state_conv_a, state_conv_b, norm_mix_g, w_in, b_gates, conv_a_w, conv_a_b, w_a_out, conv_b_w, conv_b_b, ln_b_g, ln_b_b, w_b_out, w_o, norm_ffn_g, w_router, b_router, w_gu, b_gu, w_down, b_down, norm_final_g):
    depth = w_in.shape[0]
    assert depth == 1, "the final norm is fused into the last layer's combine call"
    p = _prep_params(0, norm_mix_g, w_in, b_gates, conv_a_w, conv_a_b, w_a_out, conv_b_w, conv_b_b, ln_b_g,
                     ln_b_b, w_b_out, w_o, norm_ffn_g, w_router, b_router, w_gu, b_gu, w_down, b_down)
    y_p, y_s, na_p, nb_p, na_s, nb_s = _layer(x_prompt, x_sample, state_conv_a[0], state_conv_b[0], p,
                                               norm_final_g)
    return (y_p, y_s, na_p[None], nb_p[None], na_s[None], nb_s[None])
```

```python
import functools

import jax
import jax.numpy as jnp
from jax import lax
from jax.experimental import pallas as pl
from jax.experimental.pallas import tpu as pltpu

EPS = 1e-5
SWIGLU_ALPHA = 1.702
SWIGLU_LIMIT = 7.0
TOP_K = 4
MOE_BLOCK = 256
TOKEN_TILE = 256
LANES = 128
SUBLANES = 8
VMEM_LIMIT = 60 * 1024 * 1024
CONV_OUT_BLOCK = 32
CONV_FIR_LEVELS = 2
ROW_DMA_UNROLL = 8
CONV_LANES = 128
TWO_PHASE_MIN_TAPS = 8
EXPERT_BLOCKS_PER_STEP = 4

F32 = jnp.float32
BF16 = jnp.bfloat16


def _sigmoid(v):
    return 1.0 / (1.0 + jnp.exp(-v))


def _store_row_tiles(ref, value):
    n, d = value.shape
    pitch = d // LANES
    for c in range(pitch):
        ref[pl.ds(c, n, stride=pitch), :] = value[:, c * LANES:(c + 1) * LANES]


def _load_row_tiles(ref, n, d):
    pitch = d // LANES
    return jnp.concatenate([ref[pl.ds(c, n, stride=pitch), :] for c in range(pitch)], axis=1)


def _rms(v, g):
    return v * lax.rsqrt(jnp.mean(v * v, axis=-1, keepdims=True) + EPS) * g


def _mac(acc, term):
    return term if acc is None else acc + term


def _fir(e, w, n_out, width, levels):
    if levels == 0 or width < TWO_PHASE_MIN_TAPS:
        acc = [None] * n_out
        for k in range(width):
            wk = w(k)
            for j in range(n_out):
                acc[j] = _mac(acc[j], wk * e(j + k))
        return acc
    half, n0, n1 = (n_out + 1) // 2, (width + 1) // 2, width // 2
    s_cache, h_cache = {}, {}

    def s(i):
        if i not in s_cache:
            s_cache[i] = e(2 * i + 1) + e(2 * i + 2)
        return s_cache[i]

    def hs(j):
        if j not in h_cache:
            h_cache[j] = w(2 * j) + w(2 * j + 1) if j < n1 else w(2 * j)
        return h_cache[j]

    a = _fir(lambda i: e(2 * i), lambda j: w(2 * j), half + 1, n0, levels - 1)
    b = _fir(lambda i: e(2 * i + 1), lambda j: w(2 * j + 1), half, n1, levels - 1)
    c = _fir(s, hs, half, n0, levels - 1)
    out = []
    for m in range(half):
        out += [a[m] + b[m], c[m] - a[m + 1] - b[m]]
    return out[:n_out]


def _fir_reach(n_out, width, levels):
    seen = [0]

    def e(i):
        seen[0] = max(seen[0], i)
        return 0.0

    _fir(e, lambda k: 1.0, n_out, width, levels)
    return seen[0]


def _conv_block(n_out):
    return min(CONV_OUT_BLOCK, n_out)


def _conv_spare_rows(n_out, width):
    nb = _conv_block(n_out)
    return _fir_reach(nb, width, CONV_FIR_LEVELS) - (nb + width - 2)


def _causal_conv(ext_ref, w_ref, bias, out_ref, n_out):
    width = w_ref.shape[0]
    n_seq, d = ext_ref.shape[1:]
    nb = _conv_block(n_out)
    assert n_out % nb == 0 and n_seq % SUBLANES == 0 and d % CONV_LANES == 0
    assert ext_ref.shape[0] >= n_out + width - 1 + _conv_spare_rows(n_out, width)

    def block(tb, carry):
        t0 = tb * nb
        for sg in range(n_seq // SUBLANES):
            rows = pl.ds(sg * SUBLANES, SUBLANES)
            for lc in range(d // CONV_LANES):
                lanes = pl.ds(lc * CONV_LANES, CONV_LANES)
                loaded, taps = {}, {}

                def e(i):
                    if i not in loaded:
                        loaded[i] = ext_ref[t0 + i, rows, lanes]
                    return loaded[i]

                def w(k):
                    if k not in taps:
                        taps[k] = w_ref[k, :, lanes]
                    return taps[k]

                for j, a in enumerate(_fir(e, w, nb, width, CONV_FIR_LEVELS)):
                    row0 = pl.multiple_of((t0 + j) * n_seq + sg * SUBLANES, SUBLANES)
                    out_ref[pl.ds(row0, SUBLANES), lanes] = a + bias[:, lc * CONV_LANES:(lc + 1) * CONV_LANES]
        return carry

    lax.fori_loop(0, n_out // nb, block, 0)
    return out_ref[...]


_MIXER_CONSTS = ("gmix", "win", "bg", "caw", "cab", "waout", "cbw", "cbb", "lng", "lnb", "wbout", "wo",
                 "gffn", "wrh", "wrl", "br")


def _time_major_rows(src_ref, buf, t0, tt):
    n_seq, _, d = src_ref.shape
    for s in range(n_seq):
        for c in range(d // LANES):
            buf[c, pl.ds(s, tt, stride=n_seq), :] = src_ref[s, t0:t0 + tt, c * LANES:(c + 1) * LANES]
    return jnp.concatenate([buf[c] for c in range(d // LANES)], axis=1)


def _store_seq_major(dst_ref, buf, value):
    n_seq, tt, d = dst_ref.shape
    for c in range(d // LANES):
        buf[c] = value[:, c * LANES:(c + 1) * LANES]
    for s in range(n_seq):
        for c in range(d // LANES):
            dst_ref[s, :, c * LANES:(c + 1) * LANES] = buf[c, pl.ds(s, tt, stride=n_seq), :]


def _mixer_tile(x, exta, extb, c, xmid_ref, xn2_ref, route_ref, gate_ref, cnt_ref, cnt_acc, conv_buf):
    rows, d = x.shape
    n_seq = exta.shape[1]
    tt = rows // n_seq
    w_a = c["caw"].shape[0]
    w_b = c["cbw"].shape[0]

    xn = _rms(x, c["gmix"][...]).astype(BF16)

    def proj(g):
        return jnp.dot(xn, c["win"][:, g * d:(g + 1) * d], preferred_element_type=F32)

    exta[w_a - 1:w_a - 1 + tt] = (proj(1) * proj(2)).reshape(tt, n_seq, d)
    conv_a = _causal_conv(exta, c["caw"], c["cab"][...], conv_buf.at[0], tt)
    y_a = jnp.dot((proj(0) * conv_a).astype(BF16), c["waout"][...], preferred_element_type=F32)
    extb[w_b - 1:w_b - 1 + tt] = (proj(3) * _sigmoid(proj(4))).reshape(tt, n_seq, d)
    conv_b = _causal_conv(extb, c["cbw"], c["cbb"][...], conv_buf.at[1], tt)
    mu = jnp.mean(conv_b, axis=-1, keepdims=True)
    cen = conv_b - mu
    ln = cen * lax.rsqrt(jnp.mean(cen * cen, axis=-1, keepdims=True) + EPS) * c["lng"][...] + c["lnb"][...]
    y_b = jnp.dot((ln * _sigmoid(ln)).astype(BF16), c["wbout"][...], preferred_element_type=F32)

    bg = c["bg"]
    merged = _sigmoid(proj(5) + bg[0:1, :]) * y_a + _sigmoid(proj(6) + bg[1:2, :]) * y_b
    x_mid = x + jnp.dot(merged.astype(BF16), c["wo"][...], preferred_element_type=F32)
    xmid_ref[...] = x_mid
    xn2 = _rms(x_mid, c["gffn"][...])
    _store_row_tiles(xn2_ref, xn2)

    n_exp = c["wrh"].shape[0]
    nt = (((1,), (1,)), ((), ()))
    x_hi = xn2.astype(BF16)
    x_lo = (xn2 - x_hi.astype(F32)).astype(BF16)
    logits = (lax.dot_general(c["wrh"][...], x_hi, nt, preferred_element_type=F32)
              + lax.dot_general(c["wrh"][...], x_lo, nt, preferred_element_type=F32)
              + lax.dot_general(c["wrl"][...], x_hi, nt, preferred_element_type=F32)) + c["br"][...]
    e_io = lax.broadcasted_iota(jnp.int32, (n_exp, rows), 0)
    work = logits
    top_v, top_i = [], []
    for _ in range(TOP_K):
        m = jnp.max(work, axis=0, keepdims=True)
        idx = jnp.min(jnp.where(work == m, e_io, n_exp), axis=0, keepdims=True)
        top_v.append(m)
        top_i.append(idx)
        work = jnp.where(e_io == idx, -jnp.inf, work)
    ex = [jnp.exp(v - top_v[0]) for v in top_v]
    den = ex[0] + ex[1] + ex[2] + ex[3]
    onehot = jnp.zeros((n_exp, rows), F32)
    for idx in top_i:
        onehot = onehot + (e_io == idx).astype(F32)
    r_io = lax.broadcasted_iota(jnp.int32, (rows, rows), 0)
    c_io = lax.broadcasted_iota(jnp.int32, (rows, rows), 1)
    before = (r_io < c_io).astype(BF16)
    cnt = cnt_acc[...]
    prefix = jnp.dot(onehot.astype(BF16), before, preferred_element_type=F32) + cnt[:, 0:1]
    pos = [jnp.sum(jnp.where(e_io == idx, prefix, 0.0), axis=0, keepdims=True) for idx in top_i]
    route_ref[...] = jnp.concatenate(top_i + [p.astype(jnp.int32) for p in pos], axis=0)
    gates = jnp.concatenate([e / den for e in ex] + [jnp.zeros((LANES - TOP_K, rows), F32)], axis=0)
    gate_ref[...] = jnp.transpose(gates)
    cnt = cnt + jnp.sum(onehot, axis=1, keepdims=True)
    cnt_acc[...] = cnt
    cnt_ref[...] = cnt


def _mixer_kernel(*refs, n_steps_p):
    n_c = len(_MIXER_CONSTS)
    xp_ref, xs_ref, hsa_ref, hsb_ref = refs[:4]
    c = dict(zip(_MIXER_CONSTS, refs[4:4 + n_c]))
    (xmid_ref, xn2_ref, route_ref, gate_ref, cnt_ref,
     newa_p_ref, newb_p_ref, newa_s_ref, newb_s_ref) = refs[4 + n_c:13 + n_c]
    exta_p, extb_p, exta_s, extb_s, cnt_acc, conv_buf, xt_buf, sem = refs[13 + n_c:]
    i = pl.program_id(0)
    tt_p = xp_ref.shape[1]
    tt_s = xs_ref.shape[0]
    w_a = c["caw"].shape[0]
    w_b = c["cbw"].shape[0]
    tile_args = (c, xmid_ref, xn2_ref, route_ref, gate_ref, cnt_ref, cnt_acc, conv_buf)

    @pl.when(i == 0)
    def _():
        exta_p[0:w_a - 1] = jnp.zeros((w_a - 1,) + exta_p.shape[1:], F32)
        extb_p[0:w_b - 1] = jnp.zeros((w_b - 1,) + extb_p.shape[1:], F32)
        for ext, tt, w in ((exta_p, tt_p, w_a), (extb_p, tt_p, w_b), (exta_s, tt_s, w_a), (extb_s, tt_s, w_b)):
            if ext.shape[0] > tt + w - 1:
                ext[tt + w - 1:] = jnp.zeros((ext.shape[0] - (tt + w - 1),) + ext.shape[1:], F32)
        cnt_acc[...] = jnp.zeros_like(cnt_acc)

    @pl.when(i < n_steps_p)
    def _():
        _mixer_tile(_time_major_rows(xp_ref, xt_buf, 0, tt_p), exta_p, extb_p, *tile_args)
        exta_p[0:w_a - 1] = exta_p[tt_p:tt_p + w_a - 1]
        extb_p[0:w_b - 1] = extb_p[tt_p:tt_p + w_b - 1]

        @pl.when(i == n_steps_p - 1)
        def _():
            cp_a = pltpu.make_async_copy(exta_p.at[pl.ds(0, w_a - 1)], newa_p_ref, sem.at[0])
            cp_b = pltpu.make_async_copy(extb_p.at[pl.ds(0, w_b - 1)], newb_p_ref, sem.at[1])
            cp_a.start()
            cp_b.start()
            cp_a.wait()
            cp_b.wait()

    @pl.when(i >= n_steps_p)
    def _():
        q = i - n_steps_p
        in_a = pltpu.make_async_copy(hsa_ref.at[pl.ds(q * (w_a - 1), w_a - 1)],
                                     exta_s.at[pl.ds(0, w_a - 1)], sem.at[0])
        in_b = pltpu.make_async_copy(hsb_ref.at[pl.ds(q * (w_b - 1), w_b - 1)],
                                     extb_s.at[pl.ds(0, w_b - 1)], sem.at[1])
        in_a.start()
        in_b.start()
        in_a.wait()
        in_b.wait()
        _mixer_tile(xs_ref[...].reshape(TOKEN_TILE, xs_ref.shape[-1]), exta_s, extb_s, *tile_args)
        out_a = pltpu.make_async_copy(exta_s.at[pl.ds(tt_s, w_a - 1)],
                                      newa_s_ref.at[pl.ds(q * (w_a - 1), w_a - 1)], sem.at[0])
        out_b = pltpu.make_async_copy(extb_s.at[pl.ds(tt_s, w_b - 1)],
                                      newb_s_ref.at[pl.ds(q * (w_b - 1), w_b - 1)], sem.at[1])
        out_a.start()
        out_b.start()
        out_a.wait()
        out_b.wait()


def _mixer_call(xp, xs_tm, hs_a, hs_b, params, *, tt_p, tt_s):
    n_p, t_p, d = xp.shape
    sb = xs_tm.shape[1]
    w_a = params["caw"].shape[0]
    w_b = params["cbw"].shape[0]
    n_exp = params["wrh"].shape[0]
    assert tt_p * n_p == tt_s * sb == TOKEN_TILE and n_exp % SUBLANES == 0
    rows = TOKEN_TILE
    n_steps_p = t_p // tt_p
    n_steps_s = xs_tm.shape[0] // tt_s
    assert n_steps_p * tt_p == t_p and n_steps_s * tt_s == xs_tm.shape[0]
    n_steps = n_steps_p + n_steps_s
    n_tok = n_steps * rows
    pitch = d // LANES
    consts = [params[n] for n in _MIXER_CONSTS]
    ext_rows = lambda tt, w: tt + w - 1 + _conv_spare_rows(tt, w)
    const_spec = lambda a: pl.BlockSpec(a.shape, lambda i, _nd=a.ndim: (0,) * _nd, pipeline_mode=pl.Buffered(1))
    any_spec = pl.BlockSpec(memory_space=pl.ANY)
    in_specs = [
        pl.BlockSpec((n_p, tt_p, d), lambda i: (0, jnp.minimum(i, n_steps_p - 1), 0)),
        pl.BlockSpec((tt_s, sb, d), lambda i: (jnp.maximum(i - n_steps_p, 0), 0, 0)),
        any_spec, any_spec,
    ] + [const_spec(a) for a in consts]
    out_shape = (
        jax.ShapeDtypeStruct((n_tok, d), F32),
        jax.ShapeDtypeStruct((n_tok * pitch, LANES), F32),
        jax.ShapeDtypeStruct((2 * TOP_K, n_tok), jnp.int32),
        jax.ShapeDtypeStruct((n_tok, LANES), F32),
        jax.ShapeDtypeStruct((n_exp, LANES), F32),
        jax.ShapeDtypeStruct((w_a - 1, n_p, d), F32),
        jax.ShapeDtypeStruct((w_b - 1, n_p, d), F32),
        jax.ShapeDtypeStruct(hs_a.shape, F32),
        jax.ShapeDtypeStruct(hs_b.shape, F32),
    )
    out_specs = (
        pl.BlockSpec((rows, d), lambda i: (i, 0)),
        pl.BlockSpec((rows * pitch, LANES), lambda i: (i, 0)),
        pl.BlockSpec((2 * TOP_K, rows), lambda i: (0, i)),
        pl.BlockSpec((rows, LANES), lambda i: (i, 0)),
        pl.BlockSpec((n_exp, LANES), lambda i: (0, 0)),
        any_spec, any_spec, any_spec, any_spec,
    )
    return pl.pallas_call(
        functools.partial(_mixer_kernel, n_steps_p=n_steps_p),
        grid=(n_steps,),
        in_specs=in_specs,
        out_specs=out_specs,
        out_shape=out_shape,
        scratch_shapes=[pltpu.VMEM((ext_rows(tt_p, w_a), n_p, d), F32),
                        pltpu.VMEM((ext_rows(tt_p, w_b), n_p, d), F32),
                        pltpu.VMEM((ext_rows(tt_s, w_a), sb, d), F32),
                        pltpu.VMEM((ext_rows(tt_s, w_b), sb, d), F32),
                        pltpu.VMEM((n_exp, LANES), F32),
                        pltpu.VMEM((2, rows, d), F32),
                        pltpu.VMEM((pitch, rows, LANES), F32),
                        pltpu.SemaphoreType.DMA((2,))],
        compiler_params=pltpu.CompilerParams(dimension_semantics=("arbitrary",), vmem_limit_bytes=VMEM_LIMIT),
        name="mixer_router",
    )(xp, xs_tm, hs_a, hs_b, *consts)


def _dispatch_kernel(zero_ref, x_hbm, dest_ref, xb_ref, xbuf, zero_buf, in_sem, out_sem, zsem, *, pitch):
    i = pl.program_id(0)
    n_tiles = pl.num_programs(0)
    n_buf, tile_rows = xbuf.shape[:2]
    tm = tile_rows // pitch
    n_zero = zero_ref.shape[0]
    blk_rows = zero_buf.shape[0]
    n_all = tm * TOP_K * pitch

    def fetch(t):
        start = t * tile_rows if isinstance(t, int) else pl.multiple_of(t * tile_rows, tile_rows)
        return pltpu.make_async_copy(x_hbm.at[pl.ds(start, tile_rows)], xbuf.at[t % n_buf], in_sem.at[t % n_buf])

    def wait_scatters(t):
        pltpu.make_async_copy(xb_ref.at[pl.ds(0, n_all)], xb_ref.at[pl.ds(0, n_all)], out_sem.at[t % 2]).wait()

    @pl.when(i == 0)
    def _():
        fetch(0).start()
        zero_buf[...] = jnp.zeros_like(zero_buf)

        def zcopy(e):
            start = pl.multiple_of(jnp.maximum(zero_ref[e], 0) * pitch, blk_rows)
            return pltpu.make_async_copy(zero_buf, xb_ref.at[pl.ds(start, blk_rows)], zsem)

        def start(e, c):
            @pl.when(zero_ref[e] >= 0)
            def _():
                zcopy(e).start()
            return c

        def wait(e, c):
            @pl.when(zero_ref[e] >= 0)
            def _():
                zcopy(e).wait()
            return c

        lax.fori_loop(0, n_zero, start, 0)
        lax.fori_loop(0, n_zero, wait, 0)

    @pl.when(i + 1 < n_tiles)
    def _():
        fetch(i + 1).start()

    fetch(i).wait()
    slot = i % n_buf

    def start_rows(r, c):
        src = xbuf.at[slot, pl.ds(pl.multiple_of(r * pitch, pitch), pitch)]
        for k in range(TOP_K):
            dst = dest_ref[0, k * tm + r]
            pltpu.make_async_copy(src, xb_ref.at[pl.ds(pl.multiple_of(dst * pitch, pitch), pitch)],
                                  out_sem.at[i % 2]).start(priority=k % 2)
        return c

    lax.fori_loop(0, tm, start_rows, 0, unroll=ROW_DMA_UNROLL)

    @pl.when(i > 0)
    def _():
        wait_scatters(i - 1)

    @pl.when(i == n_tiles - 1)
    def _():
        wait_scatters(i)


def _dispatch_call(xn2_tiles, dest_tiles, zero_start, n_rows, *, tm, d):
    pitch = d // LANES
    n_tiles = xn2_tiles.shape[0] // (tm * pitch)
    grid_spec = pltpu.PrefetchScalarGridSpec(
        num_scalar_prefetch=1,
        grid=(n_tiles,),
        in_specs=[pl.BlockSpec(memory_space=pl.ANY),
                  pl.BlockSpec((None, 1, tm * TOP_K), lambda i, z: (i, 0, 0), memory_space=pltpu.SMEM)],
        out_specs=pl.BlockSpec(memory_space=pl.ANY),
        scratch_shapes=[pltpu.VMEM((3, tm * pitch, LANES), F32),
                        pltpu.VMEM((MOE_BLOCK * pitch, LANES), F32),
                        pltpu.SemaphoreType.DMA((3,)), pltpu.SemaphoreType.DMA((2,)),
                        pltpu.SemaphoreType.DMA],
    )
    return pl.pallas_call(
        functools.partial(_dispatch_kernel, pitch=pitch),
        grid_spec=grid_spec,
        out_shape=jax.ShapeDtypeStruct((n_rows * pitch, LANES), F32),
        compiler_params=pltpu.CompilerParams(dimension_semantics=("arbitrary",)),
        name="moe_dispatch",
    )(zero_start, xn2_tiles, dest_tiles)


def _expert_kernel(be_ref, nu_ref, first_ref, slot_ref, next_ref, x_ref, wgu_hbm, wd_hbm, *rest):
    n_sub = EXPERT_BLOCKS_PER_STEP
    bias_refs, (y_ref, wgu_st, wd_st, wgu_bf, wd_bf, sem) = rest[:2 * n_sub], rest[2 * n_sub:]
    blk_rows = x_ref.shape[0] // n_sub
    for h in range(n_sub):
        rows = pl.ds(h * blk_rows, blk_rows)
        _expert_block(pl.program_id(0) * n_sub + h, be_ref, nu_ref, first_ref, slot_ref, next_ref,
                      x_ref.at[rows], wgu_hbm, bias_refs[2 * h], wd_hbm, bias_refs[2 * h + 1], y_ref.at[rows],
                      wgu_st, wd_st, wgu_bf, wd_bf, sem)


def _expert_block(b, be_ref, nu_ref, first_ref, slot_ref, next_ref, x_ref, wgu_hbm, bgu_ref, wd_hbm, bd_ref, y_ref,
                  wgu_st, wd_st, wgu_bf, wd_bf, sem):
    def fetch(e, slot):
        return (pltpu.make_async_copy(wgu_hbm.at[e], wgu_st.at[slot], sem.at[slot, 0]),
                pltpu.make_async_copy(wd_hbm.at[e], wd_st.at[slot], sem.at[slot, 1]))

    @pl.when(b < nu_ref[0])
    def _():
        d_ff, d = wd_bf.shape

        @pl.when(first_ref[b] == 1)
        def _():
            slot = slot_ref[b]

            @pl.when(b == 0)
            def _():
                for cp in fetch(be_ref[b], slot):
                    cp.start()

            for cp in fetch(be_ref[b], slot):
                cp.wait()

            @pl.when(next_ref[b] >= 0)
            def _():
                for cp in fetch(next_ref[b], 1 - slot):
                    cp.start()

            wgu_bf[...] = wgu_st[slot].astype(BF16)
            wd_bf[...] = wd_st[slot].astype(BF16)

        x = _load_row_tiles(x_ref, MOE_BLOCK, d)
        h = jnp.dot(x.astype(BF16), wgu_bf[...], preferred_element_type=F32) + bgu_ref[...]
        g = jnp.minimum(h[:, :d_ff], SWIGLU_LIMIT)
        u = jnp.clip(h[:, d_ff:], -SWIGLU_LIMIT, SWIGLU_LIMIT)
        act = (u + 1.0) * (g * _sigmoid(SWIGLU_ALPHA * g))
        y = jnp.dot(act.astype(BF16), wd_bf[...], preferred_element_type=F32) + bd_ref[...]
        _store_row_tiles(y_ref, y)

    @pl.when(b >= nu_ref[0])
    def _():
        y_ref[...] = jnp.zeros(y_ref.shape, F32)


def _expert_call(xb_tiles, block_e, n_used, run_first, run_slot, run_next, wgu, bgu, wd, bd):
    n_exp, d, two_ff = wgu.shape
    d_ff = wd.shape[1]
    n_sub = EXPERT_BLOCKS_PER_STEP
    step_rows = n_sub * MOE_BLOCK * d // LANES
    n_steps = xb_tiles.shape[0] // step_rows
    assert n_steps * step_rows == xb_tiles.shape[0]
    any_spec = pl.BlockSpec(memory_space=pl.ANY)
    bias_specs, biases = [], []
    for h in range(n_sub):
        per_e = lambda s, be, *_, h=h: (be[s * n_sub + h], 0, 0)
        bias_specs += [pl.BlockSpec((None, 1, two_ff), per_e), pl.BlockSpec((None, 1, d), per_e)]
        biases += [bgu.reshape(n_exp, 1, two_ff), bd.reshape(n_exp, 1, d)]
    grid_spec = pltpu.PrefetchScalarGridSpec(
        num_scalar_prefetch=5,
        grid=(n_steps,),
        in_specs=[pl.BlockSpec((step_rows, LANES), lambda s, *_: (s, 0)), any_spec, any_spec] + bias_specs,
        out_specs=pl.BlockSpec((step_rows, LANES), lambda s, *_: (s, 0)),
        scratch_shapes=[pltpu.VMEM((2, d, two_ff), F32), pltpu.VMEM((2, d_ff, d), F32),
                        pltpu.VMEM((d, two_ff), BF16), pltpu.VMEM((d_ff, d), BF16),
                        pltpu.SemaphoreType.DMA((2, 2))],
    )
    return pl.pallas_call(
        _expert_kernel,
        grid_spec=grid_spec,
        out_shape=jax.ShapeDtypeStruct(xb_tiles.shape, F32),
        compiler_params=pltpu.CompilerParams(dimension_semantics=("arbitrary",), vmem_limit_bytes=VMEM_LIMIT),
        name="moe_experts",
    )(block_e, n_used, run_first, run_slot, run_next, xb_tiles, wgu, wd, *biases)


def _combine_kernel(xmid_ref, gate_ref, dest_ref, dest_next_ref, gfin_ref, yb_ref, outp_ref, outs_ref,
                    ybuf, ot_buf, sem, *, n_tiles_p):
    i = pl.program_id(0)
    n_tiles = pl.num_programs(0)
    tm, d = xmid_ref.shape
    pitch = d // LANES

    def start_tile(dref, slot):
        def start_rows(r, c):
            for k in range(TOP_K):
                src = dref[0, k * tm + r]
                pltpu.make_async_copy(yb_ref.at[pl.ds(pl.multiple_of(src * pitch, pitch), pitch)],
                                      ybuf.at[slot, k, pl.ds(pl.multiple_of(r * pitch, pitch), pitch)],
                                      sem.at[slot]).start(priority=k % 2)
            return c

        lax.fori_loop(0, tm, start_rows, 0, unroll=ROW_DMA_UNROLL)

    @pl.when(i == 0)
    def _():
        start_tile(dest_ref, 0)

    slot = i % 2

    @pl.when(i + 1 < n_tiles)
    def _():
        start_tile(dest_next_ref, 1 - slot)

    pltpu.make_async_copy(ybuf.at[slot], ybuf.at[slot], sem.at[slot]).wait()
    gate = gate_ref[...]
    y = xmid_ref[...]
    for k in range(TOP_K):
        y = y + gate[:, k:k + 1] * _load_row_tiles(ybuf.at[slot, k], tm, d)
    out = _rms(y, gfin_ref[...])

    @pl.when(i < n_tiles_p)
    def _():
        _store_seq_major(outp_ref, ot_buf, out)

    @pl.when(i >= n_tiles_p)
    def _():
        outs_ref[...] = out


def _combine_call(x_mid, gate, dest_tiles, g_final, yb, *, tm, n_p, n_tiles_p):
    n_tok, d = x_mid.shape
    n_tiles = n_tok // tm
    tt = tm // n_p
    dest_spec = lambda off: pl.BlockSpec((None, 1, tm * TOP_K),
                                         lambda i: (jnp.minimum(i + off, n_tiles - 1), 0, 0),
                                         memory_space=pltpu.SMEM)
    return pl.pallas_call(
        functools.partial(_combine_kernel, n_tiles_p=n_tiles_p),
        grid=(n_tiles,),
        in_specs=[pl.BlockSpec((tm, d), lambda i: (i, 0)),
                  pl.BlockSpec((tm, LANES), lambda i: (i, 0)),
                  dest_spec(0), dest_spec(1),
                  pl.BlockSpec((1, d), lambda i: (0, 0)),
                  pl.BlockSpec(memory_space=pl.ANY)],
        out_specs=(pl.BlockSpec((n_p, tt, d), lambda i: (0, jnp.minimum(i, n_tiles_p - 1), 0)),
                   pl.BlockSpec((tm, d), lambda i: (jnp.maximum(i - n_tiles_p, 0), 0))),
        out_shape=(jax.ShapeDtypeStruct((n_p, n_tiles_p * tt, d), F32),
                   jax.ShapeDtypeStruct(((n_tiles - n_tiles_p) * tm, d), F32)),
        scratch_shapes=[pltpu.VMEM((2, TOP_K, tm * d // LANES, LANES), F32),
                        pltpu.VMEM((d // LANES, tm, LANES), F32),
                        pltpu.SemaphoreType.DMA((2,))],
        compiler_params=pltpu.CompilerParams(dimension_semantics=("arbitrary",), vmem_limit_bytes=VMEM_LIMIT),
        name="moe_combine",
    )(x_mid, gate, dest_tiles, dest_tiles, g_final, yb)


def _to_time_major(x, seq_block):
    n_seqs, t, d = x.shape
    n_sb = n_seqs // seq_block
    return x.reshape(n_sb, seq_block, t, d).transpose(0, 2, 1, 3).reshape(n_sb * t, seq_block, d)


def _from_time_major(x, n_seqs, seq_block):
    d = x.shape[-1]
    n_sb = n_seqs // seq_block
    t = x.size // (n_seqs * d)
    return x.reshape(n_sb, t, seq_block, d).transpose(0, 2, 1, 3).reshape(n_seqs, t, d)


def _layer(xp, xs, state_a, state_b, p, norm_final_g):
    n_p, t_p, d = xp.shape
    n_s, t_s, _ = xs.shape
    n_exp = p["wgu"].shape[0]
    tm = TOKEN_TILE
    sb = tm // t_s
    n_tiles_p = n_p * t_p // tm

    (x_mid, xn2, route, gate, cnt, newa_p, newb_p, newa_s, newb_s) = _mixer_call(
        xp, _to_time_major(xs, sb), _to_time_major(state_a, sb), _to_time_major(state_b, sb),
        p, tt_p=tm // n_p, tt_s=t_s)
    n_tok = x_mid.shape[0]
    n_tiles = n_tok // tm

    counts = cnt[:, 0].astype(jnp.int32)
    padded = (counts + MOE_BLOCK - 1) // MOE_BLOCK * MOE_BLOCK
    pad_end = jnp.cumsum(padded)
    pad_start = pad_end - padded
    is_e = route[:TOP_K, :, None] == jnp.arange(n_exp, dtype=jnp.int32)
    dest = jnp.sum(jnp.where(is_e, pad_start, 0), axis=-1) + route[TOP_K:]
    dest_tiles = dest.reshape(TOP_K, n_tiles, tm).transpose(1, 0, 2).reshape(n_tiles, 1, TOP_K * tm)
    n_blocks = -(-(n_tok * TOP_K) // MOE_BLOCK) + n_exp
    n_blocks = -(-n_blocks // EXPERT_BLOCKS_PER_STEP) * EXPERT_BLOCKS_PER_STEP
    n_used = (pad_end[-1] // MOE_BLOCK).astype(jnp.int32)
    blk_start = jnp.minimum(jnp.arange(n_blocks, dtype=jnp.int32) * MOE_BLOCK, pad_end[-1] - 1)
    block_e = jnp.minimum(jnp.sum(blk_start[:, None] >= pad_end[None, :], axis=1), n_exp - 1).astype(jnp.int32)
    n_tail = n_exp + EXPERT_BLOCKS_PER_STEP - 1
    last_blocks = jnp.arange(n_blocks - n_tail, n_blocks, dtype=jnp.int32)
    zero_start = jnp.concatenate([jnp.where(padded > 0, pad_end - MOE_BLOCK, -1),
                                  jnp.where(last_blocks >= n_used, last_blocks * MOE_BLOCK, -1)]).astype(jnp.int32)

    xb = _dispatch_call(xn2, dest_tiles, zero_start, n_blocks * MOE_BLOCK, tm=tm, d=d)
    blk_ids = jnp.arange(n_blocks, dtype=jnp.int32)
    prev_e = jnp.concatenate([jnp.full((1,), -1, jnp.int32), block_e[:-1]])
    run_first = ((block_e != prev_e) & (blk_ids < n_used)).astype(jnp.int32)
    run_slot = ((jnp.cumsum(run_first) - 1) % 2).astype(jnp.int32)
    e_ids = jnp.arange(n_exp, dtype=jnp.int32)
    later = lax.cummin(jnp.where(padded > 0, e_ids, n_exp), axis=0, reverse=True)
    next_of = jnp.concatenate([later[1:], jnp.full((1,), n_exp, jnp.int32)])
    next_of = jnp.where(next_of >= n_exp, -1, next_of)
    run_next = jnp.sum(jnp.where(block_e[:, None] == e_ids[None, :], next_of[None, :], 0), axis=1).astype(jnp.int32)

    yb = _expert_call(xb, block_e, n_used.reshape(1), run_first, run_slot, run_next,
                      p["wgu"], p["bgu"], p["wd"], p["bd"])
    y_p, y_s = _combine_call(x_mid, gate, dest_tiles, norm_final_g.reshape(1, d), yb, tm=tm, n_p=n_p,
                             n_tiles_p=n_tiles_p)

    return (y_p, _from_time_major(y_s, n_s, sb),
            _from_time_major(newa_p, n_p, n_p), _from_time_major(newb_p, n_p, n_p),
            _from_time_major(newa_s, n_s, sb), _from_time_major(newb_s, n_s, sb))


def _prep_params(l, norm_mix_g, w_in, b_gates, conv_a_w, conv_a_b, w_a_out, conv_b_w, conv_b_b, ln_b_g,
                 ln_b_b, w_b_out, w_o, norm_ffn_g, w_router, b_router, w_gu, b_gu, w_down, b_down):
    row = lambda v: v.reshape(1, -1)
    taps = lambda w: jnp.broadcast_to(w[:, None, :], (w.shape[0], SUBLANES, w.shape[1]))
    wr_t = w_router[l].T
    wr_hi = wr_t.astype(BF16)
    return dict(
        gmix=row(norm_mix_g[l]), win=w_in[l].astype(BF16), bg=b_gates[l],
        caw=taps(conv_a_w[l]), cab=row(conv_a_b[l]), waout=w_a_out[l].astype(BF16),
        cbw=taps(conv_b_w[l]), cbb=row(conv_b_b[l]), lng=row(ln_b_g[l]), lnb=row(ln_b_b[l]),
        wbout=w_b_out[l].astype(BF16), wo=w_o[l].astype(BF16), gffn=row(norm_ffn_g[l]),
        wrh=wr_hi, wrl=(wr_t - wr_hi.astype(F32)).astype(BF16), br=b_router[l].reshape(-1, 1),
        wgu=w_gu[l], bgu=b_gu[l], wd=w_down[l], bd=b_down[l])


def kernel(x_prompt, x_sample, state_conv_a, state_conv_b, norm_mix_g, w_in, b_gates, conv_a_w, conv_a_b, w_a_out, conv_b_w, conv_b_b, ln_b_g, ln_b_b, w_b_out, w_o, norm_ffn_g, w_router, b_router, w_gu, b_gu, w_down, b_down, norm_final_g):
    depth = w_in.shape[0]
    assert depth == 1, "the final norm is fused into the last layer's combine call"
    p = _prep_params(0, norm_mix_g, w_in, b_gates, conv_a_w, conv_a_b, w_a_out, conv_b_w, conv_b_b, ln_b_g,
                     ln_b_b, w_b_out, w_o, norm_ffn_g, w_router, b_router, w_gu, b_gu, w_down, b_down)
    y_p, y_s, na_p, nb_p, na_s, nb_s = _layer(x_prompt, x_sample, state_conv_a[0], state_conv_b[0], p,
                                               norm_final_g)
    return (y_p, y_s, na_p[None], nb_p[None], na_s[None], nb_s[None])
```
